```python
import math
import jax, jax.numpy as jnp
from jax import lax
import numpy as np

D_MODEL = 1024
BATCH = 4
SEQ = 4096
DEPTH = 1

N_MEM = 256
EPS = 1e-6
GLA_HEADS = 4
GLA_DK = D_MODEL // (2 * GLA_HEADS)
GLA_DV = D_MODEL // GLA_HEADS
GLA_RANK = 16
GLA_GATE_NORM = 16.0
GLA_CHUNK = 64
DSA_HEADS = 8
DSA_KV_HEADS = 2
DSA_HD = D_MODEL // DSA_HEADS
IDX_HEADS = 8
IDX_DIM = 64
TOPK_MAX = 256
Q_BLOCK = 128
REL_BUCKETS = 32
REL_MAX_DIST = 128
X_HEADS = 4
X_HD = D_MODEL // X_HEADS
N_BRANCH = 3

SPLIT_SIZES = (
    GLA_HEADS * GLA_DK,
    GLA_HEADS * GLA_DK,
    GLA_HEADS * GLA_DV,
    GLA_RANK,
    GLA_HEADS * GLA_DV,
    DSA_HEADS * DSA_HD,
    DSA_KV_HEADS * DSA_HD,
    DSA_KV_HEADS * DSA_HD,
    IDX_HEADS * IDX_DIM,
    IDX_DIM,
    IDX_HEADS,
    DSA_HEADS * DSA_HD,
    X_HEADS * X_HD,
    X_HEADS * X_HD,
    N_BRANCH * D_MODEL,
)
W_IN_COLS = 11352

kernel_name = "hybrid_gla_dsa_memxattn_gated_merge"


def rmsnorm(x, g):
    xf = x.astype(jnp.float32)
    y = xf * lax.rsqrt(jnp.mean(xf * xf, axis=-1, keepdims=True) + EPS)
    return (y * g.astype(jnp.float32)).astype(x.dtype)


def split_cols(u):
    offsets = np.cumsum(np.array(SPLIT_SIZES))[:-1].tolist()
    return jnp.split(u, offsets, axis=-1)


def t5_bucket(dist):
    max_exact = REL_BUCKETS // 2
    d = jnp.maximum(dist, 1).astype(jnp.float32)
    large = max_exact + (jnp.log(d / max_exact) / math.log(REL_MAX_DIST / max_exact)
                         * (REL_BUCKETS - max_exact)).astype(jnp.int32)
    large = jnp.minimum(large, REL_BUCKETS - 1)
    return jnp.where(dist < max_exact, dist, large)


def gla_mixer(q, k, v, a_low, z, w_a_up, b_a, g_head):
    B, L, _ = q.shape
    H, dk, dv, C = GLA_HEADS, GLA_DK, GLA_DV, GLA_CHUNK
    nc = L // C
    q = q.reshape(B, nc, C, H, dk) * (dk ** -0.5)
    k = k.reshape(B, nc, C, H, dk)
    v = v.reshape(B, nc, C, H, dv)
    log_a = jax.nn.log_sigmoid((a_low @ w_a_up + b_a).astype(jnp.float32)) / GLA_GATE_NORM
    b = jnp.cumsum(log_a.reshape(B, nc, C, H, dk), axis=2)
    b_last = b[:, :, -1:]
    q_d = q * jnp.exp(b).astype(q.dtype)
    k_d = k * jnp.exp(-b).astype(k.dtype)
    k_t = k * jnp.exp(b_last - b).astype(k.dtype)
    decay = jnp.exp(b_last[:, :, 0]).astype(q.dtype)
    causal = jnp.tril(jnp.ones((C, C), dtype=bool))
    att = jnp.einsum('bnthd,bnshd->bnhts', q_d, k_d)
    att = jnp.where(causal, att, jnp.zeros_like(att))
    o_intra = jnp.einsum('bnhts,bnshv->bnthv', att, v)

    def step(state, inp):
        qc, kc, vc, dc = inp
        o = jnp.einsum('bthd,bhdv->bthv', qc, state)
        state = dc[..., None] * state + jnp.einsum('bthd,bthv->bhdv', kc, vc)
        return state, o

    s0 = jnp.zeros((B, H, dk, dv), q.dtype)
    xs = (q_d.transpose(1, 0, 2, 3, 4), k_t.transpose(1, 0, 2, 3, 4),
          v.transpose(1, 0, 2, 3, 4), decay.transpose(1, 0, 2, 3))
    _, o_inter = lax.scan(step, s0, xs)
    o = o_intra + o_inter.transpose(1, 0, 2, 3, 4)
    o = rmsnorm(o.reshape(B, L, H, dv), g_head)
    return o.reshape(B, L, H * dv) * jax.nn.silu(z)


def dsa_mixer(q, k, v, iq, ik, iw, z, rel_bias):
    B, L, _ = q.shape
    topk = min(TOPK_MAX, L // 4)
    nb = L // Q_BLOCK
    KV = DSA_KV_HEADS
    G = DSA_HEADS // KV
    q = q.reshape(B, nb, Q_BLOCK, KV, G, DSA_HD) * (DSA_HD ** -0.5)
    k = k.reshape(B, L, KV, DSA_HD)
    v = v.reshape(B, L, KV, DSA_HD)
    iq = iq.reshape(B, nb, Q_BLOCK, IDX_HEADS, IDX_DIM)
    iw = iw.reshape(B, nb, Q_BLOCK, IDX_HEADS) * (IDX_HEADS ** -0.5) * (IDX_DIM ** -0.5)
    ik_f = ik.astype(jnp.float32)
    key_pos = jnp.arange(L, dtype=jnp.int32)

    def block(inp):
        qb, iqb, iwb, start = inp
        q_pos = start + jnp.arange(Q_BLOCK, dtype=jnp.int32)
        s = jax.nn.relu(jnp.einsum('bqhd,bsd->bqhs', iqb.astype(jnp.float32), ik_f))
        score = jnp.einsum('bqhs,bqh->bqs', s, iwb.astype(jnp.float32))
        visible = key_pos[None, :] <= q_pos[:, None]
        score = jnp.where(visible[None], score, -jnp.inf)
        _, idx = lax.top_k(score, topk)
        valid = idx <= q_pos[None, :, None]
        k_sel = jax.vmap(lambda kb, ib: kb[ib])(k, idx)
        v_sel = jax.vmap(lambda vb, ib: vb[ib])(v, idx)
        logits = jnp.einsum('bqcgd,bqncd->bqcgn', qb, k_sel).astype(jnp.float32)
        bucket = t5_bucket(jnp.maximum(q_pos[None, :, None] - idx, 0))
        bias = rel_bias[bucket].reshape(B, Q_BLOCK, topk, KV, G).transpose(0, 1, 3, 4, 2)
        logits = logits + bias.astype(jnp.float32)
        logits = jnp.where(valid[:, :, None, None, :], logits, -1e30)
        p = jax.nn.softmax(logits, axis=-1).astype(v.dtype)
        o = jnp.einsum('bqcgn,bqncd->bqcgd', p, v_sel)
        return o.reshape(B, Q_BLOCK, DSA_HEADS * DSA_HD)

    starts = jnp.arange(nb, dtype=jnp.int32) * Q_BLOCK
    xs = (q.transpose(1, 0, 2, 3, 4, 5), iq.transpose(1, 0, 2, 3, 4),
          iw.transpose(1, 0, 2, 3), starts)
    o = lax.map(block, xs)
    o = o.transpose(1, 0, 2, 3).reshape(B, L, DSA_HEADS * DSA_HD)
    return o * jax.nn.silu(z)


def cross_mixer(q, z, mem_n, w_mem_kv):
    B, L, _ = q.shape
    M = mem_n.shape[1]
    mk, mv = jnp.split(mem_n @ w_mem_kv, 2, axis=-1)
    mk = mk.reshape(B, M, X_HEADS, X_HD)
    mv = mv.reshape(B, M, X_HEADS, X_HD)
    q = q.reshape(B, L, X_HEADS, X_HD) * (X_HD ** -0.5)
    logits = jnp.einsum('bthd,bmhd->bhtm', q, mk).astype(jnp.float32)
    p = jax.nn.softmax(logits, axis=-1).astype(mv.dtype)
    o = jnp.einsum('bhtm,bmhd->bthd', p, mv).reshape(B, L, X_HEADS * X_HD)
    return o * jax.nn.silu(z)


def setup_inputs(seed: int = 0) -> dict:
    key = jax.random.key(seed)
    ks = jax.random.split(key, 16)
    D = D_MODEL

    def w(k, shape, fan_in):
        return jax.random.normal(k, shape, jnp.float32) * (fan_in ** -0.5)

    def gain(k, shape):
        return 1.0 + 0.05 * jax.random.normal(k, shape, jnp.float32)

    return {
        "x": jax.random.normal(ks[0], (BATCH, SEQ, D), jnp.float32),
        "mem": jax.random.normal(ks[1], (BATCH, N_MEM, D), jnp.float32),
        "g_pre": gain(ks[2], (DEPTH, D)),
        "g_post": gain(ks[3], (DEPTH, D)),
        "g_mem": gain(ks[4], (DEPTH, D)),
        "w_in": w(ks[5], (DEPTH, D, W_IN_COLS), D),
        "w_gla_a_up": w(ks[6], (DEPTH, GLA_RANK, GLA_HEADS * GLA_DK), GLA_RANK),
        "b_gla_a": 0.1 * jax.random.normal(ks[7], (DEPTH, GLA_HEADS * GLA_DK), jnp.float32),
        "g_gla": gain(ks[8], (DEPTH, GLA_DV)),
        "rel_bias": 0.5 * jax.random.normal(ks[9], (REL_BUCKETS, DSA_HEADS), jnp.float32),
        "w_mem_kv": w(ks[10], (DEPTH, D, 2 * X_HEADS * X_HD), D),
        "w_gla_out": w(ks[11], (DEPTH, GLA_HEADS * GLA_DV, D), GLA_HEADS * GLA_DV),
        "w_dsa_out": w(ks[12], (DEPTH, DSA_HEADS * DSA_HD, D), DSA_HEADS * DSA_HD),
        "w_x_out": w(ks[13], (DEPTH, X_HEADS * X_HD, D), X_HEADS * X_HD),
        "w_o": w(ks[14], (DEPTH, D, D), D),
    }


def reference(x, mem, g_pre, g_post, g_mem, w_in, w_gla_a_up, b_gla_a, g_gla,
              rel_bias, w_mem_kv, w_gla_out, w_dsa_out, w_x_out, w_o):
    for i in range(DEPTH):
        h = rmsnorm(x, g_pre[i])
        (gq, gk, gv, ga, gz, dq, dk, dv, iq, ik, iw, dz, xq, xz, gates) = split_cols(h @ w_in[i])
        y_gla = gla_mixer(gq, gk, gv, ga, gz, w_gla_a_up[i], b_gla_a[i], g_gla[i])
        y_dsa = dsa_mixer(dq, dk, dv, iq, ik, iw, dz, rel_bias)
        y_mem = cross_mixer(xq, xz, rmsnorm(mem, g_mem[i]), w_mem_kv[i])
        s_gla, s_dsa, s_mem = jnp.split(jax.nn.sigmoid(gates), N_BRANCH, axis=-1)
        merged = (s_gla * (y_gla @ w_gla_out[i]) + s_dsa * (y_dsa @ w_dsa_out[i])
                  + s_mem * (y_mem @ w_x_out[i]))
        x = x + rmsnorm(merged @ w_o[i], g_post[i])
    return x
```

```python
import functools
import math

import jax
import jax.numpy as jnp
import numpy as np
from jax import lax
from jax.experimental import pallas as pl
from jax.experimental.pallas import tpu as pltpu

D_MODEL = 1024
N_MEM = 256
EPS = 1e-6
GLA_HEADS = 4
GLA_DK = 128
GLA_DV = 256
GLA_RANK = 16
GLA_GATE_NORM = 16.0
GLA_CHUNK = 64
DSA_HEADS = 8
DSA_KV_HEADS = 2
DSA_GROUPS = DSA_HEADS // DSA_KV_HEADS
DSA_HD = 128
IDX_HEADS = 8
IDX_DIM = 64
TOPK_MAX = 256
Q_BLOCK = 128
REL_BUCKETS = 32
REL_MAX_DIST = 128
X_HEADS = 4
X_HD = 256

SPLIT_SIZES = (512, 512, 1024, 16, 1024, 1024, 256, 256, 512, 64, 8, 1024, 1024, 1024, 3072)

LANES = 128
SUBLANES = 8

MXU_DTYPE = jnp.bfloat16
ACT_DTYPE = jnp.bfloat16

U_COLS = 11264
COL_GQ, COL_GK, COL_GV, COL_GZ = 0, 512, 1024, 2048
COL_DQ, COL_DZ, COL_XQ, COL_XZ = 3072, 4096, 5120, 6144
COL_SG, COL_SD, COL_SM = 7168, 8192, 9216
COL_DK, COL_DV, COL_IQ = 10240, 10496, 10752
SMALL_GA, SMALL_IW = 0, 16

MASK_NEG = -1e30
INT_MIN = -(2 ** 31)
VMEM_LIMIT = 56 * 1024 * 1024


def _nt_dot(a, b):
    return lax.dot_general(a, b, (((1,), (1,)), ((), ())),
                           preferred_element_type=jnp.float32)


def _tn_dot(a, b):
    return lax.dot_general(a, b, (((0,), (0,)), ((), ())),
                           preferred_element_type=jnp.float32)


def _dot(a, b):
    return jnp.dot(a, b, preferred_element_type=jnp.float32)


def _silu(z):
    return z * jax.nn.sigmoid(z)


def _proj_kernel(x_ref, g_ref, w_ref, ws_ref, cs_ref, u_ref, ikd_ref, sm_ref, h_ref):
    @pl.when(pl.program_id(1) == 0)
    def _():
        xf = x_ref[...]
        y = xf * lax.rsqrt(jnp.mean(xf * xf, axis=-1, keepdims=True) + EPS)
        hb = (y * g_ref[...]).astype(MXU_DTYPE)
        h_ref[...] = hb
        r = _dot(hb, ws_ref[...])
        ikd_ref[...] = r[:, :LANES].astype(ikd_ref.dtype)
        sm_ref[...] = r[:, LANES:]

    acc = _dot(h_ref[...], w_ref[...])
    u_ref[...] = (acc * cs_ref[...]).astype(u_ref.dtype)


def _proj(x2, g_pre, w_main, w_small, col_scale, tm, tn):
    m = x2.shape[0]
    grid = (m // tm, U_COLS // tn)
    return pl.pallas_call(
        _proj_kernel,
        out_shape=(jax.ShapeDtypeStruct((m, U_COLS), ACT_DTYPE),
                   jax.ShapeDtypeStruct((m, LANES), ACT_DTYPE),
                   jax.ShapeDtypeStruct((m, LANES), jnp.float32)),
        grid=grid,
        in_specs=[
            pl.BlockSpec((tm, D_MODEL), lambda i, j: (i, 0)),
            pl.BlockSpec((1, D_MODEL), lambda i, j: (0, 0)),
            pl.BlockSpec((D_MODEL, tn), lambda i, j: (0, j)),
            pl.BlockSpec((D_MODEL, 2 * LANES), lambda i, j: (0, 0)),
            pl.BlockSpec((1, tn), lambda i, j: (0, j)),
        ],
        out_specs=(
            pl.BlockSpec((tm, tn), lambda i, j: (i, j)),
            pl.BlockSpec((tm, LANES), lambda i, j: (i, 0)),
            pl.BlockSpec((tm, LANES), lambda i, j: (i, 0)),
        ),
        scratch_shapes=[pltpu.VMEM((tm, D_MODEL), MXU_DTYPE)],
        compiler_params=pltpu.CompilerParams(
            dimension_semantics=("arbitrary", "arbitrary"),
            vmem_limit_bytes=VMEM_LIMIT),
        name="proj",
    )(x2, g_pre, w_main, w_small, col_scale)


def _split3(x):
    hi = x.astype(MXU_DTYPE)
    r1 = x - hi.astype(jnp.float32)
    mid = r1.astype(MXU_DTYPE)
    lo = (r1 - mid.astype(jnp.float32)).astype(MXU_DTYPE)
    return hi, mid, lo


def _gla_kernel(q_ref, k_ref, v_ref, z_ref, sm_ref, wup_ref, ba_ref, gg_ref,
                o_ref, st_ref, *, chunks_per_step):
    c = GLA_CHUNK

    @pl.when(pl.program_id(1) == 0)
    def _():
        st_ref[...] = jnp.zeros_like(st_ref)

    row = lax.broadcasted_iota(jnp.int32, (c, c), 0)
    col = lax.broadcasted_iota(jnp.int32, (c, c), 1)
    causal = col <= row
    tril = jnp.where(causal, 1.0, 0.0).astype(MXU_DTYPE)

    for ci in range(chunks_per_step):
        r0 = ci * c
        ga = sm_ref[r0:r0 + c, SMALL_GA:SMALL_GA + GLA_RANK].astype(MXU_DTYPE)
        pre = _dot(ga, wup_ref[...]) + ba_ref[...]
        log_a = (jnp.minimum(pre, 0.0) - jnp.log1p(jnp.exp(-jnp.abs(pre)))) / GLA_GATE_NORM
        hi, mid, lo = _split3(log_a)
        b_all = _dot(tril, hi) + _dot(tril, mid) + _dot(tril, lo)
        for h in range(GLA_HEADS):
            ks = slice(h * GLA_DK, (h + 1) * GLA_DK)
            vs = slice(h * GLA_DV, (h + 1) * GLA_DV)
            b = b_all[:, ks]
            b_last = b[c - 1:c, :]
            q = q_ref[r0:r0 + c, ks].astype(jnp.float32)
            k = k_ref[r0:r0 + c, ks].astype(jnp.float32)
            v = v_ref[r0:r0 + c, vs]
            q_d = (q * jnp.exp(b)).astype(MXU_DTYPE)
            k_d = (k * jnp.exp(-b)).astype(MXU_DTYPE)
            k_t = (k * jnp.exp(b_last - b)).astype(MXU_DTYPE)
            decay = jnp.exp(b_last)
            att = _nt_dot(q_d, k_d)
            att = jnp.where(causal, att, 0.0).astype(MXU_DTYPE)
            st = st_ref[h]
            o = _dot(att, v) + _nt_dot(q_d, st.astype(MXU_DTYPE))
            st_ref[h] = st * decay + _tn_dot(v, k_t)
            y = o * lax.rsqrt(jnp.mean(o * o, axis=-1, keepdims=True) + EPS)
            y = y * gg_ref[...]
            zz = z_ref[r0:r0 + c, vs].astype(jnp.float32)
            o_ref[r0:r0 + c, vs] = (y * _silu(zz)).astype(o_ref.dtype)


def _gla(u, small, w_up, b_a, g_gla, batch, seq, chunks_per_step):
    rows = chunks_per_step * GLA_CHUNK
    steps = seq // rows
    hk = GLA_HEADS * GLA_DK
    hv = GLA_HEADS * GLA_DV

    def rmap(cb):
        return lambda b, s: (b * steps + s, cb)

    return pl.pallas_call(
        functools.partial(_gla_kernel, chunks_per_step=chunks_per_step),
        out_shape=jax.ShapeDtypeStruct((batch * seq, hv), ACT_DTYPE),
        grid=(batch, steps),
        in_specs=[
            pl.BlockSpec((rows, hk), rmap(COL_GQ // hk)),
            pl.BlockSpec((rows, hk), rmap(COL_GK // hk)),
            pl.BlockSpec((rows, hv), rmap(COL_GV // hv)),
            pl.BlockSpec((rows, hv), rmap(COL_GZ // hv)),
            pl.BlockSpec((rows, LANES), rmap(0)),
            pl.BlockSpec((GLA_RANK, hk), lambda b, s: (0, 0)),
            pl.BlockSpec((1, hk), lambda b, s: (0, 0)),
            pl.BlockSpec((1, GLA_DV), lambda b, s: (0, 0)),
        ],
        out_specs=pl.BlockSpec((rows, hv), rmap(0)),
        scratch_shapes=[pltpu.VMEM((GLA_HEADS, GLA_DV, GLA_DK), jnp.float32)],
        compiler_params=pltpu.CompilerParams(
            dimension_semantics=("arbitrary", "arbitrary"),
            vmem_limit_bytes=VMEM_LIMIT),
        name="gla",
    )(u, u, u, u, small, w_up, b_a, g_gla)


SEARCH_CHUNK = 4 * Q_BLOCK


def _t5_bucket(dist):
    max_exact = REL_BUCKETS // 2
    d = jnp.maximum(dist, 1).astype(jnp.float32)
    large = max_exact + (jnp.log(d / max_exact) / math.log(REL_MAX_DIST / max_exact)
                         * (REL_BUCKETS - max_exact)).astype(jnp.int32)
    large = jnp.minimum(large, REL_BUCKETS - 1)
    return jnp.where(dist < max_exact, dist, large)


def _dsa_kernel(rb_ref, dq_ref, dk_ref, dv_ref, iq_ref, ikd_ref, sm_ref, dz_ref,
                out_ref,
                kt_ref, mb_ref, vt_ref, bt_ref, iqm_ref, qaug_ref, acc_ref, pm_ref,
                *, seq):
    qb = pl.program_id(1)
    nb = seq // Q_BLOCK
    blk = Q_BLOCK
    k_sel = min(TOPK_MAX, seq // 4)

    row_i = lax.broadcasted_iota(jnp.int32, (blk, blk), 0)
    col_i = lax.broadcasted_iota(jnp.int32, (blk, blk), 1)

    @pl.when((pl.program_id(0) == 0) & (qb == 0))
    def _():
        for delta in range(3):
            dist = jnp.maximum(delta * blk + col_i - row_i, 0)
            bucket = _t5_bucket(dist)
            for h in range(DSA_HEADS):
                tile = jnp.zeros((blk, blk), jnp.float32)
                for bk in range(REL_BUCKETS):
                    tile = jnp.where(bucket == bk, rb_ref[bk, h], tile)
                cc, g = divmod(h, DSA_GROUPS)
                bt_ref[delta, cc, :, g * blk:(g + 1) * blk] = tile

    @pl.when(qb == 0)
    def _():
        def body(i, carry):
            r = pl.multiple_of(i * blk, blk)
            vblk = dv_ref[pl.ds(r, blk), :].astype(jnp.float32)
            for cc in range(DSA_KV_HEADS):
                vt_ref[i, cc * DSA_HD:(cc + 1) * DSA_HD, :] = (
                    vblk[:, cc * DSA_HD:(cc + 1) * DSA_HD].T.astype(vt_ref.dtype))
            return carry
        lax.fori_loop(0, nb, body, 0)

    w_t = sm_ref[...].T[SMALL_IW:SMALL_IW + IDX_HEADS, :]
    w_t = w_t * (IDX_HEADS ** -0.5) * (IDX_DIM ** -0.5)

    lane = lax.broadcasted_iota(jnp.int32, (blk, LANES), 1)
    for h in range(IDX_HEADS):
        pair = iq_ref[:, (h // 2) * LANES:(h // 2 + 1) * LANES]
        keep = (lane < IDX_DIM) if h % 2 == 0 else (lane >= IDX_DIM)
        iqm_ref[h] = jnp.where(keep, pair, jnp.zeros_like(pair))

    q_pos = qb * blk + col_i

    def score_body(kb, carry):
        r = pl.multiple_of(kb * blk, blk)
        ikb = ikd_ref[pl.ds(r, blk), :]
        acc = jnp.zeros((blk, blk), jnp.float32)
        for h in range(IDX_HEADS):
            s = _nt_dot(ikb, iqm_ref[h])
            acc = acc + w_t[h:h + 1, :] * jnp.maximum(s, 0.0)
        acc = jnp.where(kb * blk + row_i <= q_pos, acc, -jnp.inf)
        bits = lax.bitcast_convert_type(acc, jnp.int32)
        kt_ref[pl.ds(r, blk), :] = bits ^ ((bits >> 31) & 0x7FFFFFFF)
        return carry
    lax.fori_loop(0, qb + 1, score_body, 0)

    n_chunks = qb // 4 + 1

    def fill_body(kb, carry):
        r = pl.multiple_of(kb * blk, blk)
        kt_ref[pl.ds(r, blk), :] = jnp.full((blk, blk), INT_MIN, jnp.int32)
        return carry
    lax.fori_loop(qb + 1, n_chunks * 4, fill_body, 0)

    def count(pred_fn):
        def body(ci, acc):
            r = pl.multiple_of(ci * SEARCH_CHUNK, SEARCH_CHUNK)
            keys = kt_ref[pl.ds(r, SEARCH_CHUNK), :]
            hit = jnp.where(pred_fn(keys, r), 1, 0).astype(jnp.int32)
            return acc + jnp.sum(hit.reshape(SEARCH_CHUNK // SUBLANES, SUBLANES, blk), axis=0)
        acc = lax.fori_loop(0, n_chunks, body, jnp.zeros((SUBLANES, blk), jnp.int32))
        return jnp.sum(acc, axis=0, keepdims=True)

    cnt0 = count(lambda keys, r: keys >= 0)
    ans = jnp.where(cnt0 >= k_sel, 0, INT_MIN).astype(jnp.int32)

    def bit_body(i, ans):
        cand = ans | (jnp.int32(1) << (30 - i))
        cnt = count(lambda keys, r: keys >= cand)
        return jnp.where(cnt >= k_sel, cand, ans)
    ans = lax.fori_loop(0, 31, bit_body, ans)

    cnt_ge = count(lambda keys, r: keys >= ans)
    cnt_gt = count(lambda keys, r: keys > ans)
    need = k_sel - cnt_gt

    pm_ref[...] = jnp.full(pm_ref.shape, 2 * seq, jnp.int32)

    @pl.when(jnp.max(cnt_ge) > k_sel)
    def _():
        sub_i = lax.broadcasted_iota(jnp.int32, (SEARCH_CHUNK, blk), 0)
        n_bits = (2 * seq - 1).bit_length()

        def pos_body(i, p):
            cand = p | (jnp.int32(1) << (n_bits - 1 - i))
            cnt = count(lambda keys, r: (keys == ans) & (r + sub_i < cand))
            return jnp.where(cnt <= need - 1, cand, p)
        p_max = lax.fori_loop(0, n_bits, pos_body, jnp.zeros((1, blk), jnp.int32))
        pm_ref[...] = jnp.broadcast_to(p_max, pm_ref.shape)

    p_max = pm_ref[0:1, :]

    def mask_body(kb, carry):
        r = pl.multiple_of(kb * blk, blk)
        keys = kt_ref[pl.ds(r, blk), :]
        k_pos = kb * blk + row_i
        sel = (keys > ans) | ((keys == ans) & (k_pos <= p_max))
        sel = sel & (k_pos <= q_pos)
        mb_ref[pl.ds(r, blk), :] = jnp.where(sel, 0.0, MASK_NEG).astype(mb_ref.dtype)
        return carry
    lax.fori_loop(0, qb + 1, mask_body, 0)

    eye = jnp.where(row_i == col_i, 1.0, 0.0).astype(MXU_DTYPE)
    for cc in range(DSA_KV_HEADS):
        for g in range(DSA_GROUPS):
            h = cc * DSA_GROUPS + g
            qaug_ref[cc, g * blk:(g + 1) * blk, 0:DSA_HD] = dq_ref[:, h * DSA_HD:(h + 1) * DSA_HD]
            qaug_ref[cc, g * blk:(g + 1) * blk, DSA_HD:2 * DSA_HD] = eye

    width = DSA_GROUPS * blk
    for cc in range(DSA_KV_HEADS):
        acc_ref[...] = jnp.zeros_like(acc_ref)

        def attn_body(kb, carry, cc=cc):
            m, l = carry
            r = pl.multiple_of(kb * blk, blk)
            kaug = jnp.concatenate(
                [dk_ref[pl.ds(r, blk), cc * DSA_HD:(cc + 1) * DSA_HD], mb_ref[pl.ds(r, blk), :]],
                axis=1)
            lg = _nt_dot(kaug, qaug_ref[cc])
            lg = lg + bt_ref[jnp.minimum(qb - kb, 2), cc]
            m_new = jnp.maximum(m, jnp.max(lg, axis=0, keepdims=True))
            alpha = jnp.exp(m - m_new)
            p = jnp.exp(lg - m_new)
            l = alpha * l + jnp.sum(p, axis=0, keepdims=True)
            pv = _dot(vt_ref[kb, cc * DSA_HD:(cc + 1) * DSA_HD, :], p.astype(MXU_DTYPE))
            acc_ref[...] = acc_ref[...] * alpha + pv
            return m_new, l

        m0 = jnp.full((1, width), -jnp.inf, jnp.float32)
        l0 = jnp.zeros((1, width), jnp.float32)
        _, l = lax.fori_loop(0, qb + 1, attn_body, (m0, l0))
        o_t = acc_ref[...] / l
        for g in range(DSA_GROUPS):
            h = cc * DSA_GROUPS + g
            o = o_t[:, g * blk:(g + 1) * blk].T
            zz = dz_ref[:, h * DSA_HD:(h + 1) * DSA_HD].astype(jnp.float32)
            out_ref[:, h * DSA_HD:(h + 1) * DSA_HD] = (o * _silu(zz)).astype(out_ref.dtype)


def _dsa(u, ikd, small, rel_bias, batch, seq):
    nb = seq // Q_BLOCK
    hq = DSA_HEADS * DSA_HD
    hkv = DSA_KV_HEADS * DSA_HD
    hi = IDX_HEADS * IDX_DIM

    def qmap(cb):
        return lambda b, q: (b * nb + q, cb)

    def bmap(cb):
        return lambda b, q: (b, cb)

    return pl.pallas_call(
        functools.partial(_dsa_kernel, seq=seq),
        out_shape=jax.ShapeDtypeStruct((batch * seq, hq), ACT_DTYPE),
        grid=(batch, nb),
        in_specs=[
            pl.BlockSpec(memory_space=pltpu.SMEM),
            pl.BlockSpec((Q_BLOCK, hq), qmap(COL_DQ // hq)),
            pl.BlockSpec((seq, hkv), bmap(COL_DK // hkv)),
            pl.BlockSpec((seq, hkv), bmap(COL_DV // hkv)),
            pl.BlockSpec((Q_BLOCK, hi), qmap(COL_IQ // hi)),
            pl.BlockSpec((seq, LANES), bmap(0)),
            pl.BlockSpec((Q_BLOCK, LANES), qmap(0)),
            pl.BlockSpec((Q_BLOCK, hq), qmap(COL_DZ // hq)),
        ],
        out_specs=pl.BlockSpec((Q_BLOCK, hq), qmap(0)),
        scratch_shapes=[
            pltpu.VMEM((seq, Q_BLOCK), jnp.int32),
            pltpu.VMEM((seq, Q_BLOCK), MXU_DTYPE),
            pltpu.VMEM((nb, hkv, Q_BLOCK), MXU_DTYPE),
            pltpu.VMEM((3, DSA_KV_HEADS, Q_BLOCK, DSA_GROUPS * Q_BLOCK), jnp.float32),
            pltpu.VMEM((IDX_HEADS, Q_BLOCK, LANES), MXU_DTYPE),
            pltpu.VMEM((DSA_KV_HEADS, DSA_GROUPS * Q_BLOCK, 2 * DSA_HD), MXU_DTYPE),
            pltpu.VMEM((DSA_HD, DSA_GROUPS * Q_BLOCK), jnp.float32),
            pltpu.VMEM((SUBLANES, Q_BLOCK), jnp.int32),
        ],
        compiler_params=pltpu.CompilerParams(
            dimension_semantics=("arbitrary", "arbitrary"),
            vmem_limit_bytes=VMEM_LIMIT),
        name="dsa",
    )(rel_bias, u, u, u, u, ikd, small, u)


def _memkv_kernel(mem_ref, g_ref, w_ref, o_ref):
    xf = mem_ref[...]
    y = xf * lax.rsqrt(jnp.mean(xf * xf, axis=-1, keepdims=True) + EPS)
    hb = (y * g_ref[...]).astype(MXU_DTYPE)
    o_ref[...] = _dot(hb, w_ref[...]).astype(o_ref.dtype)


def _memkv(mem2, g_mem, w_kv, batch):
    n = 2 * X_HEADS * X_HD
    return pl.pallas_call(
        _memkv_kernel,
        out_shape=jax.ShapeDtypeStruct((batch * N_MEM, n), ACT_DTYPE),
        grid=(batch,),
        in_specs=[
            pl.BlockSpec((N_MEM, D_MODEL), lambda b: (b, 0)),
            pl.BlockSpec((1, D_MODEL), lambda b: (0, 0)),
            pl.BlockSpec((D_MODEL, n), lambda b: (0, 0)),
        ],
        out_specs=pl.BlockSpec((N_MEM, n), lambda b: (b, 0)),
        compiler_params=pltpu.CompilerParams(
            dimension_semantics=("arbitrary",),
            vmem_limit_bytes=VMEM_LIMIT),
        name="memkv",
    )(mem2, g_mem, w_kv)


def _merge_kernel(x_ref, yg_ref, yd_ref, xq_ref, xz_ref, sg_ref, sd_ref, sm_ref, mkv_ref,
                  wg_ref, wd_ref, wx_ref, wo_ref, gp_ref, o_ref, ym_ref):
    hw = X_HEADS * X_HD
    for h in range(X_HEADS):
        cs = slice(h * X_HD, (h + 1) * X_HD)
        mk = mkv_ref[:, h * X_HD:(h + 1) * X_HD]
        mv = mkv_ref[:, hw + h * X_HD:hw + (h + 1) * X_HD]
        lg = _nt_dot(xq_ref[:, cs], mk)
        lg = lg - jnp.max(lg, axis=-1, keepdims=True)
        e = jnp.exp(lg)
        p = (e / jnp.sum(e, axis=-1, keepdims=True)).astype(MXU_DTYPE)
        o = _dot(p, mv)
        ym_ref[:, cs] = (o * _silu(xz_ref[:, cs].astype(jnp.float32))).astype(ym_ref.dtype)

    merged = jax.nn.sigmoid(sg_ref[...].astype(jnp.float32)) * _dot(yg_ref[...], wg_ref[...])
    merged = merged + jax.nn.sigmoid(sd_ref[...].astype(jnp.float32)) * _dot(yd_ref[...], wd_ref[...])
    merged = merged + jax.nn.sigmoid(sm_ref[...].astype(jnp.float32)) * _dot(ym_ref[...], wx_ref[...])
    t = _dot(merged.astype(MXU_DTYPE), wo_ref[...])
    y = t * lax.rsqrt(jnp.mean(t * t, axis=-1, keepdims=True) + EPS)
    o_ref[...] = x_ref[...] + y * gp_ref[...]


def _merge(x2, y_gla, y_dsa, u, mkv, w_g, w_d, w_x, w_o, g_post, seq, tm):
    m = x2.shape[0]
    steps_per_batch = seq // tm
    d = D_MODEL

    def rmap(cb):
        return lambda i: (i, cb)

    wspec = pl.BlockSpec((d, d), lambda i: (0, 0))
    return pl.pallas_call(
        _merge_kernel,
        out_shape=jax.ShapeDtypeStruct((m, d), jnp.float32),
        grid=(m // tm,),
        in_specs=[
            pl.BlockSpec((tm, d), rmap(0)),
            pl.BlockSpec((tm, d), rmap(0)),
            pl.BlockSpec((tm, d), rmap(0)),
            pl.BlockSpec((tm, d), rmap(COL_XQ // d)),
            pl.BlockSpec((tm, d), rmap(COL_XZ // d)),
            pl.BlockSpec((tm, d), rmap(COL_SG // d)),
            pl.BlockSpec((tm, d), rmap(COL_SD // d)),
            pl.BlockSpec((tm, d), rmap(COL_SM // d)),
            pl.BlockSpec((N_MEM, 2 * X_HEADS * X_HD), lambda i: (i // steps_per_batch, 0)),
            wspec, wspec, wspec, wspec,
            pl.BlockSpec((1, d), lambda i: (0, 0)),
        ],
        out_specs=pl.BlockSpec((tm, d), rmap(0)),
        scratch_shapes=[pltpu.VMEM((tm, d), MXU_DTYPE)],
        compiler_params=pltpu.CompilerParams(
            dimension_semantics=("arbitrary",),
            vmem_limit_bytes=VMEM_LIMIT),
        name="merge",
    )(x2, y_gla, y_dsa, u, u, u, u, u, mkv, w_g, w_d, w_x, w_o, g_post)


def _relayout_w_in(w_in):
    offs = np.cumsum(np.array(SPLIT_SIZES))[:-1].tolist()
    (gq, gk, gv, ga, gz, dq, dk, dv, iq, ik, iw, dz, xq, xz, gates) = jnp.split(w_in, offs, axis=1)
    main = jnp.concatenate([gq, gk, gv, gz, dq, dz, xq, xz, gates, dk, dv, iq], axis=1)
    pad = jnp.zeros((w_in.shape[0], LANES - GLA_RANK - IDX_HEADS), w_in.dtype)
    small = jnp.concatenate([ik, ik, ga, iw, pad], axis=1)
    return main.astype(MXU_DTYPE), small.astype(MXU_DTYPE)


def _col_scale():
    s = np.ones((1, U_COLS), np.float32)
    s[:, COL_GQ:COL_GQ + GLA_HEADS * GLA_DK] = GLA_DK ** -0.5
    s[:, COL_DQ:COL_DQ + DSA_HEADS * DSA_HD] = DSA_HD ** -0.5
    s[:, COL_XQ:COL_XQ + X_HEADS * X_HD] = X_HD ** -0.5
    return jnp.asarray(s)


def _layer(x2, mem2, g_pre, g_post, g_mem, w_in, w_up, b_a, g_gla, rel_bias, w_kv,
           w_g, w_d, w_x, w_o, batch, seq):
    w_main, w_small = _relayout_w_in(w_in)
    tm = min(1024, batch * seq)
    u, ikd, small = _proj(x2, g_pre.reshape(1, -1), w_main, w_small, _col_scale(), tm, 1024)
    y_gla = _gla(u, small, w_up.astype(MXU_DTYPE), b_a.reshape(1, -1), g_gla.reshape(1, -1),
                 batch, seq, chunks_per_step=4)
    y_dsa = _dsa(u, ikd, small, rel_bias, batch, seq)
    mkv = _memkv(mem2, g_mem.reshape(1, -1), w_kv.astype(MXU_DTYPE), batch)
    return _merge(x2, y_gla, y_dsa, u, mkv, w_g.astype(MXU_DTYPE), w_d.astype(MXU_DTYPE),
                  w_x.astype(MXU_DTYPE), w_o.astype(MXU_DTYPE), g_post.reshape(1, -1), seq, 256)


def kernel(x, mem, g_pre, g_post, g_mem, w_in, w_gla_a_up, b_gla_a, g_gla, rel_bias,
           w_mem_kv, w_gla_out, w_dsa_out, w_x_out, w_o):
    batch, seq, d = x.shape
    x2 = x.reshape(batch * seq, d)
    mem2 = mem.reshape(batch * N_MEM, d)
    for i in range(g_pre.shape[0]):
        x2 = _layer(x2, mem2, g_pre[i], g_post[i], g_mem[i], w_in[i], w_gla_a_up[i],
                    b_gla_a[i], g_gla[i], rel_bias, w_mem_kv[i], w_gla_out[i],
                    w_dsa_out[i], w_x_out[i], w_o[i], batch, seq)
    return x2.reshape(batch, seq, d)
```

```python
import functools
import math

import jax
import jax.numpy as jnp
import numpy as np
from jax import lax
from jax.experimental import pallas as pl
from jax.experimental.pallas import tpu as pltpu

D_MODEL = 1024
N_MEM = 256
EPS = 1e-6
GLA_HEADS = 4
GLA_DK = 128
GLA_DV = 256
GLA_RANK = 16
GLA_GATE_NORM = 16.0
GLA_CHUNK = 64
DSA_HEADS = 8
DSA_KV_HEADS = 2
DSA_GROUPS = DSA_HEADS // DSA_KV_HEADS
DSA_HD = 128
IDX_HEADS = 8
IDX_DIM = 64
TOPK_MAX = 256
Q_BLOCK = 128
REL_BUCKETS = 32
REL_MAX_DIST = 128
X_HEADS = 4
X_HD = 256

SPLIT_SIZES = (512, 512, 1024, 16, 1024, 1024, 256, 256, 512, 64, 8, 1024, 1024, 1024, 3072)

LANES = 128
SUBLANES = 8

MXU_DTYPE = jnp.bfloat16
ACT_DTYPE = jnp.bfloat16

U_COLS = 11264
COL_GQ, COL_GK, COL_GV, COL_GZ = 0, 512, 1024, 2048
COL_DQ, COL_DZ, COL_XQ, COL_XZ = 3072, 4096, 5120, 6144
COL_SG, COL_SD, COL_SM = 7168, 8192, 9216
COL_DK, COL_DV, COL_IQ = 10240, 10496, 10752
SMALL_GA, SMALL_IW = 0, 16

MASK_NEG = -1e30
INT_MIN = -(2 ** 31)
VMEM_LIMIT = 56 * 1024 * 1024


def _nt_dot(a, b):
    return lax.dot_general(a, b, (((1,), (1,)), ((), ())),
                           preferred_element_type=jnp.float32)


def _tn_dot(a, b):
    return lax.dot_general(a, b, (((0,), (0,)), ((), ())),
                           preferred_element_type=jnp.float32)


def _dot(a, b):
    return jnp.dot(a, b, preferred_element_type=jnp.float32)


def _silu(z):
    return z * jax.nn.sigmoid(z)


def _proj_kernel(x_ref, g_ref, w_ref, ws_ref, cs_ref, u_ref, ikd_ref, sm_ref, h_ref):
    @pl.when(pl.program_id(1) == 0)
    def _():
        xf = x_ref[...]
        y = xf * lax.rsqrt(jnp.mean(xf * xf, axis=-1, keepdims=True) + EPS)
        hb = (y * g_ref[...]).astype(MXU_DTYPE)
        h_ref[...] = hb
        r = _dot(hb, ws_ref[...])
        ikd_ref[...] = r[:, :LANES].astype(ikd_ref.dtype)
        sm_ref[...] = r[:, LANES:]

    acc = _dot(h_ref[...], w_ref[...])
    u_ref[...] = (acc * cs_ref[...]).astype(u_ref.dtype)


def _proj(x2, g_pre, w_main, w_small, col_scale, tm, tn):
    m = x2.shape[0]
    grid = (m // tm, U_COLS // tn)
    return pl.pallas_call(
        _proj_kernel,
        out_shape=(jax.ShapeDtypeStruct((m, U_COLS), ACT_DTYPE),
                   jax.ShapeDtypeStruct((m, LANES), ACT_DTYPE),
                   jax.ShapeDtypeStruct((m, LANES), jnp.float32)),
        grid=grid,
        in_specs=[
            pl.BlockSpec((tm, D_MODEL), lambda i, j: (i, 0)),
            pl.BlockSpec((1, D_MODEL), lambda i, j: (0, 0)),
            pl.BlockSpec((D_MODEL, tn), lambda i, j: (0, j)),
            pl.BlockSpec((D_MODEL, 2 * LANES), lambda i, j: (0, 0)),
            pl.BlockSpec((1, tn), lambda i, j: (0, j)),
        ],
        out_specs=(
            pl.BlockSpec((tm, tn), lambda i, j: (i, j)),
            pl.BlockSpec((tm, LANES), lambda i, j: (i, 0)),
            pl.BlockSpec((tm, LANES), lambda i, j: (i, 0)),
        ),
        scratch_shapes=[pltpu.VMEM((tm, D_MODEL), MXU_DTYPE)],
        compiler_params=pltpu.CompilerParams(
            dimension_semantics=("arbitrary", "arbitrary"),
            vmem_limit_bytes=VMEM_LIMIT),
        name="proj",
    )(x2, g_pre, w_main, w_small, col_scale)


def _split3(x):
    hi = x.astype(MXU_DTYPE)
    r1 = x - hi.astype(jnp.float32)
    mid = r1.astype(MXU_DTYPE)
    lo = (r1 - mid.astype(jnp.float32)).astype(MXU_DTYPE)
    return hi, mid, lo


def _gla_kernel(q_ref, k_ref, v_ref, z_ref, sm_ref, wup_ref, ba_ref, gg_ref,
                o_ref, st_ref, *, chunks_per_step):
    c = GLA_CHUNK

    @pl.when(pl.program_id(1) == 0)
    def _():
        st_ref[...] = jnp.zeros_like(st_ref)

    row = lax.broadcasted_iota(jnp.int32, (c, c), 0)
    col = lax.broadcasted_iota(jnp.int32, (c, c), 1)
    causal = col <= row
    tril = jnp.where(causal, 1.0, 0.0).astype(MXU_DTYPE)

    for ci in range(chunks_per_step):
        r0 = ci * c
        ga = sm_ref[r0:r0 + c, SMALL_GA:SMALL_GA + GLA_RANK].astype(MXU_DTYPE)
        pre = _dot(ga, wup_ref[...]) + ba_ref[...]
        log_a = (jnp.minimum(pre, 0.0) - jnp.log1p(jnp.exp(-jnp.abs(pre)))) / GLA_GATE_NORM
        hi, mid, lo = _split3(log_a)
        b_all = _dot(tril, hi) + _dot(tril, mid) + _dot(tril, lo)
        for h in range(GLA_HEADS):
            ks = slice(h * GLA_DK, (h + 1) * GLA_DK)
            vs = slice(h * GLA_DV, (h + 1) * GLA_DV)
            b = b_all[:, ks]
            b_last = b[c - 1:c, :]
            q = q_ref[r0:r0 + c, ks].astype(jnp.float32)
            k = k_ref[r0:r0 + c, ks].astype(jnp.float32)
            v = v_ref[r0:r0 + c, vs]
            q_d = (q * jnp.exp(b)).astype(MXU_DTYPE)
            k_d = (k * jnp.exp(-b)).astype(MXU_DTYPE)
            k_t = (k * jnp.exp(b_last - b)).astype(MXU_DTYPE)
            decay = jnp.exp(b_last)
            att = _nt_dot(q_d, k_d)
            att = jnp.where(causal, att, 0.0).astype(MXU_DTYPE)
            st = st_ref[h]
            o = _dot(att, v) + _nt_dot(q_d, st.astype(MXU_DTYPE))
            st_ref[h] = st * decay + _tn_dot(v, k_t)
            y = o * lax.rsqrt(jnp.mean(o * o, axis=-1, keepdims=True) + EPS)
            y = y * gg_ref[...]
            zz = z_ref[r0:r0 + c, vs].astype(jnp.float32)
            o_ref[r0:r0 + c, vs] = (y * _silu(zz)).astype(o_ref.dtype)


def _gla(u, small, w_up, b_a, g_gla, batch, seq, chunks_per_step):
    rows = chunks_per_step * GLA_CHUNK
    steps = seq // rows
    hk = GLA_HEADS * GLA_DK
    hv = GLA_HEADS * GLA_DV

    def rmap(cb):
        return lambda b, s: (b * steps + s, cb)

    return pl.pallas_call(
        functools.partial(_gla_kernel, chunks_per_step=chunks_per_step),
        out_shape=jax.ShapeDtypeStruct((batch * seq, hv), ACT_DTYPE),
        grid=(batch, steps),
        in_specs=[
            pl.BlockSpec((rows, hk), rmap(COL_GQ // hk)),
            pl.BlockSpec((rows, hk), rmap(COL_GK // hk)),
            pl.BlockSpec((rows, hv), rmap(COL_GV // hv)),
            pl.BlockSpec((rows, hv), rmap(COL_GZ // hv)),
            pl.BlockSpec((rows, LANES), rmap(0)),
            pl.BlockSpec((GLA_RANK, hk), lambda b, s: (0, 0)),
            pl.BlockSpec((1, hk), lambda b, s: (0, 0)),
            pl.BlockSpec((1, GLA_DV), lambda b, s: (0, 0)),
        ],
        out_specs=pl.BlockSpec((rows, hv), rmap(0)),
        scratch_shapes=[pltpu.VMEM((GLA_HEADS, GLA_DV, GLA_DK), jnp.float32)],
        compiler_params=pltpu.CompilerParams(
            dimension_semantics=("arbitrary", "arbitrary"),
            vmem_limit_bytes=VMEM_LIMIT),
        name="gla",
    )(u, u, u, u, small, w_up, b_a, g_gla)


KEY_CHUNK = 4 * Q_BLOCK
BLOCKS_PER_CHUNK = KEY_CHUNK // Q_BLOCK


def _t5_bucket(dist):
    max_exact = REL_BUCKETS // 2
    d = jnp.maximum(dist, 1).astype(jnp.float32)
    large = max_exact + (jnp.log(d / max_exact) / math.log(REL_MAX_DIST / max_exact)
                         * (REL_BUCKETS - max_exact)).astype(jnp.int32)
    large = jnp.minimum(large, REL_BUCKETS - 1)
    return jnp.where(dist < max_exact, dist, large)


def _dsa_kernel(rb_ref, dq_ref, dk_ref, dv_ref, iq_ref, ikd_ref, sm_ref, dz_ref,
                out_ref,
                kt_ref, mb_ref, vt_ref, bt_ref, iqm_ref, qaug_ref, acc_ref, pm_ref,
                *, seq):
    qb = pl.program_id(1)
    blk = Q_BLOCK
    ch = KEY_CHUNK
    k_sel = min(TOPK_MAX, seq // 4)
    n_chunks = qb // BLOCKS_PER_CHUNK + 1
    width = DSA_GROUPS * blk

    row_i = lax.broadcasted_iota(jnp.int32, (blk, blk), 0)
    col_i = lax.broadcasted_iota(jnp.int32, (blk, blk), 1)
    crow_i = lax.broadcasted_iota(jnp.int32, (ch, blk), 0)
    q_pos = qb * blk + lax.broadcasted_iota(jnp.int32, (ch, blk), 1)

    @pl.when((pl.program_id(0) == 0) & (qb == 0))
    def _():
        for delta in range(3):
            dist = jnp.maximum(delta * blk + col_i - row_i, 0)
            bucket = _t5_bucket(dist)
            for h in range(DSA_HEADS):
                tile = jnp.zeros((blk, blk), jnp.float32)
                for bk in range(REL_BUCKETS):
                    tile = jnp.where(bucket == bk, rb_ref[bk, h], tile)
                cc, g = divmod(h, DSA_GROUPS)
                bt_ref[delta, cc, :, g * blk:(g + 1) * blk] = tile

    @pl.when(qb == 0)
    def _():
        def body(i, carry):
            r = pl.multiple_of(i * ch, ch)
            vt_ref[i] = dv_ref[pl.ds(r, ch), :].astype(jnp.float32).T.astype(vt_ref.dtype)
            return carry
        lax.fori_loop(0, seq // ch, body, 0)

    w_t = sm_ref[...].T[SMALL_IW:SMALL_IW + IDX_HEADS, :]
    w_t = w_t * (IDX_HEADS ** -0.5) * (IDX_DIM ** -0.5)

    lane = lax.broadcasted_iota(jnp.int32, (blk, LANES), 1)
    for h in range(IDX_HEADS):
        pair = iq_ref[:, (h // 2) * LANES:(h // 2 + 1) * LANES]
        keep = (lane < IDX_DIM) if h % 2 == 0 else (lane >= IDX_DIM)
        iqm_ref[h] = jnp.where(keep, pair, jnp.zeros_like(pair))

    def score_chunk(c, causal):
        r = pl.multiple_of(c * ch, ch)
        ikc = ikd_ref[pl.ds(r, ch), :]
        acc = None
        for h in range(IDX_HEADS):
            t = w_t[h:h + 1, :] * jnp.maximum(_nt_dot(ikc, iqm_ref[h]), 0.0)
            acc = t if acc is None else acc + t
        if causal:
            acc = jnp.where(r + crow_i <= q_pos, acc, -jnp.inf)
        bits = lax.bitcast_convert_type(acc, jnp.int32)
        kt_ref[pl.ds(r, ch), :] = bits ^ ((bits >> 31) & 0x7FFFFFFF)

    def score_body(c, carry):
        score_chunk(c, False)
        return carry
    lax.fori_loop(0, n_chunks - 1, score_body, 0)
    score_chunk(n_chunks - 1, True)

    def count(pred_fn):
        def body(ci, acc):
            r = pl.multiple_of(ci * ch, ch)
            keys = kt_ref[pl.ds(r, ch), :]
            hit = jnp.where(pred_fn(keys, r), 1, 0).astype(jnp.int32)
            return acc + jnp.sum(hit.reshape(ch // SUBLANES, SUBLANES, blk), axis=0)
        acc = lax.fori_loop(0, n_chunks, body, jnp.zeros((SUBLANES, blk), jnp.int32))
        return jnp.sum(acc, axis=0, keepdims=True)

    cnt0 = count(lambda keys, r: keys >= 0)
    ans = jnp.where(cnt0 >= k_sel, 0, INT_MIN).astype(jnp.int32)

    def bit_body(i, ans):
        cand = ans | (jnp.int32(1) << (30 - i))
        cnt = count(lambda keys, r: keys >= cand)
        return jnp.where(cnt >= k_sel, cand, ans)
    ans = lax.fori_loop(0, 31, bit_body, ans)

    cnt_ge = count(lambda keys, r: keys >= ans)
    cnt_gt = count(lambda keys, r: keys > ans)
    need = k_sel - cnt_gt

    pm_ref[...] = jnp.full(pm_ref.shape, 2 * seq, jnp.int32)

    @pl.when(jnp.max(cnt_ge) > k_sel)
    def _():
        n_bits = (2 * seq - 1).bit_length()

        def pos_body(i, p):
            cand = p | (jnp.int32(1) << (n_bits - 1 - i))
            cnt = count(lambda keys, r: (keys == ans) & (r + crow_i < cand))
            return jnp.where(cnt <= need - 1, cand, p)
        p_max = lax.fori_loop(0, n_bits, pos_body, jnp.zeros((1, blk), jnp.int32))
        pm_ref[...] = jnp.broadcast_to(p_max, pm_ref.shape)

    p_max = pm_ref[0:1, :]

    def mask_body(c, carry):
        r = pl.multiple_of(c * ch, ch)
        keys = kt_ref[pl.ds(r, ch), :]
        k_pos = r + crow_i
        sel = (keys > ans) | ((keys == ans) & (k_pos <= p_max))
        sel = sel & (k_pos <= q_pos)
        mb_ref[pl.ds(r, ch), :] = jnp.where(sel, 0.0, MASK_NEG).astype(mb_ref.dtype)
        return carry
    lax.fori_loop(0, n_chunks, mask_body, 0)

    eye = jnp.where(row_i == col_i, 1.0, 0.0).astype(MXU_DTYPE)
    for cc in range(DSA_KV_HEADS):
        for g in range(DSA_GROUPS):
            h = cc * DSA_GROUPS + g
            qaug_ref[cc, g * blk:(g + 1) * blk, 0:DSA_HD] = dq_ref[:, h * DSA_HD:(h + 1) * DSA_HD]
            qaug_ref[cc, g * blk:(g + 1) * blk, DSA_HD:2 * DSA_HD] = eye
    acc_ref[...] = jnp.zeros_like(acc_ref)

    def attn_chunk(c, carry, far):
        r = pl.multiple_of(c * ch, ch)
        mbc = mb_ref[pl.ds(r, ch), :]
        out = []
        for cc in range(DSA_KV_HEADS):
            m, l = carry[2 * cc], carry[2 * cc + 1]
            kaug = jnp.concatenate([dk_ref[pl.ds(r, ch), cc * DSA_HD:(cc + 1) * DSA_HD], mbc], axis=1)
            lg = _nt_dot(kaug, qaug_ref[cc])
            if far:
                cvec = bt_ref[2, cc, 0:1, :]
                m_new = jnp.maximum(m, jnp.max(lg, axis=0, keepdims=True) + cvec)
                p = jnp.exp(lg - (m_new - cvec))
            else:
                bias = jnp.concatenate(
                    [bt_ref[jnp.clip(qb - (c * BLOCKS_PER_CHUNK + j), 0, 2), cc]
                     for j in range(BLOCKS_PER_CHUNK)], axis=0)
                lg = lg + bias
                m_new = jnp.maximum(m, jnp.max(lg, axis=0, keepdims=True))
                p = jnp.exp(lg - m_new)
            alpha = jnp.exp(m - m_new)
            l_new = alpha * l + jnp.sum(p, axis=0, keepdims=True)
            pv = _dot(vt_ref[c, cc * DSA_HD:(cc + 1) * DSA_HD, :], p.astype(MXU_DTYPE))
            acc_ref[cc] = acc_ref[cc] * alpha + pv
            out += [m_new, l_new]
        return tuple(out)

    c_near = jnp.maximum(qb - 1, 0) // BLOCKS_PER_CHUNK
    m0 = jnp.full((1, width), -jnp.inf, jnp.float32)
    l0 = jnp.zeros((1, width), jnp.float32)
    carry = lax.fori_loop(0, c_near, functools.partial(attn_chunk, far=True), (m0, l0, m0, l0))
    carry = lax.fori_loop(c_near, n_chunks, functools.partial(attn_chunk, far=False), carry)

    for cc in range(DSA_KV_HEADS):
        o_t = acc_ref[cc] / carry[2 * cc + 1]
        for g in range(DSA_GROUPS):
            h = cc * DSA_GROUPS + g
            o = o_t[:, g * blk:(g + 1) * blk].T
            zz = dz_ref[:, h * DSA_HD:(h + 1) * DSA_HD].astype(jnp.float32)
            out_ref[:, h * DSA_HD:(h + 1) * DSA_HD] = (o * _silu(zz)).astype(out_ref.dtype)


def _dsa(u, ikd, small, rel_bias, batch, seq):
    nb = seq // Q_BLOCK
    hq = DSA_HEADS * DSA_HD
    hkv = DSA_KV_HEADS * DSA_HD
    hi = IDX_HEADS * IDX_DIM

    def qmap(cb):
        return lambda b, q: (b * nb + q, cb)

    def bmap(cb):
        return lambda b, q: (b, cb)

    return pl.pallas_call(
        functools.partial(_dsa_kernel, seq=seq),
        out_shape=jax.ShapeDtypeStruct((batch * seq, hq), ACT_DTYPE),
        grid=(batch, nb),
        in_specs=[
            pl.BlockSpec(memory_space=pltpu.SMEM),
            pl.BlockSpec((Q_BLOCK, hq), qmap(COL_DQ // hq)),
            pl.BlockSpec((seq, hkv), bmap(COL_DK // hkv)),
            pl.BlockSpec((seq, hkv), bmap(COL_DV // hkv)),
            pl.BlockSpec((Q_BLOCK, hi), qmap(COL_IQ // hi)),
            pl.BlockSpec((seq, LANES), bmap(0)),
            pl.BlockSpec((Q_BLOCK, LANES), qmap(0)),
            pl.BlockSpec((Q_BLOCK, hq), qmap(COL_DZ // hq)),
        ],
        out_specs=pl.BlockSpec((Q_BLOCK, hq), qmap(0)),
        scratch_shapes=[
            pltpu.VMEM((seq, Q_BLOCK), jnp.int32),
            pltpu.VMEM((seq, Q_BLOCK), MXU_DTYPE),
            pltpu.VMEM((seq // KEY_CHUNK, hkv, KEY_CHUNK), MXU_DTYPE),
            pltpu.VMEM((3, DSA_KV_HEADS, Q_BLOCK, DSA_GROUPS * Q_BLOCK), jnp.float32),
            pltpu.VMEM((IDX_HEADS, Q_BLOCK, LANES), MXU_DTYPE),
            pltpu.VMEM((DSA_KV_HEADS, DSA_GROUPS * Q_BLOCK, 2 * DSA_HD), MXU_DTYPE),
            pltpu.VMEM((DSA_KV_HEADS, DSA_HD, DSA_GROUPS * Q_BLOCK), jnp.float32),
            pltpu.VMEM((SUBLANES, Q_BLOCK), jnp.int32),
        ],
        compiler_params=pltpu.CompilerParams(
            dimension_semantics=("arbitrary", "arbitrary"),
            vmem_limit_bytes=VMEM_LIMIT),
        name="dsa",
    )(rel_bias, u, u, u, u, ikd, small, u)


def _memkv_kernel(mem_ref, g_ref, w_ref, o_ref):
    xf = mem_ref[...]
    y = xf * lax.rsqrt(jnp.mean(xf * xf, axis=-1, keepdims=True) + EPS)
    hb = (y * g_ref[...]).astype(MXU_DTYPE)
    o_ref[...] = _dot(hb, w_ref[...]).astype(o_ref.dtype)


def _memkv(mem2, g_mem, w_kv, batch):
    n = 2 * X_HEADS * X_HD
    return pl.pallas_call(
        _memkv_kernel,
        out_shape=jax.ShapeDtypeStruct((batch * N_MEM, n), ACT_DTYPE),
        grid=(batch,),
        in_specs=[
            pl.BlockSpec((N_MEM, D_MODEL), lambda b: (b, 0)),
            pl.BlockSpec((1, D_MODEL), lambda b: (0, 0)),
            pl.BlockSpec((D_MODEL, n), lambda b: (0, 0)),
        ],
        out_specs=pl.BlockSpec((N_MEM, n), lambda b: (b, 0)),
        compiler_params=pltpu.CompilerParams(
            dimension_semantics=("arbitrary",),
            vmem_limit_bytes=VMEM_LIMIT),
        name="memkv",
    )(mem2, g_mem, w_kv)


def _merge_kernel(x_ref, yg_ref, yd_ref, xq_ref, xz_ref, sg_ref, sd_ref, sm_ref, mkv_ref,
                  wg_ref, wd_ref, wx_ref, wo_ref, gp_ref, o_ref, ym_ref):
    hw = X_HEADS * X_HD
    for h in range(X_HEADS):
        cs = slice(h * X_HD, (h + 1) * X_HD)
        mk = mkv_ref[:, h * X_HD:(h + 1) * X_HD]
        mv = mkv_ref[:, hw + h * X_HD:hw + (h + 1) * X_HD]
        lg = _nt_dot(xq_ref[:, cs], mk)
        lg = lg - jnp.max(lg, axis=-1, keepdims=True)
        e = jnp.exp(lg)
        p = (e / jnp.sum(e, axis=-1, keepdims=True)).astype(MXU_DTYPE)
        o = _dot(p, mv)
        ym_ref[:, cs] = (o * _silu(xz_ref[:, cs].astype(jnp.float32))).astype(ym_ref.dtype)

    merged = jax.nn.sigmoid(sg_ref[...].astype(jnp.float32)) * _dot(yg_ref[...], wg_ref[...])
    merged = merged + jax.nn.sigmoid(sd_ref[...].astype(jnp.float32)) * _dot(yd_ref[...], wd_ref[...])
    merged = merged + jax.nn.sigmoid(sm_ref[...].astype(jnp.float32)) * _dot(ym_ref[...], wx_ref[...])
    t = _dot(merged.astype(MXU_DTYPE), wo_ref[...])
    y = t * lax.rsqrt(jnp.mean(t * t, axis=-1, keepdims=True) + EPS)
    o_ref[...] = x_ref[...] + y * gp_ref[...]


def _merge(x2, y_gla, y_dsa, u, mkv, w_g, w_d, w_x, w_o, g_post, seq, tm):
    m = x2.shape[0]
    steps_per_batch = seq // tm
    d = D_MODEL

    def rmap(cb):
        return lambda i: (i, cb)

    wspec = pl.BlockSpec((d, d), lambda i: (0, 0))
    return pl.pallas_call(
        _merge_kernel,
        out_shape=jax.ShapeDtypeStruct((m, d), jnp.float32),
        grid=(m // tm,),
        in_specs=[
            pl.BlockSpec((tm, d), rmap(0)),
            pl.BlockSpec((tm, d), rmap(0)),
            pl.BlockSpec((tm, d), rmap(0)),
            pl.BlockSpec((tm, d), rmap(COL_XQ // d)),
            pl.BlockSpec((tm, d), rmap(COL_XZ // d)),
            pl.BlockSpec((tm, d), rmap(COL_SG // d)),
            pl.BlockSpec((tm, d), rmap(COL_SD // d)),
            pl.BlockSpec((tm, d), rmap(COL_SM // d)),
            pl.BlockSpec((N_MEM, 2 * X_HEADS * X_HD), lambda i: (i // steps_per_batch, 0)),
            wspec, wspec, wspec, wspec,
            pl.BlockSpec((1, d), lambda i: (0, 0)),
        ],
        out_specs=pl.BlockSpec((tm, d), rmap(0)),
        scratch_shapes=[pltpu.VMEM((tm, d), MXU_DTYPE)],
        compiler_params=pltpu.CompilerParams(
            dimension_semantics=("arbitrary",),
            vmem_limit_bytes=VMEM_LIMIT),
        name="merge",
    )(x2, y_gla, y_dsa, u, u, u, u, u, mkv, w_g, w_d, w_x, w_o, g_post)


def _relayout_w_in(w_in):
    offs = np.cumsum(np.array(SPLIT_SIZES))[:-1].tolist()
    (gq, gk, gv, ga, gz, dq, dk, dv, iq, ik, iw, dz, xq, xz, gates) = jnp.split(w_in, offs, axis=1)
    main = jnp.concatenate([gq, gk, gv, gz, dq, dz, xq, xz, gates, dk, dv, iq], axis=1)
    pad = jnp.zeros((w_in.shape[0], LANES - GLA_RANK - IDX_HEADS), w_in.dtype)
    small = jnp.concatenate([ik, ik, ga, iw, pad], axis=1)
    return main.astype(MXU_DTYPE), small.astype(MXU_DTYPE)


def _col_scale():
    s = np.ones((1, U_COLS), np.float32)
    s[:, COL_GQ:COL_GQ + GLA_HEADS * GLA_DK] = GLA_DK ** -0.5
    s[:, COL_DQ:COL_DQ + DSA_HEADS * DSA_HD] = DSA_HD ** -0.5
    s[:, COL_XQ:COL_XQ + X_HEADS * X_HD] = X_HD ** -0.5
    return jnp.asarray(s)


def _layer(x2, mem2, g_pre, g_post, g_mem, w_in, w_up, b_a, g_gla, rel_bias, w_kv,
           w_g, w_d, w_x, w_o, batch, seq):
    w_main, w_small = _relayout_w_in(w_in)
    tm = min(1024, batch * seq)
    u, ikd, small = _proj(x2, g_pre.reshape(1, -1), w_main, w_small, _col_scale(), tm, 1024)
    y_gla = _gla(u, small, w_up.astype(MXU_DTYPE), b_a.reshape(1, -1), g_gla.reshape(1, -1),
                 batch, seq, chunks_per_step=4)
    y_dsa = _dsa(u, ikd, small, rel_bias, batch, seq)
    mkv = _memkv(mem2, g_mem.reshape(1, -1), w_kv.astype(MXU_DTYPE), batch)
    return _merge(x2, y_gla, y_dsa, u, mkv, w_g.astype(MXU_DTYPE), w_d.astype(MXU_DTYPE),
                  w_x.astype(MXU_DTYPE), w_o.astype(MXU_DTYPE), g_post.reshape(1, -1), seq, 256)


def kernel(x, mem, g_pre, g_post, g_mem, w_in, w_gla_a_up, b_gla_a, g_gla, rel_bias,
           w_mem_kv, w_gla_out, w_dsa_out, w_x_out, w_o):
    batch, seq, d = x.shape
    x2 = x.reshape(batch * seq, d)
    mem2 = mem.reshape(batch * N_MEM, d)
    for i in range(g_pre.shape[0]):
        x2 = _layer(x2, mem2, g_pre[i], g_post[i], g_mem[i], w_in[i], w_gla_a_up[i],
                    b_gla_a[i], g_gla[i], rel_bias, w_mem_kv[i], w_gla_out[i],
                    w_dsa_out[i], w_x_out[i], w_o[i], batch, seq)
    return x2.reshape(batch, seq, d)
```

```python
import functools
import math

import jax
import jax.numpy as jnp
import numpy as np
from jax import lax
from jax.experimental import pallas as pl
from jax.experimental.pallas import tpu as pltpu

D_MODEL = 1024
N_MEM = 256
EPS = 1e-6
GLA_HEADS = 4
GLA_DK = 128
GLA_DV = 256
GLA_RANK = 16
GLA_GATE_NORM = 16.0
GLA_CHUNK = 64
DSA_HEADS = 8
DSA_KV_HEADS = 2
DSA_GROUPS = DSA_HEADS // DSA_KV_HEADS
DSA_HD = 128
IDX_HEADS = 8
IDX_DIM = 64
TOPK_MAX = 256
Q_BLOCK = 128
REL_BUCKETS = 32
REL_MAX_DIST = 128
X_HEADS = 4
X_HD = 256

SPLIT_SIZES = (512, 512, 1024, 16, 1024, 1024, 256, 256, 512, 64, 8, 1024, 1024, 1024, 3072)

LANES = 128
SUBLANES = 8

MXU_DTYPE = jnp.bfloat16
ACT_DTYPE = jnp.bfloat16

U_COLS = 11264
COL_GQ, COL_GK, COL_GV, COL_GZ = 0, 512, 1024, 2048
COL_DQ, COL_DZ, COL_XQ, COL_XZ = 3072, 4096, 5120, 6144
COL_SG, COL_SD, COL_SM = 7168, 8192, 9216
COL_DK, COL_DV, COL_IQ = 10240, 10496, 10752
SMALL_GA, SMALL_IW = 0, 16

MASK_NEG = -1e30
INT_MIN = -(2 ** 31)
VMEM_LIMIT = 56 * 1024 * 1024


def _nt_dot(a, b):
    return lax.dot_general(a, b, (((1,), (1,)), ((), ())),
                           preferred_element_type=jnp.float32)


def _tn_dot(a, b):
    return lax.dot_general(a, b, (((0,), (0,)), ((), ())),
                           preferred_element_type=jnp.float32)


def _dot(a, b):
    return jnp.dot(a, b, preferred_element_type=jnp.float32)


def _silu(z):
    return z * jax.nn.sigmoid(z)


def _proj_kernel(x_ref, g_ref, w_ref, ws_ref, cs_ref, u_ref, ikd_ref, sm_ref, h_ref):
    @pl.when(pl.program_id(1) == 0)
    def _():
        xf = x_ref[...]
        y = xf * lax.rsqrt(jnp.mean(xf * xf, axis=-1, keepdims=True) + EPS)
        hb = (y * g_ref[...]).astype(MXU_DTYPE)
        h_ref[...] = hb
        r = _dot(hb, ws_ref[...])
        ikd_ref[...] = r[:, :LANES].astype(ikd_ref.dtype)
        sm_ref[...] = r[:, LANES:]

    acc = _dot(h_ref[...], w_ref[...])
    u_ref[...] = (acc * cs_ref[...]).astype(u_ref.dtype)


def _proj(x2, g_pre, w_main, w_small, col_scale, tm, tn):
    m = x2.shape[0]
    grid = (m // tm, U_COLS // tn)
    return pl.pallas_call(
        _proj_kernel,
        out_shape=(jax.ShapeDtypeStruct((m, U_COLS), ACT_DTYPE),
                   jax.ShapeDtypeStruct((m, LANES), ACT_DTYPE),
                   jax.ShapeDtypeStruct((m, LANES), jnp.float32)),
        grid=grid,
        in_specs=[
            pl.BlockSpec((tm, D_MODEL), lambda i, j: (i, 0)),
            pl.BlockSpec((1, D_MODEL), lambda i, j: (0, 0)),
            pl.BlockSpec((D_MODEL, tn), lambda i, j: (0, j)),
            pl.BlockSpec((D_MODEL, 2 * LANES), lambda i, j: (0, 0)),
            pl.BlockSpec((1, tn), lambda i, j: (0, j)),
        ],
        out_specs=(
            pl.BlockSpec((tm, tn), lambda i, j: (i, j)),
            pl.BlockSpec((tm, LANES), lambda i, j: (i, 0)),
            pl.BlockSpec((tm, LANES), lambda i, j: (i, 0)),
        ),
        scratch_shapes=[pltpu.VMEM((tm, D_MODEL), MXU_DTYPE)],
        compiler_params=pltpu.CompilerParams(
            dimension_semantics=("arbitrary", "arbitrary"),
            vmem_limit_bytes=VMEM_LIMIT),
        name="proj",
    )(x2, g_pre, w_main, w_small, col_scale)


def _split3(x):
    hi = x.astype(MXU_DTYPE)
    r1 = x - hi.astype(jnp.float32)
    mid = r1.astype(MXU_DTYPE)
    lo = (r1 - mid.astype(jnp.float32)).astype(MXU_DTYPE)
    return hi, mid, lo


def _gla_kernel(q_ref, k_ref, v_ref, z_ref, sm_ref, wup_ref, ba_ref, gg_ref,
                o_ref, st_ref, *, chunks_per_step):
    c = GLA_CHUNK

    @pl.when(pl.program_id(1) == 0)
    def _():
        st_ref[...] = jnp.zeros_like(st_ref)

    row = lax.broadcasted_iota(jnp.int32, (c, c), 0)
    col = lax.broadcasted_iota(jnp.int32, (c, c), 1)
    causal = col <= row
    tril = jnp.where(causal, 1.0, 0.0).astype(MXU_DTYPE)

    for ci in range(chunks_per_step):
        r0 = ci * c
        ga = sm_ref[r0:r0 + c, SMALL_GA:SMALL_GA + GLA_RANK].astype(MXU_DTYPE)
        pre = _dot(ga, wup_ref[...]) + ba_ref[...]
        log_a = (jnp.minimum(pre, 0.0) - jnp.log1p(jnp.exp(-jnp.abs(pre)))) / GLA_GATE_NORM
        hi, mid, lo = _split3(log_a)
        b_all = _dot(tril, hi) + _dot(tril, mid) + _dot(tril, lo)
        for h in range(GLA_HEADS):
            ks = slice(h * GLA_DK, (h + 1) * GLA_DK)
            vs = slice(h * GLA_DV, (h + 1) * GLA_DV)
            b = b_all[:, ks]
            b_last = b[c - 1:c, :]
            q = q_ref[r0:r0 + c, ks].astype(jnp.float32)
            k = k_ref[r0:r0 + c, ks].astype(jnp.float32)
            v = v_ref[r0:r0 + c, vs]
            q_d = (q * jnp.exp(b)).astype(MXU_DTYPE)
            k_d = (k * jnp.exp(-b)).astype(MXU_DTYPE)
            k_t = (k * jnp.exp(b_last - b)).astype(MXU_DTYPE)
            decay = jnp.exp(b_last)
            att = _nt_dot(q_d, k_d)
            att = jnp.where(causal, att, 0.0).astype(MXU_DTYPE)
            st = st_ref[h]
            o = _dot(att, v) + _nt_dot(q_d, st.astype(MXU_DTYPE))
            st_ref[h] = st * decay + _tn_dot(v, k_t)
            y = o * lax.rsqrt(jnp.mean(o * o, axis=-1, keepdims=True) + EPS)
            y = y * gg_ref[...]
            zz = z_ref[r0:r0 + c, vs].astype(jnp.float32)
            o_ref[r0:r0 + c, vs] = (y * _silu(zz)).astype(o_ref.dtype)


def _gla(u, small, w_up, b_a, g_gla, batch, seq, chunks_per_step):
    rows = chunks_per_step * GLA_CHUNK
    steps = seq // rows
    hk = GLA_HEADS * GLA_DK
    hv = GLA_HEADS * GLA_DV

    def rmap(cb):
        return lambda b, s: (b * steps + s, cb)

    return pl.pallas_call(
        functools.partial(_gla_kernel, chunks_per_step=chunks_per_step),
        out_shape=jax.ShapeDtypeStruct((batch * seq, hv), ACT_DTYPE),
        grid=(batch, steps),
        in_specs=[
            pl.BlockSpec((rows, hk), rmap(COL_GQ // hk)),
            pl.BlockSpec((rows, hk), rmap(COL_GK // hk)),
            pl.BlockSpec((rows, hv), rmap(COL_GV // hv)),
            pl.BlockSpec((rows, hv), rmap(COL_GZ // hv)),
            pl.BlockSpec((rows, LANES), rmap(0)),
            pl.BlockSpec((GLA_RANK, hk), lambda b, s: (0, 0)),
            pl.BlockSpec((1, hk), lambda b, s: (0, 0)),
            pl.BlockSpec((1, GLA_DV), lambda b, s: (0, 0)),
        ],
        out_specs=pl.BlockSpec((rows, hv), rmap(0)),
        scratch_shapes=[pltpu.VMEM((GLA_HEADS, GLA_DV, GLA_DK), jnp.float32)],
        compiler_params=pltpu.CompilerParams(
            dimension_semantics=("arbitrary", "arbitrary"),
            vmem_limit_bytes=VMEM_LIMIT),
        name="gla",
    )(u, u, u, u, small, w_up, b_a, g_gla)


KEY_CHUNK = 4 * Q_BLOCK
BLOCKS_PER_CHUNK = KEY_CHUNK // Q_BLOCK
PLANE_GROUP = 32 * SUBLANES


def _t5_bucket(dist):
    max_exact = REL_BUCKETS // 2
    d = jnp.maximum(dist, 1).astype(jnp.float32)
    large = max_exact + (jnp.log(d / max_exact) / math.log(REL_MAX_DIST / max_exact)
                         * (REL_BUCKETS - max_exact)).astype(jnp.int32)
    large = jnp.minimum(large, REL_BUCKETS - 1)
    return jnp.where(dist < max_exact, dist, large)


def _dsa_kernel(rb_ref, dq_ref, dk_ref, dv_ref, iq_ref, ikd_ref, sm_ref, dz_ref,
                out_ref,
                kt_ref, planes_ref, mb_ref, vt_ref, bt_ref, iqm_ref, qaug_ref, acc_ref, pm_ref,
                *, seq):
    qb = pl.program_id(1)
    blk = Q_BLOCK
    ch = KEY_CHUNK
    k_sel = min(TOPK_MAX, seq // 4)
    n_chunks = qb // BLOCKS_PER_CHUNK + 1
    width = DSA_GROUPS * blk

    row_i = lax.broadcasted_iota(jnp.int32, (blk, blk), 0)
    col_i = lax.broadcasted_iota(jnp.int32, (blk, blk), 1)
    crow_i = lax.broadcasted_iota(jnp.int32, (ch, blk), 0)
    q_pos = qb * blk + lax.broadcasted_iota(jnp.int32, (ch, blk), 1)

    @pl.when((pl.program_id(0) == 0) & (qb == 0))
    def _():
        planes_ref[...] = jnp.zeros_like(planes_ref)
        for delta in range(3):
            dist = jnp.maximum(delta * blk + col_i - row_i, 0)
            bucket = _t5_bucket(dist)
            for h in range(DSA_HEADS):
                tile = jnp.zeros((blk, blk), jnp.float32)
                for bk in range(REL_BUCKETS):
                    tile = jnp.where(bucket == bk, rb_ref[bk, h], tile)
                cc, g = divmod(h, DSA_GROUPS)
                bt_ref[delta, cc, :, g * blk:(g + 1) * blk] = tile

    @pl.when(qb == 0)
    def _():
        def body(i, carry):
            r = pl.multiple_of(i * ch, ch)
            vt_ref[i] = dv_ref[pl.ds(r, ch), :].astype(jnp.float32).T.astype(vt_ref.dtype)
            return carry
        lax.fori_loop(0, seq // ch, body, 0)

    w_t = sm_ref[...].T[SMALL_IW:SMALL_IW + IDX_HEADS, :]
    w_t = w_t * (IDX_HEADS ** -0.5) * (IDX_DIM ** -0.5)

    lane = lax.broadcasted_iota(jnp.int32, (blk, LANES), 1)
    for h in range(IDX_HEADS):
        pair = iq_ref[:, (h // 2) * LANES:(h // 2 + 1) * LANES]
        keep = (lane < IDX_DIM) if h % 2 == 0 else (lane >= IDX_DIM)
        iqm_ref[h] = jnp.where(keep, pair, jnp.zeros_like(pair))

    def score_chunk(c, causal):
        r = pl.multiple_of(c * ch, ch)
        ikc = ikd_ref[pl.ds(r, ch), :]
        acc = None
        for h in range(IDX_HEADS):
            t = w_t[h:h + 1, :] * jnp.maximum(_nt_dot(ikc, iqm_ref[h]), 0.0)
            acc = t if acc is None else acc + t
        if causal:
            acc = jnp.where(r + crow_i <= q_pos, acc, -jnp.inf)
        bits = lax.bitcast_convert_type(acc, jnp.int32)
        kt_ref[pl.ds(r, ch), :] = bits ^ ((bits >> 31) & 0x7FFFFFFF)

    def score_body(c, carry):
        score_chunk(c, False)
        return carry
    lax.fori_loop(0, n_chunks - 1, score_body, 0)
    score_chunk(n_chunks - 1, True)

    def count(pred_fn):
        def body(ci, acc):
            r = pl.multiple_of(ci * ch, ch)
            keys = kt_ref[pl.ds(r, ch), :]
            hit = jnp.where(pred_fn(keys, r), 1, 0).astype(jnp.int32)
            return acc + jnp.sum(hit.reshape(ch // SUBLANES, SUBLANES, blk), axis=0)
        acc = lax.fori_loop(0, n_chunks, body, jnp.zeros((SUBLANES, blk), jnp.int32))
        return jnp.sum(acc, axis=0, keepdims=True)

    def plane_body(c, carry):
        for g in range(ch // PLANE_GROUP):
            base = c * ch + g * PLANE_GROUP
            a = [kt_ref[pl.ds(pl.multiple_of(base + SUBLANES * v, SUBLANES), SUBLANES), :]
                 for v in range(32)]
            j, m = 16, 0x0000FFFF
            while j:
                k0 = 0
                while k0 < 32:
                    t = (a[k0] ^ lax.shift_right_logical(a[k0 + j], jnp.int32(j))) & m
                    a[k0] = a[k0] ^ t
                    a[k0 + j] = a[k0 + j] ^ (t << j)
                    k0 = (k0 + j + 1) & ~j
                j >>= 1
                m = (m ^ (m << j)) & 0xFFFFFFFF if j else m
                m = m - (1 << 32) if m >= (1 << 31) else m
            a[0] = ~a[0]
            row = pl.multiple_of(c * (ch // 32) + g * SUBLANES, SUBLANES)
            for jj in range(32):
                planes_ref[jj, pl.ds(row, SUBLANES), :] = a[jj]
        return carry
    lax.fori_loop(0, n_chunks, plane_body, 0)

    n_rows = seq // 32

    def rowsum(x):
        part = jnp.sum(x.reshape(n_rows // SUBLANES, SUBLANES, blk), axis=0)
        return jnp.sum(part, axis=0, keepdims=True)

    prow = lax.broadcasted_iota(jnp.int32, (n_rows, blk), 0)
    alive0 = jnp.where(prow < n_chunks * (ch // 32), -1, 0).astype(jnp.int32)

    def bit_body(j, carry):
        alive, cnt_gt, ukey = carry
        w = planes_ref[j]
        ones = alive & w
        c1 = rowsum(lax.population_count(ones))
        take = cnt_gt + c1 >= k_sel
        alive = jnp.where(take, ones, alive & ~w)
        cnt_gt = jnp.where(take, cnt_gt, cnt_gt + c1)
        ukey = jnp.where(take, ukey | (jnp.int32(1) << (31 - j)), ukey)
        return alive, cnt_gt, ukey

    zero_row = jnp.zeros((1, blk), jnp.int32)
    alive, cnt_gt, ukey = lax.fori_loop(0, 32, bit_body, (alive0, zero_row, zero_row))
    ans = ukey ^ INT_MIN
    cnt_ge = cnt_gt + rowsum(lax.population_count(alive))
    need = k_sel - cnt_gt

    pm_ref[...] = jnp.full(pm_ref.shape, 2 * seq, jnp.int32)

    @pl.when(jnp.max(cnt_ge) > k_sel)
    def _():
        n_bits = (2 * seq - 1).bit_length()

        def pos_body(i, p):
            cand = p | (jnp.int32(1) << (n_bits - 1 - i))
            cnt = count(lambda keys, r: (keys == ans) & (r + crow_i < cand))
            return jnp.where(cnt <= need - 1, cand, p)
        p_max = lax.fori_loop(0, n_bits, pos_body, jnp.zeros((1, blk), jnp.int32))
        pm_ref[...] = jnp.broadcast_to(p_max, pm_ref.shape)

    p_max = pm_ref[0:1, :]

    def mask_body(c, carry):
        r = pl.multiple_of(c * ch, ch)
        keys = kt_ref[pl.ds(r, ch), :]
        k_pos = r + crow_i
        sel = (keys > ans) | ((keys == ans) & (k_pos <= p_max))
        sel = sel & (k_pos <= q_pos)
        mb_ref[pl.ds(r, ch), :] = jnp.where(sel, 0.0, MASK_NEG).astype(mb_ref.dtype)
        return carry
    lax.fori_loop(0, n_chunks, mask_body, 0)

    eye = jnp.where(row_i == col_i, 1.0, 0.0).astype(MXU_DTYPE)
    for cc in range(DSA_KV_HEADS):
        for g in range(DSA_GROUPS):
            h = cc * DSA_GROUPS + g
            qaug_ref[cc, g * blk:(g + 1) * blk, 0:DSA_HD] = dq_ref[:, h * DSA_HD:(h + 1) * DSA_HD]
            qaug_ref[cc, g * blk:(g + 1) * blk, DSA_HD:2 * DSA_HD] = eye
    acc_ref[...] = jnp.zeros_like(acc_ref)

    def attn_chunk(c, carry, far):
        r = pl.multiple_of(c * ch, ch)
        mbc = mb_ref[pl.ds(r, ch), :]
        out = []
        for cc in range(DSA_KV_HEADS):
            m, l = carry[2 * cc], carry[2 * cc + 1]
            kaug = jnp.concatenate([dk_ref[pl.ds(r, ch), cc * DSA_HD:(cc + 1) * DSA_HD], mbc], axis=1)
            lg = _nt_dot(kaug, qaug_ref[cc])
            if far:
                cvec = bt_ref[2, cc, 0:1, :]
                m_new = jnp.maximum(m, jnp.max(lg, axis=0, keepdims=True) + cvec)
                p = jnp.exp(lg - (m_new - cvec))
            else:
                bias = jnp.concatenate(
                    [bt_ref[jnp.clip(qb - (c * BLOCKS_PER_CHUNK + j), 0, 2), cc]
                     for j in range(BLOCKS_PER_CHUNK)], axis=0)
                lg = lg + bias
                m_new = jnp.maximum(m, jnp.max(lg, axis=0, keepdims=True))
                p = jnp.exp(lg - m_new)
            alpha = jnp.exp(m - m_new)
            l_new = alpha * l + jnp.sum(p, axis=0, keepdims=True)
            pv = _dot(vt_ref[c, cc * DSA_HD:(cc + 1) * DSA_HD, :], p.astype(MXU_DTYPE))
            acc_ref[cc] = acc_ref[cc] * alpha + pv
            out += [m_new, l_new]
        return tuple(out)

    c_near = jnp.maximum(qb - 1, 0) // BLOCKS_PER_CHUNK
    m0 = jnp.full((1, width), -jnp.inf, jnp.float32)
    l0 = jnp.zeros((1, width), jnp.float32)
    carry = lax.fori_loop(0, c_near, functools.partial(attn_chunk, far=True), (m0, l0, m0, l0))
    carry = lax.fori_loop(c_near, n_chunks, functools.partial(attn_chunk, far=False), carry)

    for cc in range(DSA_KV_HEADS):
        o_t = acc_ref[cc] / carry[2 * cc + 1]
        for g in range(DSA_GROUPS):
            h = cc * DSA_GROUPS + g
            o = o_t[:, g * blk:(g + 1) * blk].T
            zz = dz_ref[:, h * DSA_HD:(h + 1) * DSA_HD].astype(jnp.float32)
            out_ref[:, h * DSA_HD:(h + 1) * DSA_HD] = (o * _silu(zz)).astype(out_ref.dtype)


def _dsa(u, ikd, small, rel_bias, batch, seq):
    nb = seq // Q_BLOCK
    hq = DSA_HEADS * DSA_HD
    hkv = DSA_KV_HEADS * DSA_HD
    hi = IDX_HEADS * IDX_DIM

    def qmap(cb):
        return lambda b, q: (b * nb + q, cb)

    def bmap(cb):
        return lambda b, q: (b, cb)

    return pl.pallas_call(
        functools.partial(_dsa_kernel, seq=seq),
        out_shape=jax.ShapeDtypeStruct((batch * seq, hq), ACT_DTYPE),
        grid=(batch, nb),
        in_specs=[
            pl.BlockSpec(memory_space=pltpu.SMEM),
            pl.BlockSpec((Q_BLOCK, hq), qmap(COL_DQ // hq)),
            pl.BlockSpec((seq, hkv), bmap(COL_DK // hkv)),
            pl.BlockSpec((seq, hkv), bmap(COL_DV // hkv)),
            pl.BlockSpec((Q_BLOCK, hi), qmap(COL_IQ // hi)),
            pl.BlockSpec((seq, LANES), bmap(0)),
            pl.BlockSpec((Q_BLOCK, LANES), qmap(0)),
            pl.BlockSpec((Q_BLOCK, hq), qmap(COL_DZ // hq)),
        ],
        out_specs=pl.BlockSpec((Q_BLOCK, hq), qmap(0)),
        scratch_shapes=[
            pltpu.VMEM((seq, Q_BLOCK), jnp.int32),
            pltpu.VMEM((32, seq // 32, Q_BLOCK), jnp.int32),
            pltpu.VMEM((seq, Q_BLOCK), MXU_DTYPE),
            pltpu.VMEM((seq // KEY_CHUNK, hkv, KEY_CHUNK), MXU_DTYPE),
            pltpu.VMEM((3, DSA_KV_HEADS, Q_BLOCK, DSA_GROUPS * Q_BLOCK), jnp.float32),
            pltpu.VMEM((IDX_HEADS, Q_BLOCK, LANES), MXU_DTYPE),
            pltpu.VMEM((DSA_KV_HEADS, DSA_GROUPS * Q_BLOCK, 2 * DSA_HD), MXU_DTYPE),
            pltpu.VMEM((DSA_KV_HEADS, DSA_HD, DSA_GROUPS * Q_BLOCK), jnp.float32),
            pltpu.VMEM((SUBLANES, Q_BLOCK), jnp.int32),
        ],
        compiler_params=pltpu.CompilerParams(
            dimension_semantics=("arbitrary", "arbitrary"),
            vmem_limit_bytes=VMEM_LIMIT),
        name="dsa",
    )(rel_bias, u, u, u, u, ikd, small, u)


def _memkv_kernel(mem_ref, g_ref, w_ref, o_ref):
    xf = mem_ref[...]
    y = xf * lax.rsqrt(jnp.mean(xf * xf, axis=-1, keepdims=True) + EPS)
    hb = (y * g_ref[...]).astype(MXU_DTYPE)
    o_ref[...] = _dot(hb, w_ref[...]).astype(o_ref.dtype)


def _memkv(mem2, g_mem, w_kv, batch):
    n = 2 * X_HEADS * X_HD
    return pl.pallas_call(
        _memkv_kernel,
        out_shape=jax.ShapeDtypeStruct((batch * N_MEM, n), ACT_DTYPE),
        grid=(batch,),
        in_specs=[
            pl.BlockSpec((N_MEM, D_MODEL), lambda b: (b, 0)),
            pl.BlockSpec((1, D_MODEL), lambda b: (0, 0)),
            pl.BlockSpec((D_MODEL, n), lambda b: (0, 0)),
        ],
        out_specs=pl.BlockSpec((N_MEM, n), lambda b: (b, 0)),
        compiler_params=pltpu.CompilerParams(
            dimension_semantics=("arbitrary",),
            vmem_limit_bytes=VMEM_LIMIT),
        name="memkv",
    )(mem2, g_mem, w_kv)


def _merge_kernel(x_ref, yg_ref, yd_ref, xq_ref, xz_ref, sg_ref, sd_ref, sm_ref, mkv_ref,
                  wg_ref, wd_ref, wx_ref, wo_ref, gp_ref, o_ref, ym_ref):
    hw = X_HEADS * X_HD
    for h in range(X_HEADS):
        cs = slice(h * X_HD, (h + 1) * X_HD)
        mk = mkv_ref[:, h * X_HD:(h + 1) * X_HD]
        mv = mkv_ref[:, hw + h * X_HD:hw + (h + 1) * X_HD]
        lg = _nt_dot(xq_ref[:, cs], mk)
        lg = lg - jnp.max(lg, axis=-1, keepdims=True)
        e = jnp.exp(lg)
        p = (e / jnp.sum(e, axis=-1, keepdims=True)).astype(MXU_DTYPE)
        o = _dot(p, mv)
        ym_ref[:, cs] = (o * _silu(xz_ref[:, cs].astype(jnp.float32))).astype(ym_ref.dtype)

    merged = jax.nn.sigmoid(sg_ref[...].astype(jnp.float32)) * _dot(yg_ref[...], wg_ref[...])
    merged = merged + jax.nn.sigmoid(sd_ref[...].astype(jnp.float32)) * _dot(yd_ref[...], wd_ref[...])
    merged = merged + jax.nn.sigmoid(sm_ref[...].astype(jnp.float32)) * _dot(ym_ref[...], wx_ref[...])
    t = _dot(merged.astype(MXU_DTYPE), wo_ref[...])
    y = t * lax.rsqrt(jnp.mean(t * t, axis=-1, keepdims=True) + EPS)
    o_ref[...] = x_ref[...] + y * gp_ref[...]


def _merge(x2, y_gla, y_dsa, u, mkv, w_g, w_d, w_x, w_o, g_post, seq, tm):
    m = x2.shape[0]
    steps_per_batch = seq // tm
    d = D_MODEL

    def rmap(cb):
        return lambda i: (i, cb)

    wspec = pl.BlockSpec((d, d), lambda i: (0, 0))
    return pl.pallas_call(
        _merge_kernel,
        out_shape=jax.ShapeDtypeStruct((m, d), jnp.float32),
        grid=(m // tm,),
        in_specs=[
            pl.BlockSpec((tm, d), rmap(0)),
            pl.BlockSpec((tm, d), rmap(0)),
            pl.BlockSpec((tm, d), rmap(0)),
            pl.BlockSpec((tm, d), rmap(COL_XQ // d)),
            pl.BlockSpec((tm, d), rmap(COL_XZ // d)),
            pl.BlockSpec((tm, d), rmap(COL_SG // d)),
            pl.BlockSpec((tm, d), rmap(COL_SD // d)),
            pl.BlockSpec((tm, d), rmap(COL_SM // d)),
            pl.BlockSpec((N_MEM, 2 * X_HEADS * X_HD), lambda i: (i // steps_per_batch, 0)),
            wspec, wspec, wspec, wspec,
            pl.BlockSpec((1, d), lambda i: (0, 0)),
        ],
        out_specs=pl.BlockSpec((tm, d), rmap(0)),
        scratch_shapes=[pltpu.VMEM((tm, d), MXU_DTYPE)],
        compiler_params=pltpu.CompilerParams(
            dimension_semantics=("arbitrary",),
            vmem_limit_bytes=VMEM_LIMIT),
        name="merge",
    )(x2, y_gla, y_dsa, u, u, u, u, u, mkv, w_g, w_d, w_x, w_o, g_post)


def _relayout_w_in(w_in):
    offs = np.cumsum(np.array(SPLIT_SIZES))[:-1].tolist()
    (gq, gk, gv, ga, gz, dq, dk, dv, iq, ik, iw, dz, xq, xz, gates) = jnp.split(w_in, offs, axis=1)
    main = jnp.concatenate([gq, gk, gv, gz, dq, dz, xq, xz, gates, dk, dv, iq], axis=1)
    pad = jnp.zeros((w_in.shape[0], LANES - GLA_RANK - IDX_HEADS), w_in.dtype)
    small = jnp.concatenate([ik, ik, ga, iw, pad], axis=1)
    return main.astype(MXU_DTYPE), small.astype(MXU_DTYPE)


def _col_scale():
    s = np.ones((1, U_COLS), np.float32)
    s[:, COL_GQ:COL_GQ + GLA_HEADS * GLA_DK] = GLA_DK ** -0.5
    s[:, COL_DQ:COL_DQ + DSA_HEADS * DSA_HD] = DSA_HD ** -0.5
    s[:, COL_XQ:COL_XQ + X_HEADS * X_HD] = X_HD ** -0.5
    return jnp.asarray(s)


def _layer(x2, mem2, g_pre, g_post, g_mem, w_in, w_up, b_a, g_gla, rel_bias, w_kv,
           w_g, w_d, w_x, w_o, batch, seq):
    w_main, w_small = _relayout_w_in(w_in)
    tm = min(1024, batch * seq)
    u, ikd, small = _proj(x2, g_pre.reshape(1, -1), w_main, w_small, _col_scale(), tm, 1024)
    y_gla = _gla(u, small, w_up.astype(MXU_DTYPE), b_a.reshape(1, -1), g_gla.reshape(1, -1),
                 batch, seq, chunks_per_step=4)
    y_dsa = _dsa(u, ikd, small, rel_bias, batch, seq)
    mkv = _memkv(mem2, g_mem.reshape(1, -1), w_kv.astype(MXU_DTYPE), batch)
    return _merge(x2, y_gla, y_dsa, u, mkv, w_g.astype(MXU_DTYPE), w_d.astype(MXU_DTYPE),
                  w_x.astype(MXU_DTYPE), w_o.astype(MXU_DTYPE), g_post.reshape(1, -1), seq, 256)


def kernel(x, mem, g_pre, g_post, g_mem, w_in, w_gla_a_up, b_gla_a, g_gla, rel_bias,
           w_mem_kv, w_gla_out, w_dsa_out, w_x_out, w_o):
    batch, seq, d = x.shape
    x2 = x.reshape(batch * seq, d)
    mem2 = mem.reshape(batch * N_MEM, d)
    for i in range(g_pre.shape[0]):
        x2 = _layer(x2, mem2, g_pre[i], g_post[i], g_mem[i], w_in[i], w_gla_a_up[i],
                    b_gla_a[i], g_gla[i], rel_bias, w_mem_kv[i], w_gla_out[i],
                    w_dsa_out[i], w_x_out[i], w_o[i], batch, seq)
    return x2.reshape(batch, seq, d)
```

```python
import functools
import math

import jax
import jax.numpy as jnp
import numpy as np
from jax import lax
from jax.experimental import pallas as pl
from jax.experimental.pallas import tpu as pltpu

D_MODEL = 1024
N_MEM = 256
EPS = 1e-6
GLA_HEADS = 4
GLA_DK = 128
GLA_DV = 256
GLA_RANK = 16
GLA_GATE_NORM = 16.0
GLA_CHUNK = 64
DSA_HEADS = 8
DSA_KV_HEADS = 2
DSA_GROUPS = DSA_HEADS // DSA_KV_HEADS
DSA_HD = 128
IDX_HEADS = 8
IDX_DIM = 64
TOPK_MAX = 256
Q_BLOCK = 128
REL_BUCKETS = 32
REL_MAX_DIST = 128
X_HEADS = 4
X_HD = 256

SPLIT_SIZES = (512, 512, 1024, 16, 1024, 1024, 256, 256, 512, 64, 8, 1024, 1024, 1024, 3072)

LANES = 128
SUBLANES = 8

MXU_DTYPE = jnp.bfloat16
ACT_DTYPE = jnp.bfloat16

U_COLS = 11264
COL_GQ, COL_GK, COL_GV, COL_GZ = 0, 512, 1024, 2048
COL_DQ, COL_DZ, COL_XQ, COL_XZ = 3072, 4096, 5120, 6144
COL_SG, COL_SD, COL_SM = 7168, 8192, 9216
COL_DK, COL_DV, COL_IQ = 10240, 10496, 10752
SMALL_GA, SMALL_IW = 0, 16

MASK_NEG = -1e30
INT_MIN = -(2 ** 31)
VMEM_LIMIT = 56 * 1024 * 1024


def _nt_dot(a, b):
    return lax.dot_general(a, b, (((1,), (1,)), ((), ())),
                           preferred_element_type=jnp.float32)


def _tn_dot(a, b):
    return lax.dot_general(a, b, (((0,), (0,)), ((), ())),
                           preferred_element_type=jnp.float32)


def _dot(a, b):
    return jnp.dot(a, b, preferred_element_type=jnp.float32)


def _silu(z):
    return z * jax.nn.sigmoid(z)


def _proj_kernel(x_ref, g_ref, w_ref, ws_ref, cs_ref, u_ref, ikd_ref, sm_ref, h_ref):
    @pl.when(pl.program_id(1) == 0)
    def _():
        xf = x_ref[...]
        y = xf * lax.rsqrt(jnp.mean(xf * xf, axis=-1, keepdims=True) + EPS)
        hb = (y * g_ref[...]).astype(MXU_DTYPE)
        h_ref[...] = hb
        r = _dot(hb, ws_ref[...])
        ikd_ref[...] = r[:, :LANES].astype(ikd_ref.dtype)
        sm_ref[...] = r[:, LANES:]

    acc = _dot(h_ref[...], w_ref[...])
    u_ref[...] = (acc * cs_ref[...]).astype(u_ref.dtype)


def _proj(x2, g_pre, w_main, w_small, col_scale, tm, tn):
    m = x2.shape[0]
    grid = (m // tm, U_COLS // tn)
    return pl.pallas_call(
        _proj_kernel,
        out_shape=(jax.ShapeDtypeStruct((m, U_COLS), ACT_DTYPE),
                   jax.ShapeDtypeStruct((m, LANES), ACT_DTYPE),
                   jax.ShapeDtypeStruct((m, LANES), jnp.float32)),
        grid=grid,
        in_specs=[
            pl.BlockSpec((tm, D_MODEL), lambda i, j: (i, 0)),
            pl.BlockSpec((1, D_MODEL), lambda i, j: (0, 0)),
            pl.BlockSpec((D_MODEL, tn), lambda i, j: (0, j)),
            pl.BlockSpec((D_MODEL, 2 * LANES), lambda i, j: (0, 0)),
            pl.BlockSpec((1, tn), lambda i, j: (0, j)),
        ],
        out_specs=(
            pl.BlockSpec((tm, tn), lambda i, j: (i, j)),
            pl.BlockSpec((tm, LANES), lambda i, j: (i, 0)),
            pl.BlockSpec((tm, LANES), lambda i, j: (i, 0)),
        ),
        scratch_shapes=[pltpu.VMEM((tm, D_MODEL), MXU_DTYPE)],
        compiler_params=pltpu.CompilerParams(
            dimension_semantics=("arbitrary", "arbitrary"),
            vmem_limit_bytes=VMEM_LIMIT),
        name="proj",
    )(x2, g_pre, w_main, w_small, col_scale)


def _split3(x):
    hi = x.astype(MXU_DTYPE)
    r1 = x - hi.astype(jnp.float32)
    mid = r1.astype(MXU_DTYPE)
    lo = (r1 - mid.astype(jnp.float32)).astype(MXU_DTYPE)
    return hi, mid, lo


def _gla_kernel(q_ref, k_ref, v_ref, z_ref, sm_ref, wup_ref, ba_ref, gg_ref,
                o_ref, st_ref, *, chunks_per_step):
    c = GLA_CHUNK

    @pl.when(pl.program_id(1) == 0)
    def _():
        st_ref[...] = jnp.zeros_like(st_ref)

    row = lax.broadcasted_iota(jnp.int32, (c, c), 0)
    col = lax.broadcasted_iota(jnp.int32, (c, c), 1)
    causal = col <= row
    tril = jnp.where(causal, 1.0, 0.0).astype(MXU_DTYPE)

    for ci in range(chunks_per_step):
        r0 = ci * c
        ga = sm_ref[r0:r0 + c, SMALL_GA:SMALL_GA + GLA_RANK].astype(MXU_DTYPE)
        pre = _dot(ga, wup_ref[...]) + ba_ref[...]
        log_a = (jnp.minimum(pre, 0.0) - jnp.log1p(jnp.exp(-jnp.abs(pre)))) / GLA_GATE_NORM
        hi, mid, lo = _split3(log_a)
        b_all = _dot(tril, hi) + _dot(tril, mid) + _dot(tril, lo)
        for h in range(GLA_HEADS):
            ks = slice(h * GLA_DK, (h + 1) * GLA_DK)
            vs = slice(h * GLA_DV, (h + 1) * GLA_DV)
            b = b_all[:, ks]
            b_last = b[c - 1:c, :]
            q = q_ref[r0:r0 + c, ks].astype(jnp.float32)
            k = k_ref[r0:r0 + c, ks].astype(jnp.float32)
            v = v_ref[r0:r0 + c, vs]
            q_d = (q * jnp.exp(b)).astype(MXU_DTYPE)
            k_d = (k * jnp.exp(-b)).astype(MXU_DTYPE)
            k_t = (k * jnp.exp(b_last - b)).astype(MXU_DTYPE)
            decay = jnp.exp(b_last)
            att = _nt_dot(q_d, k_d)
            att = jnp.where(causal, att, 0.0).astype(MXU_DTYPE)
            st = st_ref[h]
            o = _dot(att, v) + _nt_dot(q_d, st.astype(MXU_DTYPE))
            st_ref[h] = st * decay + _tn_dot(v, k_t)
            y = o * lax.rsqrt(jnp.mean(o * o, axis=-1, keepdims=True) + EPS)
            y = y * gg_ref[...]
            zz = z_ref[r0:r0 + c, vs].astype(jnp.float32)
            o_ref[r0:r0 + c, vs] = (y * _silu(zz)).astype(o_ref.dtype)


def _gla(u, small, w_up, b_a, g_gla, batch, seq, chunks_per_step):
    rows = chunks_per_step * GLA_CHUNK
    steps = seq // rows
    hk = GLA_HEADS * GLA_DK
    hv = GLA_HEADS * GLA_DV

    def rmap(cb):
        return lambda b, s: (b * steps + s, cb)

    return pl.pallas_call(
        functools.partial(_gla_kernel, chunks_per_step=chunks_per_step),
        out_shape=jax.ShapeDtypeStruct((batch * seq, hv), ACT_DTYPE),
        grid=(batch, steps),
        in_specs=[
            pl.BlockSpec((rows, hk), rmap(COL_GQ // hk)),
            pl.BlockSpec((rows, hk), rmap(COL_GK // hk)),
            pl.BlockSpec((rows, hv), rmap(COL_GV // hv)),
            pl.BlockSpec((rows, hv), rmap(COL_GZ // hv)),
            pl.BlockSpec((rows, LANES), rmap(0)),
            pl.BlockSpec((GLA_RANK, hk), lambda b, s: (0, 0)),
            pl.BlockSpec((1, hk), lambda b, s: (0, 0)),
            pl.BlockSpec((1, GLA_DV), lambda b, s: (0, 0)),
        ],
        out_specs=pl.BlockSpec((rows, hv), rmap(0)),
        scratch_shapes=[pltpu.VMEM((GLA_HEADS, GLA_DV, GLA_DK), jnp.float32)],
        compiler_params=pltpu.CompilerParams(
            dimension_semantics=("arbitrary", "arbitrary"),
            vmem_limit_bytes=VMEM_LIMIT),
        name="gla",
    )(u, u, u, u, small, w_up, b_a, g_gla)


KEY_CHUNK = 4 * Q_BLOCK
BLOCKS_PER_CHUNK = KEY_CHUNK // Q_BLOCK
PLANE_GROUP = 32 * SUBLANES


def _t5_bucket(dist):
    max_exact = REL_BUCKETS // 2
    d = jnp.maximum(dist, 1).astype(jnp.float32)
    large = max_exact + (jnp.log(d / max_exact) / math.log(REL_MAX_DIST / max_exact)
                         * (REL_BUCKETS - max_exact)).astype(jnp.int32)
    large = jnp.minimum(large, REL_BUCKETS - 1)
    return jnp.where(dist < max_exact, dist, large)


def _dsa_kernel(rb_ref, dq_ref, dk_ref, dv_ref, iq_ref, ikd_ref, sm_ref, dz_ref,
                out_ref,
                kt_ref, planes_ref, mb_ref, vt_ref, bt_ref, iqm_ref, qaug_ref, acc_ref, pm_ref,
                lga_ref, lgb_ref, mxa_ref, mxb_ref, ml_ref,
                *, seq):
    qb = pl.program_id(1)
    blk = Q_BLOCK
    ch = KEY_CHUNK
    k_sel = min(TOPK_MAX, seq // 4)
    n_chunks = qb // BLOCKS_PER_CHUNK + 1
    width = DSA_GROUPS * blk

    row_i = lax.broadcasted_iota(jnp.int32, (blk, blk), 0)
    col_i = lax.broadcasted_iota(jnp.int32, (blk, blk), 1)
    crow_i = lax.broadcasted_iota(jnp.int32, (ch, blk), 0)
    q_pos = qb * blk + lax.broadcasted_iota(jnp.int32, (ch, blk), 1)

    @pl.when((pl.program_id(0) == 0) & (qb == 0))
    def _():
        planes_ref[...] = jnp.zeros_like(planes_ref)
        for delta in range(3):
            dist = jnp.maximum(delta * blk + col_i - row_i, 0)
            bucket = _t5_bucket(dist)
            for h in range(DSA_HEADS):
                tile = jnp.zeros((blk, blk), jnp.float32)
                for bk in range(REL_BUCKETS):
                    tile = jnp.where(bucket == bk, rb_ref[bk, h], tile)
                cc, g = divmod(h, DSA_GROUPS)
                bt_ref[delta, cc, :, g * blk:(g + 1) * blk] = tile

    @pl.when(qb == 0)
    def _():
        def body(i, carry):
            r = pl.multiple_of(i * ch, ch)
            vt_ref[i] = dv_ref[pl.ds(r, ch), :].astype(jnp.float32).T.astype(vt_ref.dtype)
            return carry
        lax.fori_loop(0, seq // ch, body, 0)

    w_t = sm_ref[...].T[SMALL_IW:SMALL_IW + IDX_HEADS, :]
    w_t = w_t * (IDX_HEADS ** -0.5) * (IDX_DIM ** -0.5)

    lane = lax.broadcasted_iota(jnp.int32, (blk, LANES), 1)
    for h in range(IDX_HEADS):
        pair = iq_ref[:, (h // 2) * LANES:(h // 2 + 1) * LANES]
        keep = (lane < IDX_DIM) if h % 2 == 0 else (lane >= IDX_DIM)
        iqm_ref[h] = jnp.where(keep, pair, jnp.zeros_like(pair))

    def score_chunk(c, causal):
        r = pl.multiple_of(c * ch, ch)
        ikc = ikd_ref[pl.ds(r, ch), :]
        acc = None
        for h in range(IDX_HEADS):
            t = w_t[h:h + 1, :] * jnp.maximum(_nt_dot(ikc, iqm_ref[h]), 0.0)
            acc = t if acc is None else acc + t
        if causal:
            acc = jnp.where(r + crow_i <= q_pos, acc, -jnp.inf)
        bits = lax.bitcast_convert_type(acc, jnp.int32)
        kt_ref[pl.ds(r, ch), :] = bits ^ ((bits >> 31) & 0x7FFFFFFF)

    def score_body(c, carry):
        score_chunk(c, False)
        return carry
    lax.fori_loop(0, n_chunks - 1, score_body, 0)
    score_chunk(n_chunks - 1, True)

    def count(pred_fn):
        def body(ci, acc):
            r = pl.multiple_of(ci * ch, ch)
            keys = kt_ref[pl.ds(r, ch), :]
            hit = jnp.where(pred_fn(keys, r), 1, 0).astype(jnp.int32)
            return acc + jnp.sum(hit.reshape(ch // SUBLANES, SUBLANES, blk), axis=0)
        acc = lax.fori_loop(0, n_chunks, body, jnp.zeros((SUBLANES, blk), jnp.int32))
        return jnp.sum(acc, axis=0, keepdims=True)

    def plane_body(c, carry):
        for g in range(ch // PLANE_GROUP):
            base = c * ch + g * PLANE_GROUP
            a = [kt_ref[pl.ds(pl.multiple_of(base + SUBLANES * v, SUBLANES), SUBLANES), :]
                 for v in range(32)]
            j, m = 16, 0x0000FFFF
            while j:
                k0 = 0
                while k0 < 32:
                    t = (a[k0] ^ lax.shift_right_logical(a[k0 + j], jnp.int32(j))) & m
                    a[k0] = a[k0] ^ t
                    a[k0 + j] = a[k0 + j] ^ (t << j)
                    k0 = (k0 + j + 1) & ~j
                j >>= 1
                m = (m ^ (m << j)) & 0xFFFFFFFF if j else m
                m = m - (1 << 32) if m >= (1 << 31) else m
            a[0] = ~a[0]
            row = pl.multiple_of(c * (ch // 32) + g * SUBLANES, SUBLANES)
            for jj in range(32):
                planes_ref[jj, pl.ds(row, SUBLANES), :] = a[jj]
        return carry
    lax.fori_loop(0, n_chunks, plane_body, 0)

    n_rows = seq // 32

    def rowsum(x):
        part = jnp.sum(x.reshape(n_rows // SUBLANES, SUBLANES, blk), axis=0)
        return jnp.sum(part, axis=0, keepdims=True)

    prow = lax.broadcasted_iota(jnp.int32, (n_rows, blk), 0)
    alive0 = jnp.where(prow < n_chunks * (ch // 32), -1, 0).astype(jnp.int32)

    def bit_body(j, carry):
        alive, cnt_gt, ukey = carry
        w = planes_ref[j]
        ones = alive & w
        c1 = rowsum(lax.population_count(ones))
        take = cnt_gt + c1 >= k_sel
        alive = jnp.where(take, ones, alive & ~w)
        cnt_gt = jnp.where(take, cnt_gt, cnt_gt + c1)
        ukey = jnp.where(take, ukey | (jnp.int32(1) << (31 - j)), ukey)
        return alive, cnt_gt, ukey

    zero_row = jnp.zeros((1, blk), jnp.int32)
    alive, cnt_gt, ukey = lax.fori_loop(0, 32, bit_body, (alive0, zero_row, zero_row))
    ans = ukey ^ INT_MIN
    cnt_ge = cnt_gt + rowsum(lax.population_count(alive))
    need = k_sel - cnt_gt

    pm_ref[...] = jnp.full(pm_ref.shape, 2 * seq, jnp.int32)

    @pl.when(jnp.max(cnt_ge) > k_sel)
    def _():
        n_bits = (2 * seq - 1).bit_length()

        def pos_body(i, p):
            cand = p | (jnp.int32(1) << (n_bits - 1 - i))
            cnt = count(lambda keys, r: (keys == ans) & (r + crow_i < cand))
            return jnp.where(cnt <= need - 1, cand, p)
        p_max = lax.fori_loop(0, n_bits, pos_body, jnp.zeros((1, blk), jnp.int32))
        pm_ref[...] = jnp.broadcast_to(p_max, pm_ref.shape)

    p_max = pm_ref[0:1, :]

    def mask_body(c, carry):
        r = pl.multiple_of(c * ch, ch)
        keys = kt_ref[pl.ds(r, ch), :]
        k_pos = r + crow_i
        sel = (keys > ans) | ((keys == ans) & (k_pos <= p_max))
        sel = sel & (k_pos <= q_pos)
        mb_ref[pl.ds(r, ch), :] = jnp.where(sel, 0.0, MASK_NEG).astype(mb_ref.dtype)
        return carry
    lax.fori_loop(0, n_chunks, mask_body, 0)

    eye = jnp.where(row_i == col_i, 1.0, 0.0).astype(MXU_DTYPE)
    for cc in range(DSA_KV_HEADS):
        for g in range(DSA_GROUPS):
            h = cc * DSA_GROUPS + g
            qaug_ref[cc, g * blk:(g + 1) * blk, 0:DSA_HD] = dq_ref[:, h * DSA_HD:(h + 1) * DSA_HD]
            qaug_ref[cc, g * blk:(g + 1) * blk, DSA_HD:2 * DSA_HD] = eye
    acc_ref[...] = jnp.zeros_like(acc_ref)
    for cc in range(DSA_KV_HEADS):
        ml_ref[cc, 0] = jnp.full(ml_ref.shape[2:], -jnp.inf, jnp.float32)
        ml_ref[cc, 1] = jnp.zeros(ml_ref.shape[2:], jnp.float32)

    def stage_logits(c, lg_ref, mx_ref):
        r = pl.multiple_of(c * ch, ch)
        mbc = mb_ref[pl.ds(r, ch), :]
        for cc in range(DSA_KV_HEADS):
            kaug = jnp.concatenate([dk_ref[pl.ds(r, ch), cc * DSA_HD:(cc + 1) * DSA_HD], mbc], axis=1)
            bias = jnp.concatenate(
                [bt_ref[jnp.clip(qb - (c * BLOCKS_PER_CHUNK + j), 0, 2), cc]
                 for j in range(BLOCKS_PER_CHUNK)], axis=0)
            lg = _nt_dot(kaug, qaug_ref[cc]) + bias
            lg_ref[cc] = lg
            mx_ref[cc] = jnp.broadcast_to(jnp.max(lg, axis=0, keepdims=True), mx_ref.shape[1:])

    def stage_softmax(c, lg_ref, mx_ref):
        for cc in range(DSA_KV_HEADS):
            m = ml_ref[cc, 0, 0:1, :]
            l = ml_ref[cc, 1, 0:1, :]
            m_new = jnp.maximum(m, mx_ref[cc, 0:1, :])
            alpha = jnp.exp(m - m_new)
            p = jnp.exp(lg_ref[cc] - m_new)
            l_new = alpha * l + jnp.sum(p, axis=0, keepdims=True)
            pv = _dot(vt_ref[c, cc * DSA_HD:(cc + 1) * DSA_HD, :], p.astype(MXU_DTYPE))
            acc_ref[cc] = acc_ref[cc] * alpha + pv
            ml_ref[cc, 0] = jnp.broadcast_to(m_new, ml_ref.shape[2:])
            ml_ref[cc, 1] = jnp.broadcast_to(l_new, ml_ref.shape[2:])

    stage_logits(0, lga_ref, mxa_ref)

    def pair_body(pi, carry):
        c0 = 2 * pi
        stage_logits(c0 + 1, lgb_ref, mxb_ref)
        stage_softmax(c0, lga_ref, mxa_ref)

        stage_logits(jnp.minimum(c0 + 2, n_chunks - 1), lga_ref, mxa_ref)
        stage_softmax(c0 + 1, lgb_ref, mxb_ref)
        return carry
    lax.fori_loop(0, n_chunks // 2, pair_body, 0)

    @pl.when(n_chunks % 2 == 1)
    def _():
        stage_softmax(n_chunks - 1, lga_ref, mxa_ref)

    for cc in range(DSA_KV_HEADS):
        o_t = acc_ref[cc] / ml_ref[cc, 1, 0:1, :]
        for g in range(DSA_GROUPS):
            h = cc * DSA_GROUPS + g
            o = o_t[:, g * blk:(g + 1) * blk].T
            zz = dz_ref[:, h * DSA_HD:(h + 1) * DSA_HD].astype(jnp.float32)
            out_ref[:, h * DSA_HD:(h + 1) * DSA_HD] = (o * _silu(zz)).astype(out_ref.dtype)


def _dsa(u, ikd, small, rel_bias, batch, seq):
    nb = seq // Q_BLOCK
    hq = DSA_HEADS * DSA_HD
    hkv = DSA_KV_HEADS * DSA_HD
    hi = IDX_HEADS * IDX_DIM

    def qmap(cb):
        return lambda b, q: (b * nb + q, cb)

    def bmap(cb):
        return lambda b, q: (b, cb)

    return pl.pallas_call(
        functools.partial(_dsa_kernel, seq=seq),
        out_shape=jax.ShapeDtypeStruct((batch * seq, hq), ACT_DTYPE),
        grid=(batch, nb),
        in_specs=[
            pl.BlockSpec(memory_space=pltpu.SMEM),
            pl.BlockSpec((Q_BLOCK, hq), qmap(COL_DQ // hq)),
            pl.BlockSpec((seq, hkv), bmap(COL_DK // hkv)),
            pl.BlockSpec((seq, hkv), bmap(COL_DV // hkv)),
            pl.BlockSpec((Q_BLOCK, hi), qmap(COL_IQ // hi)),
            pl.BlockSpec((seq, LANES), bmap(0)),
            pl.BlockSpec((Q_BLOCK, LANES), qmap(0)),
            pl.BlockSpec((Q_BLOCK, hq), qmap(COL_DZ // hq)),
        ],
        out_specs=pl.BlockSpec((Q_BLOCK, hq), qmap(0)),
        scratch_shapes=[
            pltpu.VMEM((seq, Q_BLOCK), jnp.int32),
            pltpu.VMEM((32, seq // 32, Q_BLOCK), jnp.int32),
            pltpu.VMEM((seq, Q_BLOCK), MXU_DTYPE),
            pltpu.VMEM((seq // KEY_CHUNK, hkv, KEY_CHUNK), MXU_DTYPE),
            pltpu.VMEM((3, DSA_KV_HEADS, Q_BLOCK, DSA_GROUPS * Q_BLOCK), jnp.float32),
            pltpu.VMEM((IDX_HEADS, Q_BLOCK, LANES), MXU_DTYPE),
            pltpu.VMEM((DSA_KV_HEADS, DSA_GROUPS * Q_BLOCK, 2 * DSA_HD), MXU_DTYPE),
            pltpu.VMEM((DSA_KV_HEADS, DSA_HD, DSA_GROUPS * Q_BLOCK), jnp.float32),
            pltpu.VMEM((SUBLANES, Q_BLOCK), jnp.int32),
            pltpu.VMEM((DSA_KV_HEADS, KEY_CHUNK, DSA_GROUPS * Q_BLOCK), jnp.float32),
            pltpu.VMEM((DSA_KV_HEADS, KEY_CHUNK, DSA_GROUPS * Q_BLOCK), jnp.float32),
            pltpu.VMEM((DSA_KV_HEADS, SUBLANES, DSA_GROUPS * Q_BLOCK), jnp.float32),
            pltpu.VMEM((DSA_KV_HEADS, SUBLANES, DSA_GROUPS * Q_BLOCK), jnp.float32),
            pltpu.VMEM((DSA_KV_HEADS, 2, SUBLANES, DSA_GROUPS * Q_BLOCK), jnp.float32),
        ],
        compiler_params=pltpu.CompilerParams(
            dimension_semantics=("arbitrary", "arbitrary"),
            vmem_limit_bytes=VMEM_LIMIT),
        name="dsa",
    )(rel_bias, u, u, u, u, ikd, small, u)


def _memkv_kernel(mem_ref, g_ref, w_ref, o_ref):
    xf = mem_ref[...]
    y = xf * lax.rsqrt(jnp.mean(xf * xf, axis=-1, keepdims=True) + EPS)
    hb = (y * g_ref[...]).astype(MXU_DTYPE)
    o_ref[...] = _dot(hb, w_ref[...]).astype(o_ref.dtype)


def _memkv(mem2, g_mem, w_kv, batch):
    n = 2 * X_HEADS * X_HD
    return pl.pallas_call(
        _memkv_kernel,
        out_shape=jax.ShapeDtypeStruct((batch * N_MEM, n), ACT_DTYPE),
        grid=(batch,),
        in_specs=[
            pl.BlockSpec((N_MEM, D_MODEL), lambda b: (b, 0)),
            pl.BlockSpec((1, D_MODEL), lambda b: (0, 0)),
            pl.BlockSpec((D_MODEL, n), lambda b: (0, 0)),
        ],
        out_specs=pl.BlockSpec((N_MEM, n), lambda b: (b, 0)),
        compiler_params=pltpu.CompilerParams(
            dimension_semantics=("arbitrary",),
            vmem_limit_bytes=VMEM_LIMIT),
        name="memkv",
    )(mem2, g_mem, w_kv)


def _merge_kernel(x_ref, yg_ref, yd_ref, xq_ref, xz_ref, sg_ref, sd_ref, sm_ref, mkv_ref,
                  wg_ref, wd_ref, wx_ref, wo_ref, gp_ref, o_ref, ym_ref):
    hw = X_HEADS * X_HD
    for h in range(X_HEADS):
        cs = slice(h * X_HD, (h + 1) * X_HD)
        mk = mkv_ref[:, h * X_HD:(h + 1) * X_HD]
        mv = mkv_ref[:, hw + h * X_HD:hw + (h + 1) * X_HD]
        lg = _nt_dot(xq_ref[:, cs], mk)
        lg = lg - jnp.max(lg, axis=-1, keepdims=True)
        e = jnp.exp(lg)
        p = (e / jnp.sum(e, axis=-1, keepdims=True)).astype(MXU_DTYPE)
        o = _dot(p, mv)
        ym_ref[:, cs] = (o * _silu(xz_ref[:, cs].astype(jnp.float32))).astype(ym_ref.dtype)

    merged = jax.nn.sigmoid(sg_ref[...].astype(jnp.float32)) * _dot(yg_ref[...], wg_ref[...])
    merged = merged + jax.nn.sigmoid(sd_ref[...].astype(jnp.float32)) * _dot(yd_ref[...], wd_ref[...])
    merged = merged + jax.nn.sigmoid(sm_ref[...].astype(jnp.float32)) * _dot(ym_ref[...], wx_ref[...])
    t = _dot(merged.astype(MXU_DTYPE), wo_ref[...])
    y = t * lax.rsqrt(jnp.mean(t * t, axis=-1, keepdims=True) + EPS)
    o_ref[...] = x_ref[...] + y * gp_ref[...]


def _merge(x2, y_gla, y_dsa, u, mkv, w_g, w_d, w_x, w_o, g_post, seq, tm):
    m = x2.shape[0]
    steps_per_batch = seq // tm
    d = D_MODEL

    def rmap(cb):
        return lambda i: (i, cb)

    wspec = pl.BlockSpec((d, d), lambda i: (0, 0))
    return pl.pallas_call(
        _merge_kernel,
        out_shape=jax.ShapeDtypeStruct((m, d), jnp.float32),
        grid=(m // tm,),
        in_specs=[
            pl.BlockSpec((tm, d), rmap(0)),
            pl.BlockSpec((tm, d), rmap(0)),
            pl.BlockSpec((tm, d), rmap(0)),
            pl.BlockSpec((tm, d), rmap(COL_XQ // d)),
            pl.BlockSpec((tm, d), rmap(COL_XZ // d)),
            pl.BlockSpec((tm, d), rmap(COL_SG // d)),
            pl.BlockSpec((tm, d), rmap(COL_SD // d)),
            pl.BlockSpec((tm, d), rmap(COL_SM // d)),
            pl.BlockSpec((N_MEM, 2 * X_HEADS * X_HD), lambda i: (i // steps_per_batch, 0)),
            wspec, wspec, wspec, wspec,
            pl.BlockSpec((1, d), lambda i: (0, 0)),
        ],
        out_specs=pl.BlockSpec((tm, d), rmap(0)),
        scratch_shapes=[pltpu.VMEM((tm, d), MXU_DTYPE)],
        compiler_params=pltpu.CompilerParams(
            dimension_semantics=("arbitrary",),
            vmem_limit_bytes=VMEM_LIMIT),
        name="merge",
    )(x2, y_gla, y_dsa, u, u, u, u, u, mkv, w_g, w_d, w_x, w_o, g_post)


def _relayout_w_in(w_in):
    offs = np.cumsum(np.array(SPLIT_SIZES))[:-1].tolist()
    (gq, gk, gv, ga, gz, dq, dk, dv, iq, ik, iw, dz, xq, xz, gates) = jnp.split(w_in, offs, axis=1)
    main = jnp.concatenate([gq, gk, gv, gz, dq, dz, xq, xz, gates, dk, dv, iq], axis=1)
    pad = jnp.zeros((w_in.shape[0], LANES - GLA_RANK - IDX_HEADS), w_in.dtype)
    small = jnp.concatenate([ik, ik, ga, iw, pad], axis=1)
    return main.astype(MXU_DTYPE), small.astype(MXU_DTYPE)


def _col_scale():
    s = np.ones((1, U_COLS), np.float32)
    s[:, COL_GQ:COL_GQ + GLA_HEADS * GLA_DK] = GLA_DK ** -0.5
    s[:, COL_DQ:COL_DQ + DSA_HEADS * DSA_HD] = DSA_HD ** -0.5
    s[:, COL_XQ:COL_XQ + X_HEADS * X_HD] = X_HD ** -0.5
    return jnp.asarray(s)


def _layer(x2, mem2, g_pre, g_post, g_mem, w_in, w_up, b_a, g_gla, rel_bias, w_kv,
           w_g, w_d, w_x, w_o, batch, seq):
    w_main, w_small = _relayout_w_in(w_in)
    tm = min(1024, batch * seq)
    u, ikd, small = _proj(x2, g_pre.reshape(1, -1), w_main, w_small, _col_scale(), tm, 1024)
    y_gla = _gla(u, small, w_up.astype(MXU_DTYPE), b_a.reshape(1, -1), g_gla.reshape(1, -1),
                 batch, seq, chunks_per_step=4)
    y_dsa = _dsa(u, ikd, small, rel_bias, batch, seq)
    mkv = _memkv(mem2, g_mem.reshape(1, -1), w_kv.astype(MXU_DTYPE), batch)
    return _merge(x2, y_gla, y_dsa, u, mkv, w_g.astype(MXU_DTYPE), w_d.astype(MXU_DTYPE),
                  w_x.astype(MXU_DTYPE), w_o.astype(MXU_DTYPE), g_post.reshape(1, -1), seq, 256)


def kernel(x, mem, g_pre, g_post, g_mem, w_in, w_gla_a_up, b_gla_a, g_gla, rel_bias,
           w_mem_kv, w_gla_out, w_dsa_out, w_x_out, w_o):
    batch, seq, d = x.shape
    x2 = x.reshape(batch * seq, d)
    mem2 = mem.reshape(batch * N_MEM, d)
    for i in range(g_pre.shape[0]):
        x2 = _layer(x2, mem2, g_pre[i], g_post[i], g_mem[i], w_in[i], w_gla_a_up[i],
                    b_gla_a[i], g_gla[i], rel_bias, w_mem_kv[i], w_gla_out[i],
                    w_dsa_out[i], w_x_out[i], w_o[i], batch, seq)
    return x2.reshape(batch, seq, d)
```

```python
import functools
import math

import jax
import jax.numpy as jnp
import numpy as np
from jax import lax
from jax.experimental import pallas as pl
from jax.experimental.pallas import tpu as pltpu

D_MODEL = 1024
N_MEM = 256
EPS = 1e-6
GLA_HEADS = 4
GLA_DK = 128
GLA_DV = 256
GLA_RANK = 16
GLA_GATE_NORM = 16.0
GLA_CHUNK = 64
DSA_HEADS = 8
DSA_KV_HEADS = 2
DSA_GROUPS = DSA_HEADS // DSA_KV_HEADS
DSA_HD = 128
IDX_HEADS = 8
IDX_DIM = 64
TOPK_MAX = 256
Q_BLOCK = 128
REL_BUCKETS = 32
REL_MAX_DIST = 128
X_HEADS = 4
X_HD = 256

SPLIT_SIZES = (512, 512, 1024, 16, 1024, 1024, 256, 256, 512, 64, 8, 1024, 1024, 1024, 3072)

LANES = 128
SUBLANES = 8

MXU_DTYPE = jnp.bfloat16
ACT_DTYPE = jnp.bfloat16

U_COLS = 11264
COL_GQ, COL_GK, COL_GV, COL_GZ = 0, 512, 1024, 2048
COL_DQ, COL_DZ, COL_XQ, COL_XZ = 3072, 4096, 5120, 6144
COL_SG, COL_SD, COL_SM = 7168, 8192, 9216
COL_DK, COL_DV, COL_IQ = 10240, 10496, 10752
SMALL_GA, SMALL_IW = 0, 16

MASK_NEG = -1e30
LOG2_E = math.log2(math.e)
INT_MIN = -(2 ** 31)
VMEM_LIMIT = 56 * 1024 * 1024

PROJ_TM = 1024
PROJ_TN = U_COLS // 4
MERGE_TM = 512


def _nt_dot(a, b):
    return lax.dot_general(a, b, (((1,), (1,)), ((), ())),
                           preferred_element_type=jnp.float32)


def _tn_dot(a, b):
    return lax.dot_general(a, b, (((0,), (0,)), ((), ())),
                           preferred_element_type=jnp.float32)


def _dot(a, b):
    return jnp.dot(a, b, preferred_element_type=jnp.float32)


def _silu(z):
    return z * jax.nn.sigmoid(z)


def _proj_kernel(x_ref, g_ref, w_ref, ws_ref, cs_ref, u_ref, ikd_ref, sm_ref, h_ref):
    @pl.when(pl.program_id(1) == 0)
    def _():
        xf = x_ref[...]
        y = xf * lax.rsqrt(jnp.mean(xf * xf, axis=-1, keepdims=True) + EPS)
        hb = (y * g_ref[...]).astype(MXU_DTYPE)
        h_ref[...] = hb
        r = _dot(hb, ws_ref[...])
        ikd_ref[...] = r[:, :LANES].astype(ikd_ref.dtype)
        sm_ref[...] = r[:, LANES:]

    acc = _dot(h_ref[...], w_ref[...])
    u_ref[...] = (acc * cs_ref[...]).astype(u_ref.dtype)


def _proj(x2, g_pre, w_main, w_small, col_scale, tm, tn):
    m = x2.shape[0]
    grid = (m // tm, U_COLS // tn)
    return pl.pallas_call(
        _proj_kernel,
        out_shape=(jax.ShapeDtypeStruct((m, U_COLS), ACT_DTYPE),
                   jax.ShapeDtypeStruct((m, LANES), ACT_DTYPE),
                   jax.ShapeDtypeStruct((m, LANES), jnp.float32)),
        grid=grid,
        in_specs=[
            pl.BlockSpec((tm, D_MODEL), lambda i, j: (i, 0)),
            pl.BlockSpec((1, D_MODEL), lambda i, j: (0, 0)),
            pl.BlockSpec((D_MODEL, tn), lambda i, j: (0, j)),
            pl.BlockSpec((D_MODEL, 2 * LANES), lambda i, j: (0, 0)),
            pl.BlockSpec((1, tn), lambda i, j: (0, j)),
        ],
        out_specs=(
            pl.BlockSpec((tm, tn), lambda i, j: (i, j)),
            pl.BlockSpec((tm, LANES), lambda i, j: (i, 0)),
            pl.BlockSpec((tm, LANES), lambda i, j: (i, 0)),
        ),
        scratch_shapes=[pltpu.VMEM((tm, D_MODEL), MXU_DTYPE)],
        compiler_params=pltpu.CompilerParams(
            dimension_semantics=("arbitrary", "arbitrary"),
            vmem_limit_bytes=VMEM_LIMIT),
        name="proj",
    )(x2, g_pre, w_main, w_small, col_scale)


def _split3(x):
    hi = x.astype(MXU_DTYPE)
    r1 = x - hi.astype(jnp.float32)
    mid = r1.astype(MXU_DTYPE)
    lo = (r1 - mid.astype(jnp.float32)).astype(MXU_DTYPE)
    return hi, mid, lo


def _gla_kernel(q_ref, k_ref, v_ref, z_ref, sm_ref, wup_ref, ba_ref, gg_ref,
                o_ref, st_ref, qd_ref, kd_ref, kt_ref, *, chunks_per_step):
    c = GLA_CHUNK
    rows = chunks_per_step * c
    heads = range(GLA_HEADS)

    @pl.when(pl.program_id(1) == 0)
    def _():
        st_ref[...] = jnp.zeros_like(st_ref)

    def ks(h):
        return slice(h * GLA_DK, (h + 1) * GLA_DK)

    def vs(h):
        return slice(h * GLA_DV, (h + 1) * GLA_DV)

    def chunk(i):
        return slice(i * c, (i + 1) * c)

    row = lax.broadcasted_iota(jnp.int32, (rows, rows), 0)
    col = lax.broadcasted_iota(jnp.int32, (rows, rows), 1)
    causal = (row // c == col // c) & (col <= row)
    tril = jnp.where(causal, 1.0, 0.0).astype(MXU_DTYPE)

    ga = sm_ref[:, SMALL_GA:SMALL_GA + GLA_RANK].astype(MXU_DTYPE)
    pre = _dot(ga, wup_ref[...]) + ba_ref[...]
    log_a = (jnp.minimum(pre, 0.0) - jnp.log1p(jnp.exp(-jnp.abs(pre)))) / GLA_GATE_NORM
    hi, mid, lo = _split3(log_a)
    b = _dot(tril, hi) + _dot(tril, mid) + _dot(tril, lo)
    bl = jnp.concatenate(
        [jnp.broadcast_to(b[i * c + c - 1:i * c + c, :], (c, b.shape[1]))
         for i in range(chunks_per_step)], axis=0)
    q = q_ref[...].astype(jnp.float32)
    k = k_ref[...].astype(jnp.float32)
    qd_ref[...] = (q * jnp.exp(b)).astype(MXU_DTYPE)
    kd_ref[...] = (k * jnp.exp(-b)).astype(MXU_DTYPE)
    kt_ref[...] = (k * jnp.exp(bl - b)).astype(MXU_DTYPE)
    decay = [jnp.exp(b[i * c + c - 1:i * c + c, :]) for i in range(chunks_per_step)]

    att = [_nt_dot(qd_ref[:, ks(h)], kd_ref[:, ks(h)]) for h in heads]
    att = [jnp.where(causal, a, 0.0).astype(MXU_DTYPE) for a in att]
    o = [_dot(att[h], v_ref[:, vs(h)]) for h in heads]
    kv = [[_tn_dot(v_ref[chunk(i), vs(h)], kt_ref[chunk(i), ks(h)]) for h in heads]
          for i in range(chunks_per_step)]

    st = [st_ref[h] for h in heads]
    o_inter = [[] for _ in heads]
    for i in range(chunks_per_step):
        for h in heads:
            o_inter[h].append(_nt_dot(qd_ref[chunk(i), ks(h)], st[h].astype(MXU_DTYPE)))
            st[h] = st[h] * decay[i][:, ks(h)] + kv[i][h]
    for h in heads:
        st_ref[h] = st[h]
        oh = o[h] + jnp.concatenate(o_inter[h], axis=0)
        y = oh * lax.rsqrt(jnp.mean(oh * oh, axis=-1, keepdims=True) + EPS)
        y = y * gg_ref[...]
        zz = z_ref[:, vs(h)].astype(jnp.float32)
        o_ref[:, vs(h)] = (y * _silu(zz)).astype(o_ref.dtype)


def _gla(u, small, w_up, b_a, g_gla, batch, seq, chunks_per_step):
    rows = chunks_per_step * GLA_CHUNK
    steps = seq // rows
    hk = GLA_HEADS * GLA_DK
    hv = GLA_HEADS * GLA_DV

    def rmap(cb):
        return lambda b, s: (b * steps + s, cb)

    return pl.pallas_call(
        functools.partial(_gla_kernel, chunks_per_step=chunks_per_step),
        out_shape=jax.ShapeDtypeStruct((batch * seq, hv), ACT_DTYPE),
        grid=(batch, steps),
        in_specs=[
            pl.BlockSpec((rows, hk), rmap(COL_GQ // hk)),
            pl.BlockSpec((rows, hk), rmap(COL_GK // hk)),
            pl.BlockSpec((rows, hv), rmap(COL_GV // hv)),
            pl.BlockSpec((rows, hv), rmap(COL_GZ // hv)),
            pl.BlockSpec((rows, LANES), rmap(0)),
            pl.BlockSpec((GLA_RANK, hk), lambda b, s: (0, 0)),
            pl.BlockSpec((1, hk), lambda b, s: (0, 0)),
            pl.BlockSpec((1, GLA_DV), lambda b, s: (0, 0)),
        ],
        out_specs=pl.BlockSpec((rows, hv), rmap(0)),
        scratch_shapes=[pltpu.VMEM((GLA_HEADS, GLA_DV, GLA_DK), jnp.float32),
                        pltpu.VMEM((rows, hk), MXU_DTYPE),
                        pltpu.VMEM((rows, hk), MXU_DTYPE),
                        pltpu.VMEM((rows, hk), MXU_DTYPE)],
        compiler_params=pltpu.CompilerParams(
            dimension_semantics=("arbitrary", "arbitrary"),
            vmem_limit_bytes=VMEM_LIMIT),
        name="gla",
    )(u, u, u, u, small, w_up, b_a, g_gla)


KEY_CHUNK = 4 * Q_BLOCK
BLOCKS_PER_CHUNK = KEY_CHUNK // Q_BLOCK
PLANE_GROUP = 32 * SUBLANES


def _t5_bucket(dist):
    max_exact = REL_BUCKETS // 2
    d = jnp.maximum(dist, 1).astype(jnp.float32)
    large = max_exact + (jnp.log(d / max_exact) / math.log(REL_MAX_DIST / max_exact)
                         * (REL_BUCKETS - max_exact)).astype(jnp.int32)
    large = jnp.minimum(large, REL_BUCKETS - 1)
    return jnp.where(dist < max_exact, dist, large)


def _dsa_kernel(rb_ref, dq_ref, dk_ref, dv_ref, iq_ref, ikd_ref, sm_ref, dz_ref,
                out_ref,
                kt_ref, planes_ref, mb_ref, vt_ref, bt_ref, iqm_ref, qaug_ref, acc_ref, pm_ref,
                lga_ref, lgb_ref, mxa_ref, mxb_ref, ml_ref,
                *, seq):
    qb = pl.program_id(1)
    blk = Q_BLOCK
    ch = KEY_CHUNK
    k_sel = min(TOPK_MAX, seq // 4)
    n_chunks = qb // BLOCKS_PER_CHUNK + 1
    width = DSA_GROUPS * blk

    row_i = lax.broadcasted_iota(jnp.int32, (blk, blk), 0)
    col_i = lax.broadcasted_iota(jnp.int32, (blk, blk), 1)
    crow_i = lax.broadcasted_iota(jnp.int32, (ch, blk), 0)
    q_pos = qb * blk + lax.broadcasted_iota(jnp.int32, (ch, blk), 1)

    @pl.when((pl.program_id(0) == 0) & (qb == 0))
    def _():
        planes_ref[...] = jnp.zeros_like(planes_ref)
        for delta in range(3):
            dist = jnp.maximum(delta * blk + col_i - row_i, 0)
            bucket = _t5_bucket(dist)
            for h in range(DSA_HEADS):
                tile = jnp.zeros((blk, blk), jnp.float32)
                for bk in range(REL_BUCKETS):
                    tile = jnp.where(bucket == bk, rb_ref[bk, h] * LOG2_E, tile)
                cc, g = divmod(h, DSA_GROUPS)
                bt_ref[delta, cc, :, g * blk:(g + 1) * blk] = tile

    @pl.when(qb == 0)
    def _():
        def body(i, carry):
            r = pl.multiple_of(i * ch, ch)
            vt_ref[i] = dv_ref[pl.ds(r, ch), :].astype(jnp.float32).T.astype(vt_ref.dtype)
            return carry
        lax.fori_loop(0, seq // ch, body, 0)

    w_t = sm_ref[...].T[SMALL_IW:SMALL_IW + IDX_HEADS, :]
    w_t = w_t * (IDX_HEADS ** -0.5) * (IDX_DIM ** -0.5)

    lane = lax.broadcasted_iota(jnp.int32, (blk, LANES), 1)
    for h in range(IDX_HEADS):
        pair = iq_ref[:, (h // 2) * LANES:(h // 2 + 1) * LANES]
        keep = (lane < IDX_DIM) if h % 2 == 0 else (lane >= IDX_DIM)
        iqm_ref[h // 2, (h % 2) * blk:(h % 2 + 1) * blk, :] = jnp.where(keep, pair, jnp.zeros_like(pair))

    def score_chunk(c, causal):
        r = pl.multiple_of(c * ch, ch)
        ikc = ikd_ref[pl.ds(r, ch), :]
        acc = None
        for hp in range(IDX_HEADS // 2):
            s2 = _nt_dot(ikc, iqm_ref[hp])
            for h in (2 * hp, 2 * hp + 1):
                t = w_t[h:h + 1, :] * jnp.maximum(s2[:, (h % 2) * blk:(h % 2 + 1) * blk], 0.0)
                acc = t if acc is None else acc + t
        if causal:
            acc = jnp.where(r + crow_i <= q_pos, acc, -jnp.inf)
        bits = lax.bitcast_convert_type(acc, jnp.int32)
        kt_ref[pl.ds(r, ch), :] = bits ^ ((bits >> 31) & 0x7FFFFFFF)

    def score_body(c, carry):
        score_chunk(c, False)
        return carry
    lax.fori_loop(0, n_chunks - 1, score_body, 0)
    score_chunk(n_chunks - 1, True)

    def count(pred_fn):
        def body(ci, acc):
            r = pl.multiple_of(ci * ch, ch)
            keys = kt_ref[pl.ds(r, ch), :]
            hit = jnp.where(pred_fn(keys, r), 1, 0).astype(jnp.int32)
            return acc + jnp.sum(hit.reshape(ch // SUBLANES, SUBLANES, blk), axis=0)
        acc = lax.fori_loop(0, n_chunks, body, jnp.zeros((SUBLANES, blk), jnp.int32))
        return jnp.sum(acc, axis=0, keepdims=True)

    def plane_body(c, carry):
        for g in range(ch // PLANE_GROUP):
            base = c * ch + g * PLANE_GROUP
            a = [kt_ref[pl.ds(pl.multiple_of(base + SUBLANES * v, SUBLANES), SUBLANES), :]
                 for v in range(32)]
            j, m = 16, 0x0000FFFF
            while j:
                k0 = 0
                while k0 < 32:
                    t = (a[k0] ^ lax.shift_right_logical(a[k0 + j], jnp.int32(j))) & m
                    a[k0] = a[k0] ^ t
                    a[k0 + j] = a[k0 + j] ^ (t << j)
                    k0 = (k0 + j + 1) & ~j
                j >>= 1
                m = (m ^ (m << j)) & 0xFFFFFFFF if j else m
                m = m - (1 << 32) if m >= (1 << 31) else m
            a[0] = ~a[0]
            row = pl.multiple_of(c * (ch // 32) + g * SUBLANES, SUBLANES)
            for jj in range(32):
                planes_ref[jj, pl.ds(row, SUBLANES), :] = a[jj]
        return carry
    lax.fori_loop(0, n_chunks, plane_body, 0)

    n_rows = seq // 32

    def rowsum(x):
        part = jnp.sum(x.reshape(n_rows // SUBLANES, SUBLANES, blk), axis=0)
        return jnp.sum(part, axis=0, keepdims=True)

    prow = lax.broadcasted_iota(jnp.int32, (n_rows, blk), 0)
    alive0 = jnp.where(prow < n_chunks * (ch // 32), -1, 0).astype(jnp.int32)

    def bit_body(j, carry):
        alive, cnt_gt, ukey = carry
        w = planes_ref[j]
        ones = alive & w
        c1 = rowsum(lax.population_count(ones))
        take = cnt_gt + c1 >= k_sel
        alive = jnp.where(take, ones, alive & ~w)
        cnt_gt = jnp.where(take, cnt_gt, cnt_gt + c1)
        ukey = jnp.where(take, ukey | (jnp.int32(1) << (31 - j)), ukey)
        return alive, cnt_gt, ukey

    zero_row = jnp.zeros((1, blk), jnp.int32)
    alive, cnt_gt, ukey = lax.fori_loop(0, 32, bit_body, (alive0, zero_row, zero_row))
    ans = ukey ^ INT_MIN
    cnt_ge = cnt_gt + rowsum(lax.population_count(alive))
    need = k_sel - cnt_gt

    pm_ref[...] = jnp.full(pm_ref.shape, 2 * seq, jnp.int32)

    @pl.when(jnp.max(cnt_ge) > k_sel)
    def _():
        n_bits = (2 * seq - 1).bit_length()

        def pos_body(i, p):
            cand = p | (jnp.int32(1) << (n_bits - 1 - i))
            cnt = count(lambda keys, r: (keys == ans) & (r + crow_i < cand))
            return jnp.where(cnt <= need - 1, cand, p)
        p_max = lax.fori_loop(0, n_bits, pos_body, jnp.zeros((1, blk), jnp.int32))
        pm_ref[...] = jnp.broadcast_to(p_max, pm_ref.shape)

    p_max = pm_ref[0:1, :]

    def mask_body(c, carry):
        r = pl.multiple_of(c * ch, ch)
        keys = kt_ref[pl.ds(r, ch), :]
        k_pos = r + crow_i
        sel = (keys > ans) | ((keys == ans) & (k_pos <= p_max))
        sel = sel & (k_pos <= q_pos)
        mb_ref[pl.ds(r, ch), :] = jnp.where(sel, 0.0, MASK_NEG).astype(mb_ref.dtype)
        return carry
    lax.fori_loop(0, n_chunks, mask_body, 0)

    eye = jnp.where(row_i == col_i, 1.0, 0.0).astype(MXU_DTYPE)
    for cc in range(DSA_KV_HEADS):
        for g in range(DSA_GROUPS):
            h = cc * DSA_GROUPS + g
            qaug_ref[cc, g * blk:(g + 1) * blk, 0:DSA_HD] = dq_ref[:, h * DSA_HD:(h + 1) * DSA_HD]
            qaug_ref[cc, g * blk:(g + 1) * blk, DSA_HD:2 * DSA_HD] = eye
    acc_ref[...] = jnp.zeros_like(acc_ref)
    for cc in range(DSA_KV_HEADS):
        ml_ref[cc, 0] = jnp.full(ml_ref.shape[2:], -jnp.inf, jnp.float32)
        ml_ref[cc, 1] = jnp.zeros(ml_ref.shape[2:], jnp.float32)

    def stage_logits(c, lg_ref, mx_ref):
        r = pl.multiple_of(c * ch, ch)
        mbc = mb_ref[pl.ds(r, ch), :]
        for cc in range(DSA_KV_HEADS):
            kaug = jnp.concatenate([dk_ref[pl.ds(r, ch), cc * DSA_HD:(cc + 1) * DSA_HD], mbc], axis=1)
            bias = jnp.concatenate(
                [bt_ref[jnp.clip(qb - (c * BLOCKS_PER_CHUNK + j), 0, 2), cc]
                 for j in range(BLOCKS_PER_CHUNK)], axis=0)
            lg = _nt_dot(kaug, qaug_ref[cc]) + bias
            lg_ref[cc] = lg
            mx_ref[cc] = jnp.broadcast_to(jnp.max(lg, axis=0, keepdims=True), mx_ref.shape[1:])

    def stage_softmax(c, lg_ref, mx_ref):
        for cc in range(DSA_KV_HEADS):
            m = ml_ref[cc, 0, 0:1, :]
            l = ml_ref[cc, 1, 0:1, :]
            m_new = jnp.maximum(m, mx_ref[cc, 0:1, :])
            alpha = jnp.exp2(m - m_new)
            p = jnp.exp2(lg_ref[cc] - m_new)
            l_new = alpha * l + jnp.sum(p, axis=0, keepdims=True)
            pv = _dot(vt_ref[c, cc * DSA_HD:(cc + 1) * DSA_HD, :], p.astype(MXU_DTYPE))
            acc_ref[cc] = acc_ref[cc] * alpha + pv
            ml_ref[cc, 0] = jnp.broadcast_to(m_new, ml_ref.shape[2:])
            ml_ref[cc, 1] = jnp.broadcast_to(l_new, ml_ref.shape[2:])

    stage_logits(0, lga_ref, mxa_ref)

    def pair_body(pi, carry):
        c0 = 2 * pi
        stage_logits(c0 + 1, lgb_ref, mxb_ref)
        stage_softmax(c0, lga_ref, mxa_ref)

        stage_logits(jnp.minimum(c0 + 2, n_chunks - 1), lga_ref, mxa_ref)
        stage_softmax(c0 + 1, lgb_ref, mxb_ref)
        return carry
    lax.fori_loop(0, n_chunks // 2, pair_body, 0)

    @pl.when(n_chunks % 2 == 1)
    def _():
        stage_softmax(n_chunks - 1, lga_ref, mxa_ref)

    for cc in range(DSA_KV_HEADS):
        o_t = acc_ref[cc] / ml_ref[cc, 1, 0:1, :]
        for g in range(DSA_GROUPS):
            h = cc * DSA_GROUPS + g
            o = o_t[:, g * blk:(g + 1) * blk].T
            zz = dz_ref[:, h * DSA_HD:(h + 1) * DSA_HD].astype(jnp.float32)
            out_ref[:, h * DSA_HD:(h + 1) * DSA_HD] = (o * _silu(zz)).astype(out_ref.dtype)


def _dsa(u, ikd, small, rel_bias, batch, seq):
    nb = seq // Q_BLOCK
    hq = DSA_HEADS * DSA_HD
    hkv = DSA_KV_HEADS * DSA_HD
    hi = IDX_HEADS * IDX_DIM

    def qmap(cb):
        return lambda b, q: (b * nb + q, cb)

    def bmap(cb):
        return lambda b, q: (b, cb)

    return pl.pallas_call(
        functools.partial(_dsa_kernel, seq=seq),
        out_shape=jax.ShapeDtypeStruct((batch * seq, hq), ACT_DTYPE),
        grid=(batch, nb),
        in_specs=[
            pl.BlockSpec(memory_space=pltpu.SMEM),
            pl.BlockSpec((Q_BLOCK, hq), qmap(COL_DQ // hq)),
            pl.BlockSpec((seq, hkv), bmap(COL_DK // hkv)),
            pl.BlockSpec((seq, hkv), bmap(COL_DV // hkv)),
            pl.BlockSpec((Q_BLOCK, hi), qmap(COL_IQ // hi)),
            pl.BlockSpec((seq, LANES), bmap(0)),
            pl.BlockSpec((Q_BLOCK, LANES), qmap(0)),
            pl.BlockSpec((Q_BLOCK, hq), qmap(COL_DZ // hq)),
        ],
        out_specs=pl.BlockSpec((Q_BLOCK, hq), qmap(0)),
        scratch_shapes=[
            pltpu.VMEM((seq, Q_BLOCK), jnp.int32),
            pltpu.VMEM((32, seq // 32, Q_BLOCK), jnp.int32),
            pltpu.VMEM((seq, Q_BLOCK), MXU_DTYPE),
            pltpu.VMEM((seq // KEY_CHUNK, hkv, KEY_CHUNK), MXU_DTYPE),
            pltpu.VMEM((3, DSA_KV_HEADS, Q_BLOCK, DSA_GROUPS * Q_BLOCK), jnp.float32),
            pltpu.VMEM((IDX_HEADS // 2, 2 * Q_BLOCK, LANES), MXU_DTYPE),
            pltpu.VMEM((DSA_KV_HEADS, DSA_GROUPS * Q_BLOCK, 2 * DSA_HD), MXU_DTYPE),
            pltpu.VMEM((DSA_KV_HEADS, DSA_HD, DSA_GROUPS * Q_BLOCK), jnp.float32),
            pltpu.VMEM((SUBLANES, Q_BLOCK), jnp.int32),
            pltpu.VMEM((DSA_KV_HEADS, KEY_CHUNK, DSA_GROUPS * Q_BLOCK), jnp.float32),
            pltpu.VMEM((DSA_KV_HEADS, KEY_CHUNK, DSA_GROUPS * Q_BLOCK), jnp.float32),
            pltpu.VMEM((DSA_KV_HEADS, SUBLANES, DSA_GROUPS * Q_BLOCK), jnp.float32),
            pltpu.VMEM((DSA_KV_HEADS, SUBLANES, DSA_GROUPS * Q_BLOCK), jnp.float32),
            pltpu.VMEM((DSA_KV_HEADS, 2, SUBLANES, DSA_GROUPS * Q_BLOCK), jnp.float32),
        ],
        compiler_params=pltpu.CompilerParams(
            dimension_semantics=("arbitrary", "arbitrary"),
            vmem_limit_bytes=VMEM_LIMIT),
        name="dsa",
    )(rel_bias, u, u, u, u, ikd, small, u)


def _memkv_kernel(mem_ref, g_ref, w_ref, o_ref):
    xf = mem_ref[...]
    y = xf * lax.rsqrt(jnp.mean(xf * xf, axis=-1, keepdims=True) + EPS)
    hb = (y * g_ref[...]).astype(MXU_DTYPE)
    o_ref[...] = _dot(hb, w_ref[...]).astype(o_ref.dtype)


def _memkv(mem2, g_mem, w_kv, batch):
    n = 2 * X_HEADS * X_HD
    return pl.pallas_call(
        _memkv_kernel,
        out_shape=jax.ShapeDtypeStruct((batch * N_MEM, n), ACT_DTYPE),
        grid=(batch,),
        in_specs=[
            pl.BlockSpec((N_MEM, D_MODEL), lambda b: (b, 0)),
            pl.BlockSpec((1, D_MODEL), lambda b: (0, 0)),
            pl.BlockSpec((D_MODEL, n), lambda b: (0, 0)),
        ],
        out_specs=pl.BlockSpec((N_MEM, n), lambda b: (b, 0)),
        compiler_params=pltpu.CompilerParams(
            dimension_semantics=("arbitrary",),
            vmem_limit_bytes=VMEM_LIMIT),
        name="memkv",
    )(mem2, g_mem, w_kv)


def _merge_kernel(x_ref, yg_ref, yd_ref, xq_ref, xz_ref, sg_ref, sd_ref, sm_ref, mkv_ref,
                  wg_ref, wd_ref, wx_ref, wo_ref, gp_ref, o_ref, ym_ref):
    hw = X_HEADS * X_HD
    for h in range(X_HEADS):
        cs = slice(h * X_HD, (h + 1) * X_HD)
        mk = mkv_ref[:, h * X_HD:(h + 1) * X_HD]
        mv = mkv_ref[:, hw + h * X_HD:hw + (h + 1) * X_HD]
        lg = _nt_dot(xq_ref[:, cs], mk)
        lg = lg - jnp.max(lg, axis=-1, keepdims=True)
        e = jnp.exp(lg)
        p = (e / jnp.sum(e, axis=-1, keepdims=True)).astype(MXU_DTYPE)
        o = _dot(p, mv)
        ym_ref[:, cs] = (o * _silu(xz_ref[:, cs].astype(jnp.float32))).astype(ym_ref.dtype)

    merged = jax.nn.sigmoid(sg_ref[...].astype(jnp.float32)) * _dot(yg_ref[...], wg_ref[...])
    merged = merged + jax.nn.sigmoid(sd_ref[...].astype(jnp.float32)) * _dot(yd_ref[...], wd_ref[...])
    merged = merged + jax.nn.sigmoid(sm_ref[...].astype(jnp.float32)) * _dot(ym_ref[...], wx_ref[...])
    t = _dot(merged.astype(MXU_DTYPE), wo_ref[...])
    y = t * lax.rsqrt(jnp.mean(t * t, axis=-1, keepdims=True) + EPS)
    o_ref[...] = x_ref[...] + y * gp_ref[...]


def _merge(x2, y_gla, y_dsa, u, mkv, w_g, w_d, w_x, w_o, g_post, seq, tm):
    m = x2.shape[0]
    steps_per_batch = seq // tm
    d = D_MODEL

    def rmap(cb):
        return lambda i: (i, cb)

    wspec = pl.BlockSpec((d, d), lambda i: (0, 0))
    return pl.pallas_call(
        _merge_kernel,
        out_shape=jax.ShapeDtypeStruct((m, d), jnp.float32),
        grid=(m // tm,),
        in_specs=[
            pl.BlockSpec((tm, d), rmap(0)),
            pl.BlockSpec((tm, d), rmap(0)),
            pl.BlockSpec((tm, d), rmap(0)),
            pl.BlockSpec((tm, d), rmap(COL_XQ // d)),
            pl.BlockSpec((tm, d), rmap(COL_XZ // d)),
            pl.BlockSpec((tm, d), rmap(COL_SG // d)),
            pl.BlockSpec((tm, d), rmap(COL_SD // d)),
            pl.BlockSpec((tm, d), rmap(COL_SM // d)),
            pl.BlockSpec((N_MEM, 2 * X_HEADS * X_HD), lambda i: (i // steps_per_batch, 0)),
            wspec, wspec, wspec, wspec,
            pl.BlockSpec((1, d), lambda i: (0, 0)),
        ],
        out_specs=pl.BlockSpec((tm, d), rmap(0)),
        scratch_shapes=[pltpu.VMEM((tm, d), MXU_DTYPE)],
        compiler_params=pltpu.CompilerParams(
            dimension_semantics=("arbitrary",),
            vmem_limit_bytes=VMEM_LIMIT),
        name="merge",
    )(x2, y_gla, y_dsa, u, u, u, u, u, mkv, w_g, w_d, w_x, w_o, g_post)


def _relayout_w_in(w_in):
    offs = np.cumsum(np.array(SPLIT_SIZES))[:-1].tolist()
    w = w_in.astype(MXU_DTYPE)
    (gq, gk, gv, ga, gz, dq, dk, dv, iq, ik, iw, dz, xq, xz, gates) = jnp.split(w, offs, axis=1)
    main = jnp.concatenate([gq, gk, gv, gz, dq, dz, xq, xz, gates, dk, dv, iq], axis=1)
    pad = jnp.zeros((w.shape[0], LANES - GLA_RANK - IDX_HEADS), w.dtype)
    small = jnp.concatenate([ik, ik, ga, iw, pad], axis=1)
    return main, small


def _col_scale():
    s = np.ones((1, U_COLS), np.float32)
    s[:, COL_GQ:COL_GQ + GLA_HEADS * GLA_DK] = GLA_DK ** -0.5
    s[:, COL_DQ:COL_DQ + DSA_HEADS * DSA_HD] = DSA_HD ** -0.5 * LOG2_E
    s[:, COL_XQ:COL_XQ + X_HEADS * X_HD] = X_HD ** -0.5
    return jnp.asarray(s)


def _layer(x2, mem2, g_pre, g_post, g_mem, w_in, w_up, b_a, g_gla, rel_bias, w_kv,
           w_g, w_d, w_x, w_o, batch, seq):
    w_main, w_small = _relayout_w_in(w_in)
    tm = min(PROJ_TM, batch * seq)
    u, ikd, small = _proj(x2, g_pre.reshape(1, -1), w_main, w_small, _col_scale(), tm, PROJ_TN)
    y_gla = _gla(u, small, w_up.astype(MXU_DTYPE), b_a.reshape(1, -1), g_gla.reshape(1, -1),
                 batch, seq, chunks_per_step=4)
    y_dsa = _dsa(u, ikd, small, rel_bias, batch, seq)
    mkv = _memkv(mem2, g_mem.reshape(1, -1), w_kv.astype(MXU_DTYPE), batch)
    return _merge(x2, y_gla, y_dsa, u, mkv, w_g.astype(MXU_DTYPE), w_d.astype(MXU_DTYPE),
                  w_x.astype(MXU_DTYPE), w_o.astype(MXU_DTYPE), g_post.reshape(1, -1), seq, MERGE_TM)


def kernel(x, mem, g_pre, g_post, g_mem, w_in, w_gla_a_up, b_gla_a, g_gla, rel_bias,
           w_mem_kv, w_gla_out, w_dsa_out, w_x_out, w_o):
    batch, seq, d = x.shape
    x2 = x.reshape(batch * seq, d)
    mem2 = mem.reshape(batch * N_MEM, d)
    for i in range(g_pre.shape[0]):
        x2 = _layer(x2, mem2, g_pre[i], g_post[i], g_mem[i], w_in[i], w_gla_a_up[i],
                    b_gla_a[i], g_gla[i], rel_bias, w_mem_kv[i], w_gla_out[i],
                    w_dsa_out[i], w_x_out[i], w_o[i], batch, seq)
    return x2.reshape(batch, seq, d)
```

```python
import functools
import math

import jax
import jax.numpy as jnp
import numpy as np
from jax import lax
from jax.experimental import pallas as pl
from jax.experimental.pallas import tpu as pltpu

D_MODEL = 1024
N_MEM = 256
EPS = 1e-6
GLA_HEADS = 4
GLA_DK = 128
GLA_DV = 256
GLA_RANK = 16
GLA_GATE_NORM = 16.0
GLA_CHUNK = 64
DSA_HEADS = 8
DSA_KV_HEADS = 2
DSA_GROUPS = DSA_HEADS // DSA_KV_HEADS
DSA_HD = 128
IDX_HEADS = 8
IDX_DIM = 64
TOPK_MAX = 256
Q_BLOCK = 128
REL_BUCKETS = 32
REL_MAX_DIST = 128
X_HEADS = 4
X_HD = 256

SPLIT_SIZES = (512, 512, 1024, 16, 1024, 1024, 256, 256, 512, 64, 8, 1024, 1024, 1024, 3072)

LANES = 128
SUBLANES = 8

MXU_DTYPE = jnp.bfloat16
ACT_DTYPE = jnp.bfloat16

U_COLS = 11264
COL_GQ, COL_GK, COL_GV, COL_GZ = 0, 512, 1024, 2048
COL_DQ, COL_DZ, COL_XQ, COL_XZ = 3072, 4096, 5120, 6144
COL_SG, COL_SD, COL_SM = 7168, 8192, 9216
COL_DK, COL_DV, COL_IQ = 10240, 10496, 10752
SMALL_GA, SMALL_IW = 0, 16

MASK_NEG = -1e30
LOG2_E = math.log2(math.e)
INT_MIN = -(2 ** 31)
VMEM_LIMIT = 56 * 1024 * 1024

PROJ_TM = 1024
PROJ_TN = U_COLS // 4
MERGE_TM = 512


def _nt_dot(a, b):
    return lax.dot_general(a, b, (((1,), (1,)), ((), ())),
                           preferred_element_type=jnp.float32)


def _tn_dot(a, b):
    return lax.dot_general(a, b, (((0,), (0,)), ((), ())),
                           preferred_element_type=jnp.float32)


def _dot(a, b):
    return jnp.dot(a, b, preferred_element_type=jnp.float32)


def _silu(z):
    return z * jax.nn.sigmoid(z)


def _proj_kernel(x_ref, g_ref, w_ref, ws_ref, cs_ref, u_ref, ikd_ref, sm_ref, h_ref):
    @pl.when(pl.program_id(1) == 0)
    def _():
        xf = x_ref[...]
        y = xf * lax.rsqrt(jnp.mean(xf * xf, axis=-1, keepdims=True) + EPS)
        hb = (y * g_ref[...]).astype(MXU_DTYPE)
        h_ref[...] = hb
        r = _nt_dot(hb, ws_ref[...])
        ikd_ref[...] = r[:, :LANES].astype(ikd_ref.dtype)
        sm_ref[...] = r[:, LANES:]

    acc = _nt_dot(h_ref[...], w_ref[...])
    u_ref[...] = (acc * cs_ref[...]).astype(u_ref.dtype)


def _proj(x2, g_pre, w_main, w_small, col_scale, tm, tn):
    m = x2.shape[0]
    grid = (m // tm, U_COLS // tn)
    return pl.pallas_call(
        _proj_kernel,
        out_shape=(jax.ShapeDtypeStruct((m, U_COLS), ACT_DTYPE),
                   jax.ShapeDtypeStruct((m, LANES), ACT_DTYPE),
                   jax.ShapeDtypeStruct((m, LANES), jnp.float32)),
        grid=grid,
        in_specs=[
            pl.BlockSpec((tm, D_MODEL), lambda i, j: (i, 0)),
            pl.BlockSpec((1, D_MODEL), lambda i, j: (0, 0)),
            pl.BlockSpec((tn, D_MODEL), lambda i, j: (j, 0)),
            pl.BlockSpec((2 * LANES, D_MODEL), lambda i, j: (0, 0)),
            pl.BlockSpec((1, tn), lambda i, j: (0, j)),
        ],
        out_specs=(
            pl.BlockSpec((tm, tn), lambda i, j: (i, j)),
            pl.BlockSpec((tm, LANES), lambda i, j: (i, 0)),
            pl.BlockSpec((tm, LANES), lambda i, j: (i, 0)),
        ),
        scratch_shapes=[pltpu.VMEM((tm, D_MODEL), MXU_DTYPE)],
        compiler_params=pltpu.CompilerParams(
            dimension_semantics=("arbitrary", "arbitrary"),
            vmem_limit_bytes=VMEM_LIMIT),
        name="proj",
    )(x2, g_pre, w_main, w_small, col_scale)


def _split3(x):
    hi = x.astype(MXU_DTYPE)
    r1 = x - hi.astype(jnp.float32)
    mid = r1.astype(MXU_DTYPE)
    lo = (r1 - mid.astype(jnp.float32)).astype(MXU_DTYPE)
    return hi, mid, lo


def _gla_kernel(q_ref, k_ref, v_ref, z_ref, sm_ref, wup_ref, ba_ref, gg_ref,
                o_ref, st_ref, qd_ref, kd_ref, kt_ref, *, chunks_per_step):
    c = GLA_CHUNK
    rows = chunks_per_step * c
    heads = range(GLA_HEADS)

    @pl.when(pl.program_id(1) == 0)
    def _():
        st_ref[...] = jnp.zeros_like(st_ref)

    def ks(h):
        return slice(h * GLA_DK, (h + 1) * GLA_DK)

    def vs(h):
        return slice(h * GLA_DV, (h + 1) * GLA_DV)

    def chunk(i):
        return slice(i * c, (i + 1) * c)

    row = lax.broadcasted_iota(jnp.int32, (rows, rows), 0)
    col = lax.broadcasted_iota(jnp.int32, (rows, rows), 1)
    causal = (row // c == col // c) & (col <= row)
    tril = jnp.where(causal, 1.0, 0.0).astype(MXU_DTYPE)

    ga = sm_ref[:, SMALL_GA:SMALL_GA + GLA_RANK].astype(MXU_DTYPE)
    pre = _dot(ga, wup_ref[...]) + ba_ref[...]
    log_a = (jnp.minimum(pre, 0.0) - jnp.log1p(jnp.exp(-jnp.abs(pre)))) / GLA_GATE_NORM
    hi, mid, lo = _split3(log_a)
    b = _dot(tril, hi) + _dot(tril, mid) + _dot(tril, lo)
    bl = jnp.concatenate(
        [jnp.broadcast_to(b[i * c + c - 1:i * c + c, :], (c, b.shape[1]))
         for i in range(chunks_per_step)], axis=0)
    q = q_ref[...].astype(jnp.float32)
    k = k_ref[...].astype(jnp.float32)
    qd_ref[...] = (q * jnp.exp(b)).astype(MXU_DTYPE)
    kd_ref[...] = (k * jnp.exp(-b)).astype(MXU_DTYPE)
    kt_ref[...] = (k * jnp.exp(bl - b)).astype(MXU_DTYPE)
    decay = [jnp.exp(b[i * c + c - 1:i * c + c, :]) for i in range(chunks_per_step)]

    att = [_nt_dot(qd_ref[:, ks(h)], kd_ref[:, ks(h)]) for h in heads]
    att = [jnp.where(causal, a, 0.0).astype(MXU_DTYPE) for a in att]
    o = [_dot(att[h], v_ref[:, vs(h)]) for h in heads]
    kv = [[_tn_dot(v_ref[chunk(i), vs(h)], kt_ref[chunk(i), ks(h)]) for h in heads]
          for i in range(chunks_per_step)]

    st = [st_ref[h] for h in heads]
    o_inter = [[] for _ in heads]
    for i in range(chunks_per_step):
        for h in heads:
            o_inter[h].append(_nt_dot(qd_ref[chunk(i), ks(h)], st[h].astype(MXU_DTYPE)))
            st[h] = st[h] * decay[i][:, ks(h)] + kv[i][h]
    for h in heads:
        st_ref[h] = st[h]
        oh = o[h] + jnp.concatenate(o_inter[h], axis=0)
        y = oh * lax.rsqrt(jnp.mean(oh * oh, axis=-1, keepdims=True) + EPS)
        y = y * gg_ref[...]
        zz = z_ref[:, vs(h)].astype(jnp.float32)
        o_ref[:, vs(h)] = (y * _silu(zz)).astype(o_ref.dtype)


def _gla(u, small, w_up, b_a, g_gla, batch, seq, chunks_per_step):
    rows = chunks_per_step * GLA_CHUNK
    steps = seq // rows
    hk = GLA_HEADS * GLA_DK
    hv = GLA_HEADS * GLA_DV

    def rmap(cb):
        return lambda b, s: (b * steps + s, cb)

    return pl.pallas_call(
        functools.partial(_gla_kernel, chunks_per_step=chunks_per_step),
        out_shape=jax.ShapeDtypeStruct((batch * seq, hv), ACT_DTYPE),
        grid=(batch, steps),
        in_specs=[
            pl.BlockSpec((rows, hk), rmap(COL_GQ // hk)),
            pl.BlockSpec((rows, hk), rmap(COL_GK // hk)),
            pl.BlockSpec((rows, hv), rmap(COL_GV // hv)),
            pl.BlockSpec((rows, hv), rmap(COL_GZ // hv)),
            pl.BlockSpec((rows, LANES), rmap(0)),
            pl.BlockSpec((GLA_RANK, hk), lambda b, s: (0, 0)),
            pl.BlockSpec((1, hk), lambda b, s: (0, 0)),
            pl.BlockSpec((1, GLA_DV), lambda b, s: (0, 0)),
        ],
        out_specs=pl.BlockSpec((rows, hv), rmap(0)),
        scratch_shapes=[pltpu.VMEM((GLA_HEADS, GLA_DV, GLA_DK), jnp.float32),
                        pltpu.VMEM((rows, hk), MXU_DTYPE),
                        pltpu.VMEM((rows, hk), MXU_DTYPE),
                        pltpu.VMEM((rows, hk), MXU_DTYPE)],
        compiler_params=pltpu.CompilerParams(
            dimension_semantics=("arbitrary", "arbitrary"),
            vmem_limit_bytes=VMEM_LIMIT),
        name="gla",
    )(u, u, u, u, small, w_up, b_a, g_gla)


KEY_CHUNK = 4 * Q_BLOCK
BLOCKS_PER_CHUNK = KEY_CHUNK // Q_BLOCK
PLANE_GROUP = 32 * SUBLANES


def _t5_bucket(dist):
    max_exact = REL_BUCKETS // 2
    d = jnp.maximum(dist, 1).astype(jnp.float32)
    large = max_exact + jnp.floor(jnp.log(d / max_exact) / math.log(REL_MAX_DIST / max_exact)
                                  * (REL_BUCKETS - max_exact)).astype(jnp.int32)
    large = jnp.minimum(large, REL_BUCKETS - 1)
    return jnp.where(dist < max_exact, dist, large)


def _dsa_kernel(rb_ref, dq_ref, dk_ref, dv_ref, iq_ref, ikd_ref, sm_ref, dz_ref,
                out_ref,
                kt_ref, sc_ref, planes_ref, mb_ref, vt_ref, bt_ref, iqm_ref, qaug_ref, acc_ref,
                lga_ref, lgb_ref, mxa_ref, mxb_ref, ml_ref,
                *, seq):
    qb = pl.program_id(1)
    blk = Q_BLOCK
    ch = KEY_CHUNK
    k_sel = min(TOPK_MAX, seq // 4)
    n_chunks = qb // BLOCKS_PER_CHUNK + 1
    width = DSA_GROUPS * blk

    row_i = lax.broadcasted_iota(jnp.int32, (blk, blk), 0)
    col_i = lax.broadcasted_iota(jnp.int32, (blk, blk), 1)
    crow_i = lax.broadcasted_iota(jnp.int32, (ch, blk), 0)
    q_pos = qb * blk + lax.broadcasted_iota(jnp.int32, (ch, blk), 1)

    @pl.when((pl.program_id(0) == 0) & (qb == 0))
    def _():
        planes_ref[...] = jnp.zeros_like(planes_ref)
        kt_ref[...] = jnp.zeros_like(kt_ref)
        for delta in range(3):
            dist = jnp.maximum(delta * blk + col_i - row_i, 0)
            bucket = _t5_bucket(dist)
            for h in range(DSA_HEADS):
                tile = jnp.zeros((blk, blk), jnp.float32)
                for bk in range(REL_BUCKETS):
                    tile = jnp.where(bucket == bk, rb_ref[bk, h] * LOG2_E, tile)
                cc, g = divmod(h, DSA_GROUPS)
                bt_ref[delta, cc, :, g * blk:(g + 1) * blk] = tile

    @pl.when(qb == 0)
    def _():
        def body(i, carry):
            r = pl.multiple_of(i * ch, ch)
            vt_ref[i] = dv_ref[pl.ds(r, ch), :].astype(jnp.float32).T.astype(vt_ref.dtype)
            return carry
        lax.fori_loop(0, seq // ch, body, 0)

    w_t = sm_ref[...].T[SMALL_IW:SMALL_IW + IDX_HEADS, :]
    w_t = w_t * (IDX_HEADS ** -0.5) * (IDX_DIM ** -0.5)

    lane = lax.broadcasted_iota(jnp.int32, (blk, LANES), 1)
    for h in range(IDX_HEADS):
        pair = iq_ref[:, (h // 2) * LANES:(h // 2 + 1) * LANES]
        keep = (lane < IDX_DIM) if h % 2 == 0 else (lane >= IDX_DIM)
        iqm_ref[h // 2, (h % 2) * blk:(h % 2 + 1) * blk, :] = jnp.where(keep, pair, jnp.zeros_like(pair))

    def score_chunk(c, causal):
        r = pl.multiple_of(c * ch, ch)
        ikc = ikd_ref[pl.ds(r, ch), :]
        acc = None
        for hp in range(IDX_HEADS // 2):
            s2 = _nt_dot(ikc, iqm_ref[hp])
            for h in (2 * hp, 2 * hp + 1):
                t = w_t[h:h + 1, :] * jnp.maximum(s2[:, (h % 2) * blk:(h % 2 + 1) * blk], 0.0)
                acc = t if acc is None else acc + t
        if causal:
            acc = jnp.where(r + crow_i <= q_pos, acc, -jnp.inf)
        sc_ref[pl.ds(r, ch), :] = acc
        bits = lax.bitcast_convert_type(acc, jnp.int32)
        kt_ref[pl.ds(r, ch), :] = bits ^ ((bits >> 31) & 0x7FFFFFFF)

    def build_planes(c):
        for g in range(ch // PLANE_GROUP):
            base = c * ch + g * PLANE_GROUP
            a = [kt_ref[pl.ds(pl.multiple_of(base + SUBLANES * v, SUBLANES), SUBLANES), :]
                 for v in range(32)]
            j, m = 16, 0x0000FFFF
            while j:
                k0 = 0
                while k0 < 32:
                    t = (a[k0] ^ lax.shift_right_logical(a[k0 + j], jnp.int32(j))) & m
                    a[k0] = a[k0] ^ t
                    a[k0 + j] = a[k0 + j] ^ (t << j)
                    k0 = (k0 + j + 1) & ~j
                j >>= 1
                m = (m ^ (m << j)) & 0xFFFFFFFF if j else m
                m = m - (1 << 32) if m >= (1 << 31) else m
            a[0] = ~a[0]
            row = pl.multiple_of(c * (ch // 32) + g * SUBLANES, SUBLANES)
            for jj in range(32):
                planes_ref[jj, pl.ds(row, SUBLANES), :] = a[jj]

    @pl.when(n_chunks > 1)
    def _():
        score_chunk(0, False)

    def score_body(c, carry):
        build_planes(c - 1)
        score_chunk(c, False)
        return carry
    lax.fori_loop(1, n_chunks - 1, score_body, 0)
    build_planes(jnp.maximum(n_chunks - 2, 0))
    score_chunk(n_chunks - 1, True)
    build_planes(n_chunks - 1)

    n_rows = seq // 32

    def rowsum(x):
        part = jnp.sum(x.reshape(n_rows // SUBLANES, SUBLANES, blk), axis=0)
        return jnp.sum(part, axis=0, keepdims=True)

    prow = lax.broadcasted_iota(jnp.int32, (n_rows, blk), 0)
    alive0 = jnp.where(prow < n_chunks * (ch // 32), -1, 0).astype(jnp.int32)

    def bit_body(j, carry):
        alive, cnt_gt, ukey = carry
        w = planes_ref[j]
        ones = alive & w
        c1 = rowsum(lax.population_count(ones))
        take = cnt_gt + c1 >= k_sel
        alive = jnp.where(take, ones, alive & ~w)
        cnt_gt = jnp.where(take, cnt_gt, cnt_gt + c1)
        ukey = jnp.where(take, ukey | (jnp.int32(1) << (31 - j)), ukey)
        return alive, cnt_gt, ukey

    zero_row = jnp.zeros((1, blk), jnp.int32)
    alive, cnt_gt, ukey = lax.fori_loop(0, 32, bit_body, (alive0, zero_row, zero_row))
    ans = ukey ^ INT_MIN
    thr_bits = jnp.where(ans < 0, ans ^ 0x7FFFFFFF, ans)
    thr0 = lax.bitcast_convert_type(thr_bits, jnp.float32)

    def tile_sum(x):
        return jnp.sum(x.reshape(ch // SUBLANES, SUBLANES, blk), axis=0)

    def fold_rows(acc, combine):
        return functools.reduce(combine, [acc[i:i + 1] for i in range(SUBLANES)])

    def mask_pass(thr, p_max):
        def body(c, carry):
            a_gt, a_ge = carry
            r = pl.multiple_of(c * ch, ch)
            sc = sc_ref[pl.ds(r, ch), :]
            k_pos = r + crow_i
            gt = sc > thr
            ge = sc >= thr
            sel = (gt | (ge & (k_pos <= p_max))) & (k_pos <= q_pos)
            mb_ref[pl.ds(r, ch), :] = jnp.where(sel, 0.0, MASK_NEG).astype(mb_ref.dtype)
            return (a_gt + tile_sum(jnp.where(gt, 1, 0).astype(jnp.int32)),
                    a_ge + tile_sum(jnp.where(ge, 1, 0).astype(jnp.int32)))
        zero = jnp.zeros((SUBLANES, blk), jnp.int32)
        a_gt, a_ge = lax.fori_loop(0, n_chunks, body, (zero, zero))
        return fold_rows(a_gt, jnp.add), fold_rows(a_ge, jnp.add)

    def nearest(pred_fn, fill, combine):
        def body(c, acc):
            r = pl.multiple_of(c * ch, ch)
            sc = sc_ref[pl.ds(r, ch), :]
            part = jnp.where(pred_fn(sc), sc, fill).reshape(ch // SUBLANES, SUBLANES, blk)
            return combine(acc, functools.reduce(combine, [part[i] for i in range(ch // SUBLANES)]))
        acc = lax.fori_loop(0, n_chunks, body, jnp.full((SUBLANES, blk), fill, jnp.float32))
        return fold_rows(acc, combine)

    def off_target(state):
        _, c_gt, c_ge = state
        return jnp.max(jnp.where((c_gt >= k_sel) | (c_ge < k_sel), 1, 0)) > 0

    def walk(state):
        thr, c_gt, c_ge = state
        above = nearest(lambda sc: sc > thr, jnp.inf, jnp.minimum)
        below = nearest(lambda sc: sc < thr, -jnp.inf, jnp.maximum)
        thr = jnp.where(c_gt >= k_sel, above, jnp.where(c_ge < k_sel, below, thr))
        return (thr,) + mask_pass(thr, no_bound)

    no_bound = jnp.full((1, blk), 2 * seq, jnp.int32)
    thr, cnt_gt, cnt_ge = lax.while_loop(off_target, walk, (thr0,) + mask_pass(thr0, no_bound))

    @pl.when(jnp.max(cnt_ge) > k_sel)
    def _():
        need = k_sel - cnt_gt
        n_bits = (2 * seq - 1).bit_length()

        def tie_count(cand):
            def body(c, acc):
                r = pl.multiple_of(c * ch, ch)
                hit = (sc_ref[pl.ds(r, ch), :] == thr) & (r + crow_i < cand)
                return acc + tile_sum(jnp.where(hit, 1, 0).astype(jnp.int32))
            return fold_rows(lax.fori_loop(0, n_chunks, body, jnp.zeros((SUBLANES, blk), jnp.int32)), jnp.add)

        def pos_body(i, p):
            cand = p | (jnp.int32(1) << (n_bits - 1 - i))
            return jnp.where(tie_count(cand) <= need - 1, cand, p)
        p_max = lax.fori_loop(0, n_bits, pos_body, jnp.zeros((1, blk), jnp.int32))
        mask_pass(thr, p_max)

    eye = jnp.where(row_i == col_i, 1.0, 0.0).astype(MXU_DTYPE)
    for cc in range(DSA_KV_HEADS):
        for g in range(DSA_GROUPS):
            h = cc * DSA_GROUPS + g
            qaug_ref[cc, g * blk:(g + 1) * blk, 0:DSA_HD] = dq_ref[:, h * DSA_HD:(h + 1) * DSA_HD]
            qaug_ref[cc, g * blk:(g + 1) * blk, DSA_HD:2 * DSA_HD] = eye
    acc_ref[...] = jnp.zeros_like(acc_ref)
    for cc in range(DSA_KV_HEADS):
        ml_ref[cc, 0] = jnp.full(ml_ref.shape[2:], -jnp.inf, jnp.float32)
        ml_ref[cc, 1] = jnp.zeros(ml_ref.shape[2:], jnp.float32)

    def stage_logits(c, lg_ref, mx_ref):
        r = pl.multiple_of(c * ch, ch)
        mbc = mb_ref[pl.ds(r, ch), :]
        for cc in range(DSA_KV_HEADS):
            kaug = jnp.concatenate([dk_ref[pl.ds(r, ch), cc * DSA_HD:(cc + 1) * DSA_HD], mbc], axis=1)
            bias = jnp.concatenate(
                [bt_ref[jnp.clip(qb - (c * BLOCKS_PER_CHUNK + j), 0, 2), cc]
                 for j in range(BLOCKS_PER_CHUNK)], axis=0)
            lg = _nt_dot(kaug, qaug_ref[cc]) + bias
            lg_ref[cc] = lg
            mx_ref[cc] = jnp.broadcast_to(jnp.max(lg, axis=0, keepdims=True), mx_ref.shape[1:])

    def stage_softmax(c, lg_ref, mx_ref):
        for cc in range(DSA_KV_HEADS):
            m = ml_ref[cc, 0, 0:1, :]
            l = ml_ref[cc, 1, 0:1, :]
            m_new = jnp.maximum(m, mx_ref[cc, 0:1, :])
            alpha = jnp.exp2(m - m_new)
            p = jnp.exp2(lg_ref[cc] - m_new)
            l_new = alpha * l + jnp.sum(p, axis=0, keepdims=True)
            pv = _dot(vt_ref[c, cc * DSA_HD:(cc + 1) * DSA_HD, :], p.astype(MXU_DTYPE))
            acc_ref[cc] = acc_ref[cc] * alpha + pv
            ml_ref[cc, 0] = jnp.broadcast_to(m_new, ml_ref.shape[2:])
            ml_ref[cc, 1] = jnp.broadcast_to(l_new, ml_ref.shape[2:])

    stage_logits(0, lga_ref, mxa_ref)

    def pair_body(pi, carry):
        c0 = 2 * pi
        stage_logits(c0 + 1, lgb_ref, mxb_ref)
        stage_softmax(c0, lga_ref, mxa_ref)

        stage_logits(jnp.minimum(c0 + 2, n_chunks - 1), lga_ref, mxa_ref)
        stage_softmax(c0 + 1, lgb_ref, mxb_ref)
        return carry
    lax.fori_loop(0, n_chunks // 2, pair_body, 0)

    @pl.when(n_chunks % 2 == 1)
    def _():
        stage_softmax(n_chunks - 1, lga_ref, mxa_ref)

    for cc in range(DSA_KV_HEADS):
        o_t = acc_ref[cc] / ml_ref[cc, 1, 0:1, :]
        for g in range(DSA_GROUPS):
            h = cc * DSA_GROUPS + g
            o = o_t[:, g * blk:(g + 1) * blk].T
            zz = dz_ref[:, h * DSA_HD:(h + 1) * DSA_HD].astype(jnp.float32)
            out_ref[:, h * DSA_HD:(h + 1) * DSA_HD] = (o * _silu(zz)).astype(out_ref.dtype)


def _dsa(u, ikd, small, rel_bias, batch, seq):
    nb = seq // Q_BLOCK
    hq = DSA_HEADS * DSA_HD
    hkv = DSA_KV_HEADS * DSA_HD
    hi = IDX_HEADS * IDX_DIM

    def qmap(cb):
        return lambda b, q: (b * nb + q, cb)

    def bmap(cb):
        return lambda b, q: (b, cb)

    return pl.pallas_call(
        functools.partial(_dsa_kernel, seq=seq),
        out_shape=jax.ShapeDtypeStruct((batch * seq, hq), ACT_DTYPE),
        grid=(batch, nb),
        in_specs=[
            pl.BlockSpec(memory_space=pltpu.SMEM),
            pl.BlockSpec((Q_BLOCK, hq), qmap(COL_DQ // hq)),
            pl.BlockSpec((seq, hkv), bmap(COL_DK // hkv)),
            pl.BlockSpec((seq, hkv), bmap(COL_DV // hkv)),
            pl.BlockSpec((Q_BLOCK, hi), qmap(COL_IQ // hi)),
            pl.BlockSpec((seq, LANES), bmap(0)),
            pl.BlockSpec((Q_BLOCK, LANES), qmap(0)),
            pl.BlockSpec((Q_BLOCK, hq), qmap(COL_DZ // hq)),
        ],
        out_specs=pl.BlockSpec((Q_BLOCK, hq), qmap(0)),
        scratch_shapes=[
            pltpu.VMEM((seq, Q_BLOCK), jnp.int32),
            pltpu.VMEM((seq, Q_BLOCK), jnp.float32),
            pltpu.VMEM((32, seq // 32, Q_BLOCK), jnp.int32),
            pltpu.VMEM((seq, Q_BLOCK), MXU_DTYPE),
            pltpu.VMEM((seq // KEY_CHUNK, hkv, KEY_CHUNK), MXU_DTYPE),
            pltpu.VMEM((3, DSA_KV_HEADS, Q_BLOCK, DSA_GROUPS * Q_BLOCK), jnp.float32),
            pltpu.VMEM((IDX_HEADS // 2, 2 * Q_BLOCK, LANES), MXU_DTYPE),
            pltpu.VMEM((DSA_KV_HEADS, DSA_GROUPS * Q_BLOCK, 2 * DSA_HD), MXU_DTYPE),
            pltpu.VMEM((DSA_KV_HEADS, DSA_HD, DSA_GROUPS * Q_BLOCK), jnp.float32),
            pltpu.VMEM((DSA_KV_HEADS, KEY_CHUNK, DSA_GROUPS * Q_BLOCK), jnp.float32),
            pltpu.VMEM((DSA_KV_HEADS, KEY_CHUNK, DSA_GROUPS * Q_BLOCK), jnp.float32),
            pltpu.VMEM((DSA_KV_HEADS, SUBLANES, DSA_GROUPS * Q_BLOCK), jnp.float32),
            pltpu.VMEM((DSA_KV_HEADS, SUBLANES, DSA_GROUPS * Q_BLOCK), jnp.float32),
            pltpu.VMEM((DSA_KV_HEADS, 2, SUBLANES, DSA_GROUPS * Q_BLOCK), jnp.float32),
        ],
        compiler_params=pltpu.CompilerParams(
            dimension_semantics=("arbitrary", "arbitrary"),
            vmem_limit_bytes=VMEM_LIMIT),
        name="dsa",
    )(rel_bias, u, u, u, u, ikd, small, u)


def _memkv_kernel(mem_ref, g_ref, w_ref, o_ref):
    xf = mem_ref[...]
    y = xf * lax.rsqrt(jnp.mean(xf * xf, axis=-1, keepdims=True) + EPS)
    hb = (y * g_ref[...]).astype(MXU_DTYPE)
    o_ref[...] = _dot(hb, w_ref[...]).astype(o_ref.dtype)


def _memkv(mem2, g_mem, w_kv, batch):
    n = 2 * X_HEADS * X_HD
    return pl.pallas_call(
        _memkv_kernel,
        out_shape=jax.ShapeDtypeStruct((batch * N_MEM, n), ACT_DTYPE),
        grid=(batch,),
        in_specs=[
            pl.BlockSpec((N_MEM, D_MODEL), lambda b: (b, 0)),
            pl.BlockSpec((1, D_MODEL), lambda b: (0, 0)),
            pl.BlockSpec((D_MODEL, n), lambda b: (0, 0)),
        ],
        out_specs=pl.BlockSpec((N_MEM, n), lambda b: (b, 0)),
        compiler_params=pltpu.CompilerParams(
            dimension_semantics=("arbitrary",),
            vmem_limit_bytes=VMEM_LIMIT),
        name="memkv",
    )(mem2, g_mem, w_kv)


def _merge_kernel(x_ref, yg_ref, yd_ref, xq_ref, xz_ref, sg_ref, sd_ref, sm_ref, mkv_ref,
                  wg_ref, wd_ref, wx_ref, wo_ref, gp_ref, o_ref, ym_ref):
    hw = X_HEADS * X_HD
    heads = range(X_HEADS)

    def cs(h):
        return slice(h * X_HD, (h + 1) * X_HD)

    lgs = [_nt_dot(xq_ref[:, cs(h)], mkv_ref[:, cs(h)]) for h in heads]
    t_g = _dot(yg_ref[...], wg_ref[...])
    ps = []
    for lg in lgs:
        e = jnp.exp(lg - jnp.max(lg, axis=-1, keepdims=True))
        ps.append((e / jnp.sum(e, axis=-1, keepdims=True)).astype(MXU_DTYPE))
    outs = [_dot(ps[h], mkv_ref[:, hw + h * X_HD:hw + (h + 1) * X_HD]) for h in heads]
    t_d = _dot(yd_ref[...], wd_ref[...])
    for h in heads:
        ym_ref[:, cs(h)] = (outs[h] * _silu(xz_ref[:, cs(h)].astype(jnp.float32))).astype(ym_ref.dtype)
    merged = jax.nn.sigmoid(sg_ref[...].astype(jnp.float32)) * t_g
    merged = merged + jax.nn.sigmoid(sd_ref[...].astype(jnp.float32)) * t_d
    merged = merged + jax.nn.sigmoid(sm_ref[...].astype(jnp.float32)) * _dot(ym_ref[...], wx_ref[...])
    t = _dot(merged.astype(MXU_DTYPE), wo_ref[...])
    y = t * lax.rsqrt(jnp.mean(t * t, axis=-1, keepdims=True) + EPS)
    o_ref[...] = x_ref[...] + y * gp_ref[...]


def _merge(x2, y_gla, y_dsa, u, mkv, w_g, w_d, w_x, w_o, g_post, seq, tm):
    m = x2.shape[0]
    steps_per_batch = seq // tm
    d = D_MODEL

    def rmap(cb):
        return lambda i: (i, cb)

    wspec = pl.BlockSpec((d, d), lambda i: (0, 0))
    return pl.pallas_call(
        _merge_kernel,
        out_shape=jax.ShapeDtypeStruct((m, d), jnp.float32),
        grid=(m // tm,),
        in_specs=[
            pl.BlockSpec((tm, d), rmap(0)),
            pl.BlockSpec((tm, d), rmap(0)),
            pl.BlockSpec((tm, d), rmap(0)),
            pl.BlockSpec((tm, d), rmap(COL_XQ // d)),
            pl.BlockSpec((tm, d), rmap(COL_XZ // d)),
            pl.BlockSpec((tm, d), rmap(COL_SG // d)),
            pl.BlockSpec((tm, d), rmap(COL_SD // d)),
            pl.BlockSpec((tm, d), rmap(COL_SM // d)),
            pl.BlockSpec((N_MEM, 2 * X_HEADS * X_HD), lambda i: (i // steps_per_batch, 0)),
            wspec, wspec, wspec, wspec,
            pl.BlockSpec((1, d), lambda i: (0, 0)),
        ],
        out_specs=pl.BlockSpec((tm, d), rmap(0)),
        scratch_shapes=[pltpu.VMEM((tm, d), MXU_DTYPE)],
        compiler_params=pltpu.CompilerParams(
            dimension_semantics=("arbitrary",),
            vmem_limit_bytes=VMEM_LIMIT),
        name="merge",
    )(x2, y_gla, y_dsa, u, u, u, u, u, mkv, w_g, w_d, w_x, w_o, g_post)


_MAIN_GROUPS = (0, 1, 2, 4, 5, 11, 12, 13, 14, 6, 7, 8)
_GROUP_GA, _GROUP_IK, _GROUP_IW = 3, 9, 10
RELAYOUT_COLS = 128


def _relayout_kernel(wt_ref, main_ref, small_ref):
    offs = np.concatenate([[0], np.cumsum(np.array(SPLIT_SIZES))]).tolist()

    def group(g):
        return wt_ref[offs[g]:offs[g + 1], :].astype(MXU_DTYPE)

    dst = 0
    for g in _MAIN_GROUPS:
        main_ref[dst:dst + SPLIT_SIZES[g], :] = group(g)
        dst += SPLIT_SIZES[g]
    pad = jnp.zeros((LANES - GLA_RANK - IDX_HEADS, wt_ref.shape[1]), MXU_DTYPE)
    small_ref[...] = jnp.concatenate(
        [group(_GROUP_IK), group(_GROUP_IK), group(_GROUP_GA), group(_GROUP_IW), pad], axis=0)


def _relayout_w_in(w_in):
    wt = w_in.T
    n, d = wt.shape
    return pl.pallas_call(
        _relayout_kernel,
        out_shape=(jax.ShapeDtypeStruct((U_COLS, d), MXU_DTYPE),
                   jax.ShapeDtypeStruct((2 * LANES, d), MXU_DTYPE)),
        grid=(d // RELAYOUT_COLS,),
        in_specs=[pl.BlockSpec((n, RELAYOUT_COLS), lambda i: (0, i))],
        out_specs=(pl.BlockSpec((U_COLS, RELAYOUT_COLS), lambda i: (0, i)),
                   pl.BlockSpec((2 * LANES, RELAYOUT_COLS), lambda i: (0, i))),
        compiler_params=pltpu.CompilerParams(
            dimension_semantics=("arbitrary",),
            vmem_limit_bytes=VMEM_LIMIT),
        name="relayout",
    )(wt)


def _col_scale():
    s = np.ones((1, U_COLS), np.float32)
    s[:, COL_GQ:COL_GQ + GLA_HEADS * GLA_DK] = GLA_DK ** -0.5
    s[:, COL_DQ:COL_DQ + DSA_HEADS * DSA_HD] = DSA_HD ** -0.5 * LOG2_E
    s[:, COL_XQ:COL_XQ + X_HEADS * X_HD] = X_HD ** -0.5
    return jnp.asarray(s)


def _layer(x2, mem2, g_pre, g_post, g_mem, w_in, w_up, b_a, g_gla, rel_bias, w_kv,
           w_g, w_d, w_x, w_o, batch, seq):
    w_main, w_small = _relayout_w_in(w_in)
    tm = min(PROJ_TM, batch * seq)
    u, ikd, small = _proj(x2, g_pre.reshape(1, -1), w_main, w_small, _col_scale(), tm, PROJ_TN)
    y_gla = _gla(u, small, w_up.astype(MXU_DTYPE), b_a.reshape(1, -1), g_gla.reshape(1, -1),
                 batch, seq, chunks_per_step=4)
    y_dsa = _dsa(u, ikd, small, rel_bias, batch, seq)
    mkv = _memkv(mem2, g_mem.reshape(1, -1), w_kv.astype(MXU_DTYPE), batch)
    return _merge(x2, y_gla, y_dsa, u, mkv, w_g.astype(MXU_DTYPE), w_d.astype(MXU_DTYPE),
                  w_x.astype(MXU_DTYPE), w_o.astype(MXU_DTYPE), g_post.reshape(1, -1), seq, MERGE_TM)


def kernel(x, mem, g_pre, g_post, g_mem, w_in, w_gla_a_up, b_gla_a, g_gla, rel_bias,
           w_mem_kv, w_gla_out, w_dsa_out, w_x_out, w_o):
    batch, seq, d = x.shape
    x2 = x.reshape(batch * seq, d)
    mem2 = mem.reshape(batch * N_MEM, d)
    for i in range(g_pre.shape[0]):
        x2 = _layer(x2, mem2, g_pre[i], g_post[i], g_mem[i], w_in[i], w_gla_a_up[i],
                    b_gla_a[i], g_gla[i], rel_bias, w_mem_kv[i], w_gla_out[i],
                    w_dsa_out[i], w_x_out[i], w_o[i], batch, seq)
    return x2.reshape(batch, seq, d)
```

```python
import functools
import math

import jax
import jax.numpy as jnp
import numpy as np
from jax import lax
from jax.experimental import pallas as pl
from jax.experimental.pallas import tpu as pltpu

D_MODEL = 1024
N_MEM = 256
EPS = 1e-6
GLA_HEADS = 4
GLA_DK = 128
GLA_DV = 256
GLA_RANK = 16
GLA_GATE_NORM = 16.0
GLA_CHUNK = 64
DSA_HEADS = 8
DSA_KV_HEADS = 2
DSA_GROUPS = DSA_HEADS // DSA_KV_HEADS
DSA_HD = 128
IDX_HEADS = 8
IDX_DIM = 64
TOPK_MAX = 256
Q_BLOCK = 128
REL_BUCKETS = 32
REL_MAX_DIST = 128
X_HEADS = 4
X_HD = 256

SPLIT_SIZES = (512, 512, 1024, 16, 1024, 1024, 256, 256, 512, 64, 8, 1024, 1024, 1024, 3072)

LANES = 128
SUBLANES = 8

MXU_DTYPE = jnp.bfloat16
ACT_DTYPE = jnp.bfloat16

U_COLS = 11264
COL_GQ, COL_GK, COL_GV, COL_GZ = 0, 512, 1024, 2048
COL_DQ, COL_DZ, COL_XQ, COL_XZ = 3072, 4096, 5120, 6144
COL_SG, COL_SD, COL_SM = 7168, 8192, 9216
COL_DK, COL_DV, COL_IQ = 10240, 10496, 10752
SMALL_GA, SMALL_IW = 0, 16

MASK_NEG = -1e30
LOG2_E = math.log2(math.e)
INT_MIN = -(2 ** 31)
VMEM_LIMIT = 56 * 1024 * 1024

PROJ_TM = 1024
PROJ_TN = U_COLS // 4
MERGE_TM = 512


def _nt_dot(a, b):
    return lax.dot_general(a, b, (((1,), (1,)), ((), ())),
                           preferred_element_type=jnp.float32)


def _tn_dot(a, b):
    return lax.dot_general(a, b, (((0,), (0,)), ((), ())),
                           preferred_element_type=jnp.float32)


def _dot(a, b):
    return jnp.dot(a, b, preferred_element_type=jnp.float32)


def _silu(z):
    return z * jax.nn.sigmoid(z)


def _proj_kernel(x_ref, g_ref, w_ref, ws_ref, cs_ref, u_ref, ikd_ref, sm_ref, h_ref):
    @pl.when(pl.program_id(1) == 0)
    def _():
        xf = x_ref[...]
        y = xf * lax.rsqrt(jnp.mean(xf * xf, axis=-1, keepdims=True) + EPS)
        hb = (y * g_ref[...]).astype(MXU_DTYPE)
        h_ref[...] = hb
        r = _nt_dot(hb, ws_ref[...])
        ikd_ref[...] = r[:, :LANES].astype(ikd_ref.dtype)
        sm_ref[...] = r[:, LANES:]

    acc = _nt_dot(h_ref[...], w_ref[...])
    u_ref[...] = (acc * cs_ref[...]).astype(u_ref.dtype)


def _proj(x2, g_pre, w_main, w_small, col_scale, tm, tn):
    m = x2.shape[0]
    grid = (m // tm, U_COLS // tn)
    return pl.pallas_call(
        _proj_kernel,
        out_shape=(jax.ShapeDtypeStruct((m, U_COLS), ACT_DTYPE),
                   jax.ShapeDtypeStruct((m, LANES), ACT_DTYPE),
                   jax.ShapeDtypeStruct((m, LANES), jnp.float32)),
        grid=grid,
        in_specs=[
            pl.BlockSpec((tm, D_MODEL), lambda i, j: (i, 0)),
            pl.BlockSpec((1, D_MODEL), lambda i, j: (0, 0)),
            pl.BlockSpec((tn, D_MODEL), lambda i, j: (j, 0)),
            pl.BlockSpec((2 * LANES, D_MODEL), lambda i, j: (0, 0)),
            pl.BlockSpec((1, tn), lambda i, j: (0, j)),
        ],
        out_specs=(
            pl.BlockSpec((tm, tn), lambda i, j: (i, j)),
            pl.BlockSpec((tm, LANES), lambda i, j: (i, 0)),
            pl.BlockSpec((tm, LANES), lambda i, j: (i, 0)),
        ),
        scratch_shapes=[pltpu.VMEM((tm, D_MODEL), MXU_DTYPE)],
        compiler_params=pltpu.CompilerParams(
            dimension_semantics=("arbitrary", "arbitrary"),
            vmem_limit_bytes=VMEM_LIMIT),
        name="proj",
    )(x2, g_pre, w_main, w_small, col_scale)


def _split3(x):
    hi = x.astype(MXU_DTYPE)
    r1 = x - hi.astype(jnp.float32)
    mid = r1.astype(MXU_DTYPE)
    lo = (r1 - mid.astype(jnp.float32)).astype(MXU_DTYPE)
    return hi, mid, lo


def _gla_kernel(q_ref, k_ref, v_ref, z_ref, sm_ref, wup_ref, ba_ref, gg_ref,
                o_ref, st_ref, qd_ref, kd_ref, kt_ref, b_ref, oacc_ref, *, chunks_per_step):
    c = GLA_CHUNK
    rows = chunks_per_step * c
    heads = range(GLA_HEADS)

    @pl.when(pl.program_id(1) == 0)
    def _():
        st_ref[...] = jnp.zeros_like(st_ref)

    def ks(h):
        return slice(h * GLA_DK, (h + 1) * GLA_DK)

    def vs(h):
        return slice(h * GLA_DV, (h + 1) * GLA_DV)

    def chunk(i):
        return slice(i * c, (i + 1) * c)

    row = lax.broadcasted_iota(jnp.int32, (rows, rows), 0)
    col = lax.broadcasted_iota(jnp.int32, (rows, rows), 1)
    causal = (row // c == col // c) & (col <= row)
    tril = jnp.where(causal, 1.0, 0.0).astype(MXU_DTYPE)

    ga = sm_ref[:, SMALL_GA:SMALL_GA + GLA_RANK].astype(MXU_DTYPE)
    pre = _dot(ga, wup_ref[...]) + ba_ref[...]
    log_a = (jnp.minimum(pre, 0.0) - jnp.log1p(jnp.exp(-jnp.abs(pre)))) / GLA_GATE_NORM
    hi, mid, lo = _split3(log_a)
    b_ref[...] = _dot(tril, hi) + _dot(tril, mid) + _dot(tril, lo)

    decay = []
    for i in range(chunks_per_step):
        b_last = b_ref[i * c + c - 1:i * c + c, :]
        decay.append(jnp.exp(b_last))
        for h in heads:
            b = b_ref[chunk(i), ks(h)]
            q = q_ref[chunk(i), ks(h)].astype(jnp.float32)
            k = k_ref[chunk(i), ks(h)].astype(jnp.float32)
            qd_ref[chunk(i), ks(h)] = (q * jnp.exp(b)).astype(MXU_DTYPE)
            kd_ref[chunk(i), ks(h)] = (k * jnp.exp(-b)).astype(MXU_DTYPE)
            kt_ref[chunk(i), ks(h)] = (k * jnp.exp(b_last[:, ks(h)] - b)).astype(MXU_DTYPE)

    att = [_nt_dot(qd_ref[:, ks(h)], kd_ref[:, ks(h)]) for h in heads]
    att = [jnp.where(causal, a, 0.0).astype(MXU_DTYPE) for a in att]
    for h in heads:
        oacc_ref[:, vs(h)] = _dot(att[h], v_ref[:, vs(h)])

    st = [st_ref[h] for h in heads]
    for i in range(chunks_per_step):
        for h in heads:
            kv = _tn_dot(v_ref[chunk(i), vs(h)], kt_ref[chunk(i), ks(h)])
            oacc_ref[chunk(i), vs(h)] += _nt_dot(qd_ref[chunk(i), ks(h)], st[h].astype(MXU_DTYPE))
            st[h] = st[h] * decay[i][:, ks(h)] + kv
    for h in heads:
        st_ref[h] = st[h]
        oh = oacc_ref[:, vs(h)]
        y = oh * lax.rsqrt(jnp.mean(oh * oh, axis=-1, keepdims=True) + EPS)
        y = y * gg_ref[...]
        zz = z_ref[:, vs(h)].astype(jnp.float32)
        o_ref[:, vs(h)] = (y * _silu(zz)).astype(o_ref.dtype)


def _gla(u, small, w_up, b_a, g_gla, batch, seq, chunks_per_step):
    rows = chunks_per_step * GLA_CHUNK
    steps = seq // rows
    hk = GLA_HEADS * GLA_DK
    hv = GLA_HEADS * GLA_DV

    def rmap(cb):
        return lambda b, s: (b * steps + s, cb)

    return pl.pallas_call(
        functools.partial(_gla_kernel, chunks_per_step=chunks_per_step),
        out_shape=jax.ShapeDtypeStruct((batch * seq, hv), ACT_DTYPE),
        grid=(batch, steps),
        in_specs=[
            pl.BlockSpec((rows, hk), rmap(COL_GQ // hk)),
            pl.BlockSpec((rows, hk), rmap(COL_GK // hk)),
            pl.BlockSpec((rows, hv), rmap(COL_GV // hv)),
            pl.BlockSpec((rows, hv), rmap(COL_GZ // hv)),
            pl.BlockSpec((rows, LANES), rmap(0)),
            pl.BlockSpec((GLA_RANK, hk), lambda b, s: (0, 0)),
            pl.BlockSpec((1, hk), lambda b, s: (0, 0)),
            pl.BlockSpec((1, GLA_DV), lambda b, s: (0, 0)),
        ],
        out_specs=pl.BlockSpec((rows, hv), rmap(0)),
        scratch_shapes=[pltpu.VMEM((GLA_HEADS, GLA_DV, GLA_DK), jnp.float32),
                        pltpu.VMEM((rows, hk), MXU_DTYPE),
                        pltpu.VMEM((rows, hk), MXU_DTYPE),
                        pltpu.VMEM((rows, hk), MXU_DTYPE),
                        pltpu.VMEM((rows, hk), jnp.float32),
                        pltpu.VMEM((rows, hv), jnp.float32)],
        compiler_params=pltpu.CompilerParams(
            dimension_semantics=("arbitrary", "arbitrary"),
            vmem_limit_bytes=VMEM_LIMIT),
        name="gla",
    )(u, u, u, u, small, w_up, b_a, g_gla)


KEY_CHUNK = 4 * Q_BLOCK
BLOCKS_PER_CHUNK = KEY_CHUNK // Q_BLOCK
PLANE_GROUP = 32 * SUBLANES


def _t5_bucket(dist):
    max_exact = REL_BUCKETS // 2
    d = jnp.maximum(dist, 1).astype(jnp.float32)
    large = max_exact + jnp.floor(jnp.log(d / max_exact) / math.log(REL_MAX_DIST / max_exact)
                                  * (REL_BUCKETS - max_exact)).astype(jnp.int32)
    large = jnp.minimum(large, REL_BUCKETS - 1)
    return jnp.where(dist < max_exact, dist, large)


def _dsa_kernel(rb_ref, dq_ref, dk_ref, dv_ref, iq_ref, ikd_ref, sm_ref, dz_ref,
                out_ref,
                kt_ref, sc_ref, planes_ref, mb_ref, vt_ref, bt_ref, iqm_ref, qaug_ref, acc_ref,
                lga_ref, lgb_ref, mxa_ref, mxb_ref, ml_ref,
                *, seq):
    qb = pl.program_id(1)
    blk = Q_BLOCK
    ch = KEY_CHUNK
    k_sel = min(TOPK_MAX, seq // 4)
    n_chunks = qb // BLOCKS_PER_CHUNK + 1
    width = DSA_GROUPS * blk

    row_i = lax.broadcasted_iota(jnp.int32, (blk, blk), 0)
    col_i = lax.broadcasted_iota(jnp.int32, (blk, blk), 1)
    crow_i = lax.broadcasted_iota(jnp.int32, (ch, blk), 0)
    q_pos = qb * blk + lax.broadcasted_iota(jnp.int32, (ch, blk), 1)

    @pl.when((pl.program_id(0) == 0) & (qb == 0))
    def _():
        planes_ref[...] = jnp.zeros_like(planes_ref)
        kt_ref[...] = jnp.zeros_like(kt_ref)
        for delta in range(3):
            dist = jnp.maximum(delta * blk + col_i - row_i, 0)
            bucket = _t5_bucket(dist)
            for h in range(DSA_HEADS):
                tile = jnp.zeros((blk, blk), jnp.float32)
                for bk in range(REL_BUCKETS):
                    tile = jnp.where(bucket == bk, rb_ref[bk, h] * LOG2_E, tile)
                cc, g = divmod(h, DSA_GROUPS)
                bt_ref[delta, cc, :, g * blk:(g + 1) * blk] = tile

    @pl.when(qb == 0)
    def _():
        def body(i, carry):
            r = pl.multiple_of(i * ch, ch)
            vt_ref[i] = dv_ref[pl.ds(r, ch), :].astype(jnp.float32).T.astype(vt_ref.dtype)
            return carry
        lax.fori_loop(0, seq // ch, body, 0)

    w_t = sm_ref[...].T[SMALL_IW:SMALL_IW + IDX_HEADS, :]
    w_t = w_t * (IDX_HEADS ** -0.5) * (IDX_DIM ** -0.5)

    lane = lax.broadcasted_iota(jnp.int32, (blk, LANES), 1)
    for h in range(IDX_HEADS):
        pair = iq_ref[:, (h // 2) * LANES:(h // 2 + 1) * LANES]
        keep = (lane < IDX_DIM) if h % 2 == 0 else (lane >= IDX_DIM)
        iqm_ref[h // 2, (h % 2) * blk:(h % 2 + 1) * blk, :] = jnp.where(keep, pair, jnp.zeros_like(pair))

    def score_chunk(c, causal):
        r = pl.multiple_of(c * ch, ch)
        ikc = ikd_ref[pl.ds(r, ch), :]
        acc = None
        for hp in range(IDX_HEADS // 2):
            s2 = _nt_dot(ikc, iqm_ref[hp])
            for h in (2 * hp, 2 * hp + 1):
                t = w_t[h:h + 1, :] * jnp.maximum(s2[:, (h % 2) * blk:(h % 2 + 1) * blk], 0.0)
                acc = t if acc is None else acc + t
        if causal:
            acc = jnp.where(r + crow_i <= q_pos, acc, -jnp.inf)
        sc_ref[pl.ds(r, ch), :] = acc
        bits = lax.bitcast_convert_type(acc, jnp.int32)
        kt_ref[pl.ds(r, ch), :] = bits ^ ((bits >> 31) & 0x7FFFFFFF)

    def build_planes(c):
        for g in range(ch // PLANE_GROUP):
            base = c * ch + g * PLANE_GROUP
            a = [kt_ref[pl.ds(pl.multiple_of(base + SUBLANES * v, SUBLANES), SUBLANES), :]
                 for v in range(32)]
            j, m = 16, 0x0000FFFF
            while j:
                k0 = 0
                while k0 < 32:
                    t = (a[k0] ^ lax.shift_right_logical(a[k0 + j], jnp.int32(j))) & m
                    a[k0] = a[k0] ^ t
                    a[k0 + j] = a[k0 + j] ^ (t << j)
                    k0 = (k0 + j + 1) & ~j
                j >>= 1
                m = (m ^ (m << j)) & 0xFFFFFFFF if j else m
                m = m - (1 << 32) if m >= (1 << 31) else m
            a[0] = ~a[0]
            row = pl.multiple_of(c * (ch // 32) + g * SUBLANES, SUBLANES)
            for jj in range(32):
                planes_ref[jj, pl.ds(row, SUBLANES), :] = a[jj]

    @pl.when(n_chunks > 1)
    def _():
        score_chunk(0, False)

    def score_body(c, carry):
        build_planes(c - 1)
        score_chunk(c, False)
        return carry
    lax.fori_loop(1, n_chunks - 1, score_body, 0)
    build_planes(jnp.maximum(n_chunks - 2, 0))
    score_chunk(n_chunks - 1, True)
    build_planes(n_chunks - 1)

    n_rows = seq // 32

    def rowsum(x):
        part = jnp.sum(x.reshape(n_rows // SUBLANES, SUBLANES, blk), axis=0)
        return jnp.sum(part, axis=0, keepdims=True)

    prow = lax.broadcasted_iota(jnp.int32, (n_rows, blk), 0)
    alive0 = jnp.where(prow < n_chunks * (ch // 32), -1, 0).astype(jnp.int32)

    def bit_body(j, carry):
        alive, cnt_gt, ukey = carry
        w = planes_ref[j]
        ones = alive & w
        c1 = rowsum(lax.population_count(ones))
        take = cnt_gt + c1 >= k_sel
        alive = jnp.where(take, ones, alive & ~w)
        cnt_gt = jnp.where(take, cnt_gt, cnt_gt + c1)
        ukey = jnp.where(take, ukey | (jnp.int32(1) << (31 - j)), ukey)
        return alive, cnt_gt, ukey

    zero_row = jnp.zeros((1, blk), jnp.int32)
    alive, cnt_gt, ukey = lax.fori_loop(0, 32, bit_body, (alive0, zero_row, zero_row))
    ans = ukey ^ INT_MIN
    thr_bits = jnp.where(ans < 0, ans ^ 0x7FFFFFFF, ans)
    thr0 = lax.bitcast_convert_type(thr_bits, jnp.float32)

    def tile_sum(x):
        return jnp.sum(x.reshape(ch // SUBLANES, SUBLANES, blk), axis=0)

    def fold_rows(acc, combine):
        return functools.reduce(combine, [acc[i:i + 1] for i in range(SUBLANES)])

    def mask_pass(thr, p_max):
        def body(c, carry):
            a_gt, a_ge = carry
            r = pl.multiple_of(c * ch, ch)
            sc = sc_ref[pl.ds(r, ch), :]
            k_pos = r + crow_i
            gt = sc > thr
            ge = sc >= thr
            sel = (gt | (ge & (k_pos <= p_max))) & (k_pos <= q_pos)
            mb_ref[pl.ds(r, ch), :] = jnp.where(sel, 0.0, MASK_NEG).astype(mb_ref.dtype)
            return (a_gt + tile_sum(jnp.where(gt, 1, 0).astype(jnp.int32)),
                    a_ge + tile_sum(jnp.where(ge, 1, 0).astype(jnp.int32)))
        zero = jnp.zeros((SUBLANES, blk), jnp.int32)
        a_gt, a_ge = lax.fori_loop(0, n_chunks, body, (zero, zero))
        return fold_rows(a_gt, jnp.add), fold_rows(a_ge, jnp.add)

    def nearest(pred_fn, fill, combine):
        def body(c, acc):
            r = pl.multiple_of(c * ch, ch)
            sc = sc_ref[pl.ds(r, ch), :]
            part = jnp.where(pred_fn(sc), sc, fill).reshape(ch // SUBLANES, SUBLANES, blk)
            return combine(acc, functools.reduce(combine, [part[i] for i in range(ch // SUBLANES)]))
        acc = lax.fori_loop(0, n_chunks, body, jnp.full((SUBLANES, blk), fill, jnp.float32))
        return fold_rows(acc, combine)

    def off_target(state):
        _, c_gt, c_ge = state
        return jnp.max(jnp.where((c_gt >= k_sel) | (c_ge < k_sel), 1, 0)) > 0

    def walk(state):
        thr, c_gt, c_ge = state
        above = nearest(lambda sc: sc > thr, jnp.inf, jnp.minimum)
        below = nearest(lambda sc: sc < thr, -jnp.inf, jnp.maximum)
        thr = jnp.where(c_gt >= k_sel, above, jnp.where(c_ge < k_sel, below, thr))
        return (thr,) + mask_pass(thr, no_bound)

    no_bound = jnp.full((1, blk), 2 * seq, jnp.int32)
    thr, cnt_gt, cnt_ge = lax.while_loop(off_target, walk, (thr0,) + mask_pass(thr0, no_bound))

    @pl.when(jnp.max(cnt_ge) > k_sel)
    def _():
        need = k_sel - cnt_gt
        n_bits = (2 * seq - 1).bit_length()

        def tie_count(cand):
            def body(c, acc):
                r = pl.multiple_of(c * ch, ch)
                hit = (sc_ref[pl.ds(r, ch), :] == thr) & (r + crow_i < cand)
                return acc + tile_sum(jnp.where(hit, 1, 0).astype(jnp.int32))
            return fold_rows(lax.fori_loop(0, n_chunks, body, jnp.zeros((SUBLANES, blk), jnp.int32)), jnp.add)

        def pos_body(i, p):
            cand = p | (jnp.int32(1) << (n_bits - 1 - i))
            return jnp.where(tie_count(cand) <= need - 1, cand, p)
        p_max = lax.fori_loop(0, n_bits, pos_body, jnp.zeros((1, blk), jnp.int32))
        mask_pass(thr, p_max)

    eye = jnp.where(row_i == col_i, 1.0, 0.0).astype(MXU_DTYPE)
    for cc in range(DSA_KV_HEADS):
        for g in range(DSA_GROUPS):
            h = cc * DSA_GROUPS + g
            qaug_ref[cc, g * blk:(g + 1) * blk, 0:DSA_HD] = dq_ref[:, h * DSA_HD:(h + 1) * DSA_HD]
            qaug_ref[cc, g * blk:(g + 1) * blk, DSA_HD:2 * DSA_HD] = eye
    acc_ref[...] = jnp.zeros_like(acc_ref)
    for cc in range(DSA_KV_HEADS):
        ml_ref[cc, 0] = jnp.full(ml_ref.shape[2:], -jnp.inf, jnp.float32)
        ml_ref[cc, 1] = jnp.zeros(ml_ref.shape[2:], jnp.float32)

    def stage_logits(c, lg_ref, mx_ref):
        r = pl.multiple_of(c * ch, ch)
        mbc = mb_ref[pl.ds(r, ch), :]
        for cc in range(DSA_KV_HEADS):
            kaug = jnp.concatenate([dk_ref[pl.ds(r, ch), cc * DSA_HD:(cc + 1) * DSA_HD], mbc], axis=1)
            bias = jnp.concatenate(
                [bt_ref[jnp.clip(qb - (c * BLOCKS_PER_CHUNK + j), 0, 2), cc]
                 for j in range(BLOCKS_PER_CHUNK)], axis=0)
            lg = _nt_dot(kaug, qaug_ref[cc]) + bias
            lg_ref[cc] = lg
            mx_ref[cc] = jnp.broadcast_to(jnp.max(lg, axis=0, keepdims=True), mx_ref.shape[1:])

    def stage_softmax(c, lg_ref, mx_ref):
        for cc in range(DSA_KV_HEADS):
            m = ml_ref[cc, 0, 0:1, :]
            l = ml_ref[cc, 1, 0:1, :]
            m_new = jnp.maximum(m, mx_ref[cc, 0:1, :])
            alpha = jnp.exp2(m - m_new)
            p = jnp.exp2(lg_ref[cc] - m_new)
            l_new = alpha * l + jnp.sum(p, axis=0, keepdims=True)
            pv = _dot(vt_ref[c, cc * DSA_HD:(cc + 1) * DSA_HD, :], p.astype(MXU_DTYPE))
            acc_ref[cc] = acc_ref[cc] * alpha + pv
            ml_ref[cc, 0] = jnp.broadcast_to(m_new, ml_ref.shape[2:])
            ml_ref[cc, 1] = jnp.broadcast_to(l_new, ml_ref.shape[2:])

    stage_logits(0, lga_ref, mxa_ref)

    def pair_body(pi, carry):
        c0 = 2 * pi
        stage_logits(c0 + 1, lgb_ref, mxb_ref)
        stage_softmax(c0, lga_ref, mxa_ref)

        stage_logits(jnp.minimum(c0 + 2, n_chunks - 1), lga_ref, mxa_ref)
        stage_softmax(c0 + 1, lgb_ref, mxb_ref)
        return carry
    lax.fori_loop(0, n_chunks // 2, pair_body, 0)

    @pl.when(n_chunks % 2 == 1)
    def _():
        stage_softmax(n_chunks - 1, lga_ref, mxa_ref)

    for cc in range(DSA_KV_HEADS):
        o_t = acc_ref[cc] / ml_ref[cc, 1, 0:1, :]
        for g in range(DSA_GROUPS):
            h = cc * DSA_GROUPS + g
            o = o_t[:, g * blk:(g + 1) * blk].T
            zz = dz_ref[:, h * DSA_HD:(h + 1) * DSA_HD].astype(jnp.float32)
            out_ref[:, h * DSA_HD:(h + 1) * DSA_HD] = (o * _silu(zz)).astype(out_ref.dtype)


def _dsa(u, ikd, small, rel_bias, batch, seq):
    nb = seq // Q_BLOCK
    hq = DSA_HEADS * DSA_HD
    hkv = DSA_KV_HEADS * DSA_HD
    hi = IDX_HEADS * IDX_DIM

    def qmap(cb):
        return lambda b, q: (b * nb + q, cb)

    def bmap(cb):
        return lambda b, q: (b, cb)

    return pl.pallas_call(
        functools.partial(_dsa_kernel, seq=seq),
        out_shape=jax.ShapeDtypeStruct((batch * seq, hq), ACT_DTYPE),
        grid=(batch, nb),
        in_specs=[
            pl.BlockSpec(memory_space=pltpu.SMEM),
            pl.BlockSpec((Q_BLOCK, hq), qmap(COL_DQ // hq)),
            pl.BlockSpec((seq, hkv), bmap(COL_DK // hkv)),
            pl.BlockSpec((seq, hkv), bmap(COL_DV // hkv)),
            pl.BlockSpec((Q_BLOCK, hi), qmap(COL_IQ // hi)),
            pl.BlockSpec((seq, LANES), bmap(0)),
            pl.BlockSpec((Q_BLOCK, LANES), qmap(0)),
            pl.BlockSpec((Q_BLOCK, hq), qmap(COL_DZ // hq)),
        ],
        out_specs=pl.BlockSpec((Q_BLOCK, hq), qmap(0)),
        scratch_shapes=[
            pltpu.VMEM((seq, Q_BLOCK), jnp.int32),
            pltpu.VMEM((seq, Q_BLOCK), jnp.float32),
            pltpu.VMEM((32, seq // 32, Q_BLOCK), jnp.int32),
            pltpu.VMEM((seq, Q_BLOCK), MXU_DTYPE),
            pltpu.VMEM((seq // KEY_CHUNK, hkv, KEY_CHUNK), MXU_DTYPE),
            pltpu.VMEM((3, DSA_KV_HEADS, Q_BLOCK, DSA_GROUPS * Q_BLOCK), jnp.float32),
            pltpu.VMEM((IDX_HEADS // 2, 2 * Q_BLOCK, LANES), MXU_DTYPE),
            pltpu.VMEM((DSA_KV_HEADS, DSA_GROUPS * Q_BLOCK, 2 * DSA_HD), MXU_DTYPE),
            pltpu.VMEM((DSA_KV_HEADS, DSA_HD, DSA_GROUPS * Q_BLOCK), jnp.float32),
            pltpu.VMEM((DSA_KV_HEADS, KEY_CHUNK, DSA_GROUPS * Q_BLOCK), jnp.float32),
            pltpu.VMEM((DSA_KV_HEADS, KEY_CHUNK, DSA_GROUPS * Q_BLOCK), jnp.float32),
            pltpu.VMEM((DSA_KV_HEADS, SUBLANES, DSA_GROUPS * Q_BLOCK), jnp.float32),
            pltpu.VMEM((DSA_KV_HEADS, SUBLANES, DSA_GROUPS * Q_BLOCK), jnp.float32),
            pltpu.VMEM((DSA_KV_HEADS, 2, SUBLANES, DSA_GROUPS * Q_BLOCK), jnp.float32),
        ],
        compiler_params=pltpu.CompilerParams(
            dimension_semantics=("arbitrary", "arbitrary"),
            vmem_limit_bytes=VMEM_LIMIT),
        name="dsa",
    )(rel_bias, u, u, u, u, ikd, small, u)


def _memkv_kernel(mem_ref, g_ref, w_ref, o_ref):
    xf = mem_ref[...]
    y = xf * lax.rsqrt(jnp.mean(xf * xf, axis=-1, keepdims=True) + EPS)
    hb = (y * g_ref[...]).astype(MXU_DTYPE)
    o_ref[...] = _dot(hb, w_ref[...]).astype(o_ref.dtype)


def _memkv(mem2, g_mem, w_kv, batch):
    n = 2 * X_HEADS * X_HD
    return pl.pallas_call(
        _memkv_kernel,
        out_shape=jax.ShapeDtypeStruct((batch * N_MEM, n), ACT_DTYPE),
        grid=(batch,),
        in_specs=[
            pl.BlockSpec((N_MEM, D_MODEL), lambda b: (b, 0)),
            pl.BlockSpec((1, D_MODEL), lambda b: (0, 0)),
            pl.BlockSpec((D_MODEL, n), lambda b: (0, 0)),
        ],
        out_specs=pl.BlockSpec((N_MEM, n), lambda b: (b, 0)),
        compiler_params=pltpu.CompilerParams(
            dimension_semantics=("arbitrary",),
            vmem_limit_bytes=VMEM_LIMIT),
        name="memkv",
    )(mem2, g_mem, w_kv)


def _merge_kernel(x_ref, yg_ref, yd_ref, xq_ref, xz_ref, sg_ref, sd_ref, sm_ref, mkv_ref,
                  wg_ref, wd_ref, wx_ref, wo_ref, gp_ref, o_ref, ym_ref):
    hw = X_HEADS * X_HD
    heads = range(X_HEADS)

    def cs(h):
        return slice(h * X_HD, (h + 1) * X_HD)

    lgs = [_nt_dot(xq_ref[:, cs(h)], mkv_ref[:, cs(h)]) for h in heads]
    t_g = _dot(yg_ref[...], wg_ref[...])
    ps = []
    for lg in lgs:
        e = jnp.exp(lg - jnp.max(lg, axis=-1, keepdims=True))
        ps.append((e / jnp.sum(e, axis=-1, keepdims=True)).astype(MXU_DTYPE))
    outs = [_dot(ps[h], mkv_ref[:, hw + h * X_HD:hw + (h + 1) * X_HD]) for h in heads]
    t_d = _dot(yd_ref[...], wd_ref[...])
    for h in heads:
        ym_ref[:, cs(h)] = (outs[h] * _silu(xz_ref[:, cs(h)].astype(jnp.float32))).astype(ym_ref.dtype)
    merged = jax.nn.sigmoid(sg_ref[...].astype(jnp.float32)) * t_g
    merged = merged + jax.nn.sigmoid(sd_ref[...].astype(jnp.float32)) * t_d
    merged = merged + jax.nn.sigmoid(sm_ref[...].astype(jnp.float32)) * _dot(ym_ref[...], wx_ref[...])
    t = _dot(merged.astype(MXU_DTYPE), wo_ref[...])
    y = t * lax.rsqrt(jnp.mean(t * t, axis=-1, keepdims=True) + EPS)
    o_ref[...] = x_ref[...] + y * gp_ref[...]


def _merge(x2, y_gla, y_dsa, u, mkv, w_g, w_d, w_x, w_o, g_post, seq, tm):
    m = x2.shape[0]
    steps_per_batch = seq // tm
    d = D_MODEL

    def rmap(cb):
        return lambda i: (i, cb)

    wspec = pl.BlockSpec((d, d), lambda i: (0, 0))
    return pl.pallas_call(
        _merge_kernel,
        out_shape=jax.ShapeDtypeStruct((m, d), jnp.float32),
        grid=(m // tm,),
        in_specs=[
            pl.BlockSpec((tm, d), rmap(0)),
            pl.BlockSpec((tm, d), rmap(0)),
            pl.BlockSpec((tm, d), rmap(0)),
            pl.BlockSpec((tm, d), rmap(COL_XQ // d)),
            pl.BlockSpec((tm, d), rmap(COL_XZ // d)),
            pl.BlockSpec((tm, d), rmap(COL_SG // d)),
            pl.BlockSpec((tm, d), rmap(COL_SD // d)),
            pl.BlockSpec((tm, d), rmap(COL_SM // d)),
            pl.BlockSpec((N_MEM, 2 * X_HEADS * X_HD), lambda i: (i // steps_per_batch, 0)),
            wspec, wspec, wspec, wspec,
            pl.BlockSpec((1, d), lambda i: (0, 0)),
        ],
        out_specs=pl.BlockSpec((tm, d), rmap(0)),
        scratch_shapes=[pltpu.VMEM((tm, d), MXU_DTYPE)],
        compiler_params=pltpu.CompilerParams(
            dimension_semantics=("arbitrary",),
            vmem_limit_bytes=VMEM_LIMIT),
        name="merge",
    )(x2, y_gla, y_dsa, u, u, u, u, u, mkv, w_g, w_d, w_x, w_o, g_post)


_MAIN_GROUPS = (0, 1, 2, 4, 5, 11, 12, 13, 14, 6, 7, 8)
_GROUP_GA, _GROUP_IK, _GROUP_IW = 3, 9, 10
RELAYOUT_COLS = 128


def _relayout_kernel(wt_ref, main_ref, small_ref):
    offs = np.concatenate([[0], np.cumsum(np.array(SPLIT_SIZES))]).tolist()

    def group(g):
        return wt_ref[offs[g]:offs[g + 1], :].astype(MXU_DTYPE)

    dst = 0
    for g in _MAIN_GROUPS:
        main_ref[dst:dst + SPLIT_SIZES[g], :] = group(g)
        dst += SPLIT_SIZES[g]
    pad = jnp.zeros((LANES - GLA_RANK - IDX_HEADS, wt_ref.shape[1]), MXU_DTYPE)
    small_ref[...] = jnp.concatenate(
        [group(_GROUP_IK), group(_GROUP_IK), group(_GROUP_GA), group(_GROUP_IW), pad], axis=0)


def _relayout_w_in(w_in):
    wt = w_in.T
    n, d = wt.shape
    return pl.pallas_call(
        _relayout_kernel,
        out_shape=(jax.ShapeDtypeStruct((U_COLS, d), MXU_DTYPE),
                   jax.ShapeDtypeStruct((2 * LANES, d), MXU_DTYPE)),
        grid=(d // RELAYOUT_COLS,),
        in_specs=[pl.BlockSpec((n, RELAYOUT_COLS), lambda i: (0, i))],
        out_specs=(pl.BlockSpec((U_COLS, RELAYOUT_COLS), lambda i: (0, i)),
                   pl.BlockSpec((2 * LANES, RELAYOUT_COLS), lambda i: (0, i))),
        compiler_params=pltpu.CompilerParams(
            dimension_semantics=("arbitrary",),
            vmem_limit_bytes=VMEM_LIMIT),
        name="relayout",
    )(wt)


def _col_scale():
    s = np.ones((1, U_COLS), np.float32)
    s[:, COL_GQ:COL_GQ + GLA_HEADS * GLA_DK] = GLA_DK ** -0.5
    s[:, COL_DQ:COL_DQ + DSA_HEADS * DSA_HD] = DSA_HD ** -0.5 * LOG2_E
    s[:, COL_XQ:COL_XQ + X_HEADS * X_HD] = X_HD ** -0.5
    return jnp.asarray(s)


def _layer(x2, mem2, g_pre, g_post, g_mem, w_in, w_up, b_a, g_gla, rel_bias, w_kv,
           w_g, w_d, w_x, w_o, batch, seq):
    w_main, w_small = _relayout_w_in(w_in)
    tm = min(PROJ_TM, batch * seq)
    u, ikd, small = _proj(x2, g_pre.reshape(1, -1), w_main, w_small, _col_scale(), tm, PROJ_TN)
    y_gla = _gla(u, small, w_up.astype(MXU_DTYPE), b_a.reshape(1, -1), g_gla.reshape(1, -1),
                 batch, seq, chunks_per_step=4)
    y_dsa = _dsa(u, ikd, small, rel_bias, batch, seq)
    mkv = _memkv(mem2, g_mem.reshape(1, -1), w_kv.astype(MXU_DTYPE), batch)
    return _merge(x2, y_gla, y_dsa, u, mkv, w_g.astype(MXU_DTYPE), w_d.astype(MXU_DTYPE),
                  w_x.astype(MXU_DTYPE), w_o.astype(MXU_DTYPE), g_post.reshape(1, -1), seq, MERGE_TM)


def kernel(x, mem, g_pre, g_post, g_mem, w_in, w_gla_a_up, b_gla_a, g_gla, rel_bias,
           w_mem_kv, w_gla_out, w_dsa_out, w_x_out, w_o):
    batch, seq, d = x.shape
    x2 = x.reshape(batch * seq, d)
    mem2 = mem.reshape(batch * N_MEM, d)
    for i in range(g_pre.shape[0]):
        x2 = _layer(x2, mem2, g_pre[i], g_post[i], g_mem[i], w_in[i], w_gla_a_up[i],
                    b_gla_a[i], g_gla[i], rel_bias, w_mem_kv[i], w_gla_out[i],
                    w_dsa_out[i], w_x_out[i], w_o[i], batch, seq)
    return x2.reshape(batch, seq, d)
```

```python
import functools
import math

import jax
import jax.numpy as jnp
import numpy as np
from jax import lax
from jax.experimental import pallas as pl
from jax.experimental.pallas import tpu as pltpu

D_MODEL = 1024
N_MEM = 256
EPS = 1e-6
GLA_HEADS = 4
GLA_DK = 128
GLA_DV = 256
GLA_RANK = 16
GLA_GATE_NORM = 16.0
GLA_CHUNK = 64
DSA_HEADS = 8
DSA_KV_HEADS = 2
DSA_GROUPS = DSA_HEADS // DSA_KV_HEADS
DSA_HD = 128
IDX_HEADS = 8
IDX_DIM = 64
TOPK_MAX = 256
Q_BLOCK = 128
REL_BUCKETS = 32
REL_MAX_DIST = 128
X_HEADS = 4
X_HD = 256

SPLIT_SIZES = (512, 512, 1024, 16, 1024, 1024, 256, 256, 512, 64, 8, 1024, 1024, 1024, 3072)

LANES = 128
SUBLANES = 8

MXU_DTYPE = jnp.bfloat16
ACT_DTYPE = jnp.bfloat16

U_COLS = 11264
COL_GQ, COL_GK, COL_GV, COL_GZ = 0, 512, 1024, 2048
COL_DQ, COL_DZ, COL_XQ, COL_XZ = 3072, 4096, 5120, 6144
COL_SG, COL_SD, COL_SM = 7168, 8192, 9216
COL_DK, COL_DV, COL_IQ = 10240, 10496, 10752
SMALL_GA, SMALL_IW = 0, 16

MASK_NEG = -1e30
LOG2_E = math.log2(math.e)
INT_MIN = -(2 ** 31)
VMEM_LIMIT = 56 * 1024 * 1024

PROJ_TM = 1024
PROJ_TN = U_COLS // 4
MERGE_TM = 512


def _nt_dot(a, b):
    return lax.dot_general(a, b, (((1,), (1,)), ((), ())),
                           preferred_element_type=jnp.float32)


def _tn_dot(a, b):
    return lax.dot_general(a, b, (((0,), (0,)), ((), ())),
                           preferred_element_type=jnp.float32)


def _dot(a, b):
    return jnp.dot(a, b, preferred_element_type=jnp.float32)


def _silu(z):
    return z * jax.nn.sigmoid(z)


def _proj_kernel(x_ref, g_ref, w_ref, ws_ref, cs_ref, u_ref, ikd_ref, sm_ref, h_ref):
    @pl.when(pl.program_id(1) == 0)
    def _():
        xf = x_ref[...]
        y = xf * lax.rsqrt(jnp.mean(xf * xf, axis=-1, keepdims=True) + EPS)
        hb = (y * g_ref[...]).astype(MXU_DTYPE)
        h_ref[...] = hb
        r = _nt_dot(hb, ws_ref[...])
        ikd_ref[...] = r[:, :LANES].astype(ikd_ref.dtype)
        sm_ref[...] = r[:, LANES:]

    acc = _nt_dot(h_ref[...], w_ref[...])
    u_ref[...] = (acc * cs_ref[...]).astype(u_ref.dtype)


def _proj(x2, g_pre, w_main, w_small, col_scale, tm, tn):
    m = x2.shape[0]
    grid = (m // tm, U_COLS // tn)
    return pl.pallas_call(
        _proj_kernel,
        out_shape=(jax.ShapeDtypeStruct((m, U_COLS), ACT_DTYPE),
                   jax.ShapeDtypeStruct((m, LANES), ACT_DTYPE),
                   jax.ShapeDtypeStruct((m, LANES), jnp.float32)),
        grid=grid,
        in_specs=[
            pl.BlockSpec((tm, D_MODEL), lambda i, j: (i, 0)),
            pl.BlockSpec((1, D_MODEL), lambda i, j: (0, 0)),
            pl.BlockSpec((tn, D_MODEL), lambda i, j: (j, 0)),
            pl.BlockSpec((2 * LANES, D_MODEL), lambda i, j: (0, 0)),
            pl.BlockSpec((1, tn), lambda i, j: (0, j)),
        ],
        out_specs=(
            pl.BlockSpec((tm, tn), lambda i, j: (i, j)),
            pl.BlockSpec((tm, LANES), lambda i, j: (i, 0)),
            pl.BlockSpec((tm, LANES), lambda i, j: (i, 0)),
        ),
        scratch_shapes=[pltpu.VMEM((tm, D_MODEL), MXU_DTYPE)],
        compiler_params=pltpu.CompilerParams(
            dimension_semantics=("arbitrary", "arbitrary"),
            vmem_limit_bytes=VMEM_LIMIT),
        name="proj",
    )(x2, g_pre, w_main, w_small, col_scale)


def _split3(x):
    hi = x.astype(MXU_DTYPE)
    r1 = x - hi.astype(jnp.float32)
    mid = r1.astype(MXU_DTYPE)
    lo = (r1 - mid.astype(jnp.float32)).astype(MXU_DTYPE)
    return hi, mid, lo


def _gla_kernel(q_ref, k_ref, v_ref, z_ref, sm_ref, wup_ref, ba_ref, gg_ref,
                o_ref, st_ref, qd_ref, kd_ref, kt_ref, b_ref, oacc_ref, *, chunks_per_step):
    c = GLA_CHUNK
    rows = chunks_per_step * c
    heads = range(GLA_HEADS)

    @pl.when(pl.program_id(1) == 0)
    def _():
        st_ref[...] = jnp.zeros_like(st_ref)

    def ks(h):
        return slice(h * GLA_DK, (h + 1) * GLA_DK)

    def vs(h):
        return slice(h * GLA_DV, (h + 1) * GLA_DV)

    def chunk(i):
        return slice(i * c, (i + 1) * c)

    row = lax.broadcasted_iota(jnp.int32, (rows, rows), 0)
    col = lax.broadcasted_iota(jnp.int32, (rows, rows), 1)
    causal = (row // c == col // c) & (col <= row)
    tril = jnp.where(causal, 1.0, 0.0).astype(MXU_DTYPE)

    ga = sm_ref[:, SMALL_GA:SMALL_GA + GLA_RANK].astype(MXU_DTYPE)
    pre = _dot(ga, wup_ref[...]) + ba_ref[...]
    log_a = (jnp.minimum(pre, 0.0) - jnp.log1p(jnp.exp(-jnp.abs(pre)))) / GLA_GATE_NORM
    hi, mid, lo = _split3(log_a)
    b_ref[...] = _dot(tril, hi) + _dot(tril, mid) + _dot(tril, lo)

    decay = []
    for i in range(chunks_per_step):
        b_last = b_ref[i * c + c - 1:i * c + c, :]
        decay.append(jnp.exp(b_last))
        for h in heads:
            b = b_ref[chunk(i), ks(h)]
            q = q_ref[chunk(i), ks(h)].astype(jnp.float32)
            k = k_ref[chunk(i), ks(h)].astype(jnp.float32)
            qd_ref[chunk(i), ks(h)] = (q * jnp.exp(b)).astype(MXU_DTYPE)
            kd_ref[chunk(i), ks(h)] = (k * jnp.exp(-b)).astype(MXU_DTYPE)
            kt_ref[chunk(i), ks(h)] = (k * jnp.exp(b_last[:, ks(h)] - b)).astype(MXU_DTYPE)

    att = [_nt_dot(qd_ref[:, ks(h)], kd_ref[:, ks(h)]) for h in heads]
    att = [jnp.where(causal, a, 0.0).astype(MXU_DTYPE) for a in att]
    for h in heads:
        oacc_ref[:, vs(h)] = _dot(att[h], v_ref[:, vs(h)])

    st = [st_ref[h] for h in heads]
    for i in range(chunks_per_step):
        for h in heads:
            kv = _tn_dot(v_ref[chunk(i), vs(h)], kt_ref[chunk(i), ks(h)])
            oacc_ref[chunk(i), vs(h)] += _nt_dot(qd_ref[chunk(i), ks(h)], st[h].astype(MXU_DTYPE))
            st[h] = st[h] * decay[i][:, ks(h)] + kv
    for h in heads:
        st_ref[h] = st[h]
        oh = oacc_ref[:, vs(h)]
        y = oh * lax.rsqrt(jnp.mean(oh * oh, axis=-1, keepdims=True) + EPS)
        y = y * gg_ref[...]
        zz = z_ref[:, vs(h)].astype(jnp.float32)
        o_ref[:, vs(h)] = (y * _silu(zz)).astype(o_ref.dtype)


def _gla(u, small, w_up, b_a, g_gla, batch, seq, chunks_per_step):
    rows = chunks_per_step * GLA_CHUNK
    steps = seq // rows
    hk = GLA_HEADS * GLA_DK
    hv = GLA_HEADS * GLA_DV

    def rmap(cb):
        return lambda b, s: (b * steps + s, cb)

    return pl.pallas_call(
        functools.partial(_gla_kernel, chunks_per_step=chunks_per_step),
        out_shape=jax.ShapeDtypeStruct((batch * seq, hv), ACT_DTYPE),
        grid=(batch, steps),
        in_specs=[
            pl.BlockSpec((rows, hk), rmap(COL_GQ // hk)),
            pl.BlockSpec((rows, hk), rmap(COL_GK // hk)),
            pl.BlockSpec((rows, hv), rmap(COL_GV // hv)),
            pl.BlockSpec((rows, hv), rmap(COL_GZ // hv)),
            pl.BlockSpec((rows, LANES), rmap(0)),
            pl.BlockSpec((GLA_RANK, hk), lambda b, s: (0, 0)),
            pl.BlockSpec((1, hk), lambda b, s: (0, 0)),
            pl.BlockSpec((1, GLA_DV), lambda b, s: (0, 0)),
        ],
        out_specs=pl.BlockSpec((rows, hv), rmap(0)),
        scratch_shapes=[pltpu.VMEM((GLA_HEADS, GLA_DV, GLA_DK), jnp.float32),
                        pltpu.VMEM((rows, hk), MXU_DTYPE),
                        pltpu.VMEM((rows, hk), MXU_DTYPE),
                        pltpu.VMEM((rows, hk), MXU_DTYPE),
                        pltpu.VMEM((rows, hk), jnp.float32),
                        pltpu.VMEM((rows, hv), jnp.float32)],
        compiler_params=pltpu.CompilerParams(
            dimension_semantics=("arbitrary", "arbitrary"),
            vmem_limit_bytes=VMEM_LIMIT),
        name="gla",
    )(u, u, u, u, small, w_up, b_a, g_gla)


KEY_CHUNK = 4 * Q_BLOCK
BLOCKS_PER_CHUNK = KEY_CHUNK // Q_BLOCK
PLANE_GROUP = 32 * SUBLANES


def _t5_bucket(dist):
    max_exact = REL_BUCKETS // 2
    d = jnp.maximum(dist, 1).astype(jnp.float32)
    large = max_exact + jnp.floor(jnp.log(d / max_exact) / math.log(REL_MAX_DIST / max_exact)
                                  * (REL_BUCKETS - max_exact)).astype(jnp.int32)
    large = jnp.minimum(large, REL_BUCKETS - 1)
    return jnp.where(dist < max_exact, dist, large)


def _dsa_kernel(rb_ref, dq_ref, dk_ref, dv_ref, iq_ref, ikd_ref, sm_ref, dz_ref,
                out_ref,
                kt_ref, sc_ref, planes_ref, mb_ref, vt_ref, bt_ref, iqm_ref, qaug_ref, acc_ref,
                lga_ref, lgb_ref, mxa_ref, mxb_ref, ml_ref,
                *, seq):
    qb = pl.program_id(1)
    blk = Q_BLOCK
    ch = KEY_CHUNK
    k_sel = min(TOPK_MAX, seq // 4)
    n_chunks = qb // BLOCKS_PER_CHUNK + 1
    width = DSA_GROUPS * blk

    row_i = lax.broadcasted_iota(jnp.int32, (blk, blk), 0)
    col_i = lax.broadcasted_iota(jnp.int32, (blk, blk), 1)
    crow_i = lax.broadcasted_iota(jnp.int32, (ch, blk), 0)
    q_pos = qb * blk + lax.broadcasted_iota(jnp.int32, (ch, blk), 1)

    @pl.when((pl.program_id(0) == 0) & (qb == 0))
    def _():
        planes_ref[...] = jnp.zeros_like(planes_ref)
        kt_ref[...] = jnp.zeros_like(kt_ref)
        for delta in range(3):
            dist = jnp.maximum(delta * blk + col_i - row_i, 0)
            bucket = _t5_bucket(dist)
            for h in range(DSA_HEADS):
                tile = jnp.zeros((blk, blk), jnp.float32)
                for bk in range(REL_BUCKETS):
                    tile = jnp.where(bucket == bk, rb_ref[bk, h] * LOG2_E, tile)
                cc, g = divmod(h, DSA_GROUPS)
                bt_ref[delta, cc, :, g * blk:(g + 1) * blk] = tile

    @pl.when(qb == 0)
    def _():
        def body(i, carry):
            r = pl.multiple_of(i * ch, ch)
            vt_ref[i] = dv_ref[pl.ds(r, ch), :].astype(jnp.float32).T.astype(vt_ref.dtype)
            return carry
        lax.fori_loop(0, seq // ch, body, 0)

    w_t = sm_ref[...].T[SMALL_IW:SMALL_IW + IDX_HEADS, :]
    w_t = w_t * (IDX_HEADS ** -0.5) * (IDX_DIM ** -0.5)

    lane = lax.broadcasted_iota(jnp.int32, (blk, LANES), 1)
    for h in range(IDX_HEADS):
        pair = iq_ref[:, (h // 2) * LANES:(h // 2 + 1) * LANES]
        keep = (lane < IDX_DIM) if h % 2 == 0 else (lane >= IDX_DIM)
        iqm_ref[h // 2, (h % 2) * blk:(h % 2 + 1) * blk, :] = jnp.where(keep, pair, jnp.zeros_like(pair))

    def score_chunk(c, causal):
        r = pl.multiple_of(c * ch, ch)
        ikc = ikd_ref[pl.ds(r, ch), :]
        acc = None
        for hp in range(IDX_HEADS // 2):
            s2 = _nt_dot(ikc, iqm_ref[hp])
            for h in (2 * hp, 2 * hp + 1):
                t = w_t[h:h + 1, :] * jnp.maximum(s2[:, (h % 2) * blk:(h % 2 + 1) * blk], 0.0)
                acc = t if acc is None else acc + t
        if causal:
            acc = jnp.where(r + crow_i <= q_pos, acc, -jnp.inf)
        sc_ref[pl.ds(r, ch), :] = acc
        bits = lax.bitcast_convert_type(acc, jnp.int32)
        kt_ref[pl.ds(r, ch), :] = bits ^ ((bits >> 31) & 0x7FFFFFFF)

    def build_planes(c):
        for g in range(ch // PLANE_GROUP):
            base = c * ch + g * PLANE_GROUP
            a = [kt_ref[pl.ds(pl.multiple_of(base + SUBLANES * v, SUBLANES), SUBLANES), :]
                 for v in range(32)]
            j, m = 16, 0x0000FFFF
            while j:
                k0 = 0
                while k0 < 32:
                    t = (a[k0] ^ lax.shift_right_logical(a[k0 + j], jnp.int32(j))) & m
                    a[k0] = a[k0] ^ t
                    a[k0 + j] = a[k0 + j] ^ (t << j)
                    k0 = (k0 + j + 1) & ~j
                j >>= 1
                m = (m ^ (m << j)) & 0xFFFFFFFF if j else m
                m = m - (1 << 32) if m >= (1 << 31) else m
            a[0] = ~a[0]
            row = pl.multiple_of(c * (ch // 32) + g * SUBLANES, SUBLANES)
            for jj in range(32):
                planes_ref[jj, pl.ds(row, SUBLANES), :] = a[jj]

    @pl.when(n_chunks > 1)
    def _():
        score_chunk(0, False)

    def score_body(c, carry):
        build_planes(c - 1)
        score_chunk(c, False)
        return carry
    lax.fori_loop(1, n_chunks - 1, score_body, 0)
    build_planes(jnp.maximum(n_chunks - 2, 0))
    score_chunk(n_chunks - 1, True)
    build_planes(n_chunks - 1)

    n_rows = seq // 32

    def rowsum(x):
        part = jnp.sum(x.reshape(n_rows // SUBLANES, SUBLANES, blk), axis=0)
        return jnp.sum(part, axis=0, keepdims=True)

    prow = lax.broadcasted_iota(jnp.int32, (n_rows, blk), 0)
    alive0 = jnp.where(prow < n_chunks * (ch // 32), -1, 0).astype(jnp.int32)

    def bit_body(j, carry):
        alive, cnt_gt, ukey = carry
        w = planes_ref[j]
        ones = alive & w
        c1 = rowsum(lax.population_count(ones))
        take = cnt_gt + c1 >= k_sel
        alive = jnp.where(take, ones, alive & ~w)
        cnt_gt = jnp.where(take, cnt_gt, cnt_gt + c1)
        ukey = jnp.where(take, ukey | (jnp.int32(1) << (31 - j)), ukey)
        return alive, cnt_gt, ukey

    zero_row = jnp.zeros((1, blk), jnp.int32)
    alive, cnt_gt, ukey = lax.fori_loop(0, 32, bit_body, (alive0, zero_row, zero_row))
    ans = ukey ^ INT_MIN
    thr_bits = jnp.where(ans < 0, ans ^ 0x7FFFFFFF, ans)
    thr0 = lax.bitcast_convert_type(thr_bits, jnp.float32)

    def tile_sum(x):
        return jnp.sum(x.reshape(ch // SUBLANES, SUBLANES, blk), axis=0)

    def fold_rows(acc, combine):
        return functools.reduce(combine, [acc[i:i + 1] for i in range(SUBLANES)])

    def mask_pass(thr, p_max):
        def body(c, carry):
            a_gt, a_ge = carry
            r = pl.multiple_of(c * ch, ch)
            sc = sc_ref[pl.ds(r, ch), :]
            k_pos = r + crow_i
            gt = sc > thr
            ge = sc >= thr
            sel = (gt | (ge & (k_pos <= p_max))) & (k_pos <= q_pos)
            mb_ref[pl.ds(r, ch), :] = jnp.where(sel, 0.0, MASK_NEG).astype(mb_ref.dtype)
            return (a_gt + tile_sum(jnp.where(gt, 1, 0).astype(jnp.int32)),
                    a_ge + tile_sum(jnp.where(ge, 1, 0).astype(jnp.int32)))
        zero = jnp.zeros((SUBLANES, blk), jnp.int32)
        a_gt, a_ge = lax.fori_loop(0, n_chunks, body, (zero, zero))
        return fold_rows(a_gt, jnp.add), fold_rows(a_ge, jnp.add)

    def nearest(pred_fn, fill, combine):
        def body(c, acc):
            r = pl.multiple_of(c * ch, ch)
            sc = sc_ref[pl.ds(r, ch), :]
            part = jnp.where(pred_fn(sc), sc, fill).reshape(ch // SUBLANES, SUBLANES, blk)
            return combine(acc, functools.reduce(combine, [part[i] for i in range(ch // SUBLANES)]))
        acc = lax.fori_loop(0, n_chunks, body, jnp.full((SUBLANES, blk), fill, jnp.float32))
        return fold_rows(acc, combine)

    def off_target(state):
        _, c_gt, c_ge = state
        return jnp.max(jnp.where((c_gt >= k_sel) | (c_ge < k_sel), 1, 0)) > 0

    def walk(state):
        thr, c_gt, c_ge = state
        above = nearest(lambda sc: sc > thr, jnp.inf, jnp.minimum)
        below = nearest(lambda sc: sc < thr, -jnp.inf, jnp.maximum)
        thr = jnp.where(c_gt >= k_sel, above, jnp.where(c_ge < k_sel, below, thr))
        return (thr,) + mask_pass(thr, no_bound)

    no_bound = jnp.full((1, blk), 2 * seq, jnp.int32)
    thr, cnt_gt, cnt_ge = lax.while_loop(off_target, walk, (thr0,) + mask_pass(thr0, no_bound))

    @pl.when(jnp.max(cnt_ge) > k_sel)
    def _():
        need = k_sel - cnt_gt
        n_bits = (2 * seq - 1).bit_length()

        def tie_count(cand):
            def body(c, acc):
                r = pl.multiple_of(c * ch, ch)
                hit = (sc_ref[pl.ds(r, ch), :] == thr) & (r + crow_i < cand)
                return acc + tile_sum(jnp.where(hit, 1, 0).astype(jnp.int32))
            return fold_rows(lax.fori_loop(0, n_chunks, body, jnp.zeros((SUBLANES, blk), jnp.int32)), jnp.add)

        def pos_body(i, p):
            cand = p | (jnp.int32(1) << (n_bits - 1 - i))
            return jnp.where(tie_count(cand) <= need - 1, cand, p)
        p_max = lax.fori_loop(0, n_bits, pos_body, jnp.zeros((1, blk), jnp.int32))
        mask_pass(thr, p_max)

    eye = jnp.where(row_i == col_i, 1.0, 0.0).astype(MXU_DTYPE)
    for cc in range(DSA_KV_HEADS):
        for g in range(DSA_GROUPS):
            h = cc * DSA_GROUPS + g
            qaug_ref[cc, g * blk:(g + 1) * blk, 0:DSA_HD] = dq_ref[:, h * DSA_HD:(h + 1) * DSA_HD]
            qaug_ref[cc, g * blk:(g + 1) * blk, DSA_HD:2 * DSA_HD] = eye
    acc_ref[...] = jnp.zeros_like(acc_ref)
    for cc in range(DSA_KV_HEADS):
        ml_ref[cc, 0] = jnp.full(ml_ref.shape[2:], -jnp.inf, jnp.float32)
        ml_ref[cc, 1] = jnp.zeros(ml_ref.shape[2:], jnp.float32)

    def stage_logits(c, lg_ref, mx_ref, far):
        r = pl.multiple_of(c * ch, ch)
        mbc = mb_ref[pl.ds(r, ch), :]
        for cc in range(DSA_KV_HEADS):
            kaug = jnp.concatenate([dk_ref[pl.ds(r, ch), cc * DSA_HD:(cc + 1) * DSA_HD], mbc], axis=1)
            lg = _nt_dot(kaug, qaug_ref[cc])
            if far:
                off = bt_ref[2, cc, 0:1, :]
            else:
                lg = lg + jnp.concatenate(
                    [bt_ref[jnp.clip(qb - (c * BLOCKS_PER_CHUNK + j), 0, 2), cc]
                     for j in range(BLOCKS_PER_CHUNK)], axis=0)
                off = jnp.zeros((1, width), jnp.float32)
            lg_ref[cc] = lg
            mx_ref[cc, 0] = jnp.broadcast_to(jnp.max(lg, axis=0, keepdims=True) + off, mx_ref.shape[2:])
            mx_ref[cc, 1] = jnp.broadcast_to(off, mx_ref.shape[2:])

    def stage_softmax(c, lg_ref, mx_ref):
        for cc in range(DSA_KV_HEADS):
            m = ml_ref[cc, 0, 0:1, :]
            l = ml_ref[cc, 1, 0:1, :]
            m_new = jnp.maximum(m, mx_ref[cc, 0, 0:1, :])
            alpha = jnp.exp2(m - m_new)
            p = jnp.exp2(lg_ref[cc] - (m_new - mx_ref[cc, 1, 0:1, :]))
            l_new = alpha * l + jnp.sum(p, axis=0, keepdims=True)
            pv = _dot(vt_ref[c, cc * DSA_HD:(cc + 1) * DSA_HD, :], p.astype(MXU_DTYPE))
            acc_ref[cc] = acc_ref[cc] * alpha + pv
            ml_ref[cc, 0] = jnp.broadcast_to(m_new, ml_ref.shape[2:])
            ml_ref[cc, 1] = jnp.broadcast_to(l_new, ml_ref.shape[2:])

    stage_logits(0, lga_ref, mxa_ref, far=False)

    def pair_body(pi, carry, far):
        c0 = 2 * pi
        stage_logits(c0 + 1, lgb_ref, mxb_ref, far)
        stage_softmax(c0, lga_ref, mxa_ref)

        stage_logits(jnp.minimum(c0 + 2, n_chunks - 1), lga_ref, mxa_ref, far)
        stage_softmax(c0 + 1, lgb_ref, mxb_ref)
        return carry

    c_near = jnp.maximum(qb - 1, 0) // BLOCKS_PER_CHUNK
    far_pairs = jnp.maximum(c_near - 1, 0) // 2
    lax.fori_loop(0, far_pairs, functools.partial(pair_body, far=True), 0)
    lax.fori_loop(far_pairs, n_chunks // 2, functools.partial(pair_body, far=False), 0)

    @pl.when(n_chunks % 2 == 1)
    def _():
        stage_softmax(n_chunks - 1, lga_ref, mxa_ref)

    for cc in range(DSA_KV_HEADS):
        o_t = acc_ref[cc] / ml_ref[cc, 1, 0:1, :]
        for g in range(DSA_GROUPS):
            h = cc * DSA_GROUPS + g
            o = o_t[:, g * blk:(g + 1) * blk].T
            zz = dz_ref[:, h * DSA_HD:(h + 1) * DSA_HD].astype(jnp.float32)
            out_ref[:, h * DSA_HD:(h + 1) * DSA_HD] = (o * _silu(zz)).astype(out_ref.dtype)


def _dsa(u, ikd, small, rel_bias, batch, seq):
    nb = seq // Q_BLOCK
    hq = DSA_HEADS * DSA_HD
    hkv = DSA_KV_HEADS * DSA_HD
    hi = IDX_HEADS * IDX_DIM

    def qmap(cb):
        return lambda b, q: (b * nb + q, cb)

    def bmap(cb):
        return lambda b, q: (b, cb)

    return pl.pallas_call(
        functools.partial(_dsa_kernel, seq=seq),
        out_shape=jax.ShapeDtypeStruct((batch * seq, hq), ACT_DTYPE),
        grid=(batch, nb),
        in_specs=[
            pl.BlockSpec(memory_space=pltpu.SMEM),
            pl.BlockSpec((Q_BLOCK, hq), qmap(COL_DQ // hq)),
            pl.BlockSpec((seq, hkv), bmap(COL_DK // hkv)),
            pl.BlockSpec((seq, hkv), bmap(COL_DV // hkv)),
            pl.BlockSpec((Q_BLOCK, hi), qmap(COL_IQ // hi)),
            pl.BlockSpec((seq, LANES), bmap(0)),
            pl.BlockSpec((Q_BLOCK, LANES), qmap(0)),
            pl.BlockSpec((Q_BLOCK, hq), qmap(COL_DZ // hq)),
        ],
        out_specs=pl.BlockSpec((Q_BLOCK, hq), qmap(0)),
        scratch_shapes=[
            pltpu.VMEM((seq, Q_BLOCK), jnp.int32),
            pltpu.VMEM((seq, Q_BLOCK), jnp.float32),
            pltpu.VMEM((32, seq // 32, Q_BLOCK), jnp.int32),
            pltpu.VMEM((seq, Q_BLOCK), MXU_DTYPE),
            pltpu.VMEM((seq // KEY_CHUNK, hkv, KEY_CHUNK), MXU_DTYPE),
            pltpu.VMEM((3, DSA_KV_HEADS, Q_BLOCK, DSA_GROUPS * Q_BLOCK), jnp.float32),
            pltpu.VMEM((IDX_HEADS // 2, 2 * Q_BLOCK, LANES), MXU_DTYPE),
            pltpu.VMEM((DSA_KV_HEADS, DSA_GROUPS * Q_BLOCK, 2 * DSA_HD), MXU_DTYPE),
            pltpu.VMEM((DSA_KV_HEADS, DSA_HD, DSA_GROUPS * Q_BLOCK), jnp.float32),
            pltpu.VMEM((DSA_KV_HEADS, KEY_CHUNK, DSA_GROUPS * Q_BLOCK), jnp.float32),
            pltpu.VMEM((DSA_KV_HEADS, KEY_CHUNK, DSA_GROUPS * Q_BLOCK), jnp.float32),
            pltpu.VMEM((DSA_KV_HEADS, 2, SUBLANES, DSA_GROUPS * Q_BLOCK), jnp.float32),
            pltpu.VMEM((DSA_KV_HEADS, 2, SUBLANES, DSA_GROUPS * Q_BLOCK), jnp.float32),
            pltpu.VMEM((DSA_KV_HEADS, 2, SUBLANES, DSA_GROUPS * Q_BLOCK), jnp.float32),
        ],
        compiler_params=pltpu.CompilerParams(
            dimension_semantics=("arbitrary", "arbitrary"),
            vmem_limit_bytes=VMEM_LIMIT),
        name="dsa",
    )(rel_bias, u, u, u, u, ikd, small, u)


def _memkv_kernel(mem_ref, g_ref, w_ref, o_ref):
    xf = mem_ref[...]
    y = xf * lax.rsqrt(jnp.mean(xf * xf, axis=-1, keepdims=True) + EPS)
    hb = (y * g_ref[...]).astype(MXU_DTYPE)
    o_ref[...] = _dot(hb, w_ref[...]).astype(o_ref.dtype)


def _memkv(mem2, g_mem, w_kv, batch):
    n = 2 * X_HEADS * X_HD
    return pl.pallas_call(
        _memkv_kernel,
        out_shape=jax.ShapeDtypeStruct((batch * N_MEM, n), ACT_DTYPE),
        grid=(batch,),
        in_specs=[
            pl.BlockSpec((N_MEM, D_MODEL), lambda b: (b, 0)),
            pl.BlockSpec((1, D_MODEL), lambda b: (0, 0)),
            pl.BlockSpec((D_MODEL, n), lambda b: (0, 0)),
        ],
        out_specs=pl.BlockSpec((N_MEM, n), lambda b: (b, 0)),
        compiler_params=pltpu.CompilerParams(
            dimension_semantics=("arbitrary",),
            vmem_limit_bytes=VMEM_LIMIT),
        name="memkv",
    )(mem2, g_mem, w_kv)


def _merge_kernel(x_ref, yg_ref, yd_ref, xq_ref, xz_ref, sg_ref, sd_ref, sm_ref, mkv_ref,
                  wg_ref, wd_ref, wx_ref, wo_ref, gp_ref, o_ref, ym_ref):
    hw = X_HEADS * X_HD
    heads = range(X_HEADS)

    def cs(h):
        return slice(h * X_HD, (h + 1) * X_HD)

    lgs = [_nt_dot(xq_ref[:, cs(h)], mkv_ref[:, cs(h)]) for h in heads]
    t_g = _dot(yg_ref[...], wg_ref[...])
    ps = []
    for lg in lgs:
        e = jnp.exp(lg - jnp.max(lg, axis=-1, keepdims=True))
        ps.append((e / jnp.sum(e, axis=-1, keepdims=True)).astype(MXU_DTYPE))
    outs = [_dot(ps[h], mkv_ref[:, hw + h * X_HD:hw + (h + 1) * X_HD]) for h in heads]
    t_d = _dot(yd_ref[...], wd_ref[...])
    for h in heads:
        ym_ref[:, cs(h)] = (outs[h] * _silu(xz_ref[:, cs(h)].astype(jnp.float32))).astype(ym_ref.dtype)
    merged = jax.nn.sigmoid(sg_ref[...].astype(jnp.float32)) * t_g
    merged = merged + jax.nn.sigmoid(sd_ref[...].astype(jnp.float32)) * t_d
    merged = merged + jax.nn.sigmoid(sm_ref[...].astype(jnp.float32)) * _dot(ym_ref[...], wx_ref[...])
    t = _dot(merged.astype(MXU_DTYPE), wo_ref[...])
    y = t * lax.rsqrt(jnp.mean(t * t, axis=-1, keepdims=True) + EPS)
    o_ref[...] = x_ref[...] + y * gp_ref[...]


def _merge(x2, y_gla, y_dsa, u, mkv, w_g, w_d, w_x, w_o, g_post, seq, tm):
    m = x2.shape[0]
    steps_per_batch = seq // tm
    d = D_MODEL

    def rmap(cb):
        return lambda i: (i, cb)

    wspec = pl.BlockSpec((d, d), lambda i: (0, 0))
    return pl.pallas_call(
        _merge_kernel,
        out_shape=jax.ShapeDtypeStruct((m, d), jnp.float32),
        grid=(m // tm,),
        in_specs=[
            pl.BlockSpec((tm, d), rmap(0)),
            pl.BlockSpec((tm, d), rmap(0)),
            pl.BlockSpec((tm, d), rmap(0)),
            pl.BlockSpec((tm, d), rmap(COL_XQ // d)),
            pl.BlockSpec((tm, d), rmap(COL_XZ // d)),
            pl.BlockSpec((tm, d), rmap(COL_SG // d)),
            pl.BlockSpec((tm, d), rmap(COL_SD // d)),
            pl.BlockSpec((tm, d), rmap(COL_SM // d)),
            pl.BlockSpec((N_MEM, 2 * X_HEADS * X_HD), lambda i: (i // steps_per_batch, 0)),
            wspec, wspec, wspec, wspec,
            pl.BlockSpec((1, d), lambda i: (0, 0)),
        ],
        out_specs=pl.BlockSpec((tm, d), rmap(0)),
        scratch_shapes=[pltpu.VMEM((tm, d), MXU_DTYPE)],
        compiler_params=pltpu.CompilerParams(
            dimension_semantics=("arbitrary",),
            vmem_limit_bytes=VMEM_LIMIT),
        name="merge",
    )(x2, y_gla, y_dsa, u, u, u, u, u, mkv, w_g, w_d, w_x, w_o, g_post)


_MAIN_GROUPS = (0, 1, 2, 4, 5, 11, 12, 13, 14, 6, 7, 8)
_GROUP_GA, _GROUP_IK, _GROUP_IW = 3, 9, 10
RELAYOUT_COLS = 128


def _relayout_kernel(wt_ref, main_ref, small_ref):
    offs = np.concatenate([[0], np.cumsum(np.array(SPLIT_SIZES))]).tolist()

    def group(g):
        return wt_ref[offs[g]:offs[g + 1], :].astype(MXU_DTYPE)

    dst = 0
    for g in _MAIN_GROUPS:
        main_ref[dst:dst + SPLIT_SIZES[g], :] = group(g)
        dst += SPLIT_SIZES[g]
    pad = jnp.zeros((LANES - GLA_RANK - IDX_HEADS, wt_ref.shape[1]), MXU_DTYPE)
    small_ref[...] = jnp.concatenate(
        [group(_GROUP_IK), group(_GROUP_IK), group(_GROUP_GA), group(_GROUP_IW), pad], axis=0)


def _relayout_w_in(w_in):
    wt = w_in.T
    n, d = wt.shape
    return pl.pallas_call(
        _relayout_kernel,
        out_shape=(jax.ShapeDtypeStruct((U_COLS, d), MXU_DTYPE),
                   jax.ShapeDtypeStruct((2 * LANES, d), MXU_DTYPE)),
        grid=(d // RELAYOUT_COLS,),
        in_specs=[pl.BlockSpec((n, RELAYOUT_COLS), lambda i: (0, i))],
        out_specs=(pl.BlockSpec((U_COLS, RELAYOUT_COLS), lambda i: (0, i)),
                   pl.BlockSpec((2 * LANES, RELAYOUT_COLS), lambda i: (0, i))),
        compiler_params=pltpu.CompilerParams(
            dimension_semantics=("arbitrary",),
            vmem_limit_bytes=VMEM_LIMIT),
        name="relayout",
    )(wt)


def _col_scale():
    s = np.ones((1, U_COLS), np.float32)
    s[:, COL_GQ:COL_GQ + GLA_HEADS * GLA_DK] = GLA_DK ** -0.5
    s[:, COL_DQ:COL_DQ + DSA_HEADS * DSA_HD] = DSA_HD ** -0.5 * LOG2_E
    s[:, COL_XQ:COL_XQ + X_HEADS * X_HD] = X_HD ** -0.5
    return jnp.asarray(s)


def _layer(x2, mem2, g_pre, g_post, g_mem, w_in, w_up, b_a, g_gla, rel_bias, w_kv,
           w_g, w_d, w_x, w_o, batch, seq):
    w_main, w_small = _relayout_w_in(w_in)
    tm = min(PROJ_TM, batch * seq)
    u, ikd, small = _proj(x2, g_pre.reshape(1, -1), w_main, w_small, _col_scale(), tm, PROJ_TN)
    y_gla = _gla(u, small, w_up.astype(MXU_DTYPE), b_a.reshape(1, -1), g_gla.reshape(1, -1),
                 batch, seq, chunks_per_step=4)
    y_dsa = _dsa(u, ikd, small, rel_bias, batch, seq)
    mkv = _memkv(mem2, g_mem.reshape(1, -1), w_kv.astype(MXU_DTYPE), batch)
    return _merge(x2, y_gla, y_dsa, u, mkv, w_g.astype(MXU_DTYPE), w_d.astype(MXU_DTYPE),
                  w_x.astype(MXU_DTYPE), w_o.astype(MXU_DTYPE), g_post.reshape(1, -1), seq, MERGE_TM)


def kernel(x, mem, g_pre, g_post, g_mem, w_in, w_gla_a_up, b_gla_a, g_gla, rel_bias,
           w_mem_kv, w_gla_out, w_dsa_out, w_x_out, w_o):
    batch, seq, d = x.shape
    x2 = x.reshape(batch * seq, d)
    mem2 = mem.reshape(batch * N_MEM, d)
    for i in range(g_pre.shape[0]):
        x2 = _layer(x2, mem2, g_pre[i], g_post[i], g_mem[i], w_in[i], w_gla_a_up[i],
                    b_gla_a[i], g_gla[i], rel_bias, w_mem_kv[i], w_gla_out[i],
                    w_dsa_out[i], w_x_out[i], w_o[i], batch, seq)
    return x2.reshape(batch, seq, d)
```

```python
import functools
import math

import jax
import jax.numpy as jnp
import numpy as np
from jax import lax
from jax.experimental import pallas as pl
from jax.experimental.pallas import tpu as pltpu

D_MODEL = 1024
N_MEM = 256
EPS = 1e-6
GLA_HEADS = 4
GLA_DK = 128
GLA_DV = 256
GLA_RANK = 16
GLA_GATE_NORM = 16.0
GLA_CHUNK = 64
DSA_HEADS = 8
DSA_KV_HEADS = 2
DSA_GROUPS = DSA_HEADS // DSA_KV_HEADS
DSA_HD = 128
IDX_HEADS = 8
IDX_DIM = 64
TOPK_MAX = 256
Q_BLOCK = 128
REL_BUCKETS = 32
REL_MAX_DIST = 128
X_HEADS = 4
X_HD = 256

SPLIT_SIZES = (512, 512, 1024, 16, 1024, 1024, 256, 256, 512, 64, 8, 1024, 1024, 1024, 3072)

LANES = 128
SUBLANES = 8

MXU_DTYPE = jnp.bfloat16
ACT_DTYPE = jnp.bfloat16

U_COLS = 11264
COL_GQ, COL_GK, COL_GV, COL_GZ = 0, 512, 1024, 2048
COL_DQ, COL_DZ, COL_XQ, COL_XZ = 3072, 4096, 5120, 6144
COL_SG, COL_SD, COL_SM = 7168, 8192, 9216
COL_DK, COL_DV, COL_IQ = 10240, 10496, 10752
SMALL_GA, SMALL_IW = 0, 16

MASK_NEG = -1e30
LOG2_E = math.log2(math.e)
INT_MIN = -(2 ** 31)
VMEM_LIMIT = 56 * 1024 * 1024

PROJ_TM = 1024
PROJ_TN = U_COLS // 4
MERGE_TM = 512


def _nt_dot(a, b):
    return lax.dot_general(a, b, (((1,), (1,)), ((), ())),
                           preferred_element_type=jnp.float32)


def _tn_dot(a, b):
    return lax.dot_general(a, b, (((0,), (0,)), ((), ())),
                           preferred_element_type=jnp.float32)


def _dot(a, b):
    return jnp.dot(a, b, preferred_element_type=jnp.float32)


def _silu(z):
    return z * jax.nn.sigmoid(z)


def _proj_kernel(x_ref, g_ref, w_ref, ws_ref, cs_ref, u_ref, ikd_ref, sm_ref, h_ref):
    @pl.when(pl.program_id(1) == 0)
    def _():
        xf = x_ref[...]
        y = xf * lax.rsqrt(jnp.mean(xf * xf, axis=-1, keepdims=True) + EPS)
        hb = (y * g_ref[...]).astype(MXU_DTYPE)
        h_ref[...] = hb
        r = _nt_dot(hb, ws_ref[...])
        ikd_ref[...] = r[:, :LANES].astype(ikd_ref.dtype)
        sm_ref[...] = r[:, LANES:]

    acc = _nt_dot(h_ref[...], w_ref[...])
    u_ref[...] = (acc * cs_ref[...]).astype(u_ref.dtype)


def _proj(x2, g_pre, w_main, w_small, col_scale, tm, tn):
    m = x2.shape[0]
    grid = (m // tm, U_COLS // tn)
    return pl.pallas_call(
        _proj_kernel,
        out_shape=(jax.ShapeDtypeStruct((m, U_COLS), ACT_DTYPE),
                   jax.ShapeDtypeStruct((m, LANES), ACT_DTYPE),
                   jax.ShapeDtypeStruct((m, LANES), jnp.float32)),
        grid=grid,
        in_specs=[
            pl.BlockSpec((tm, D_MODEL), lambda i, j: (i, 0)),
            pl.BlockSpec((1, D_MODEL), lambda i, j: (0, 0)),
            pl.BlockSpec((tn, D_MODEL), lambda i, j: (j, 0)),
            pl.BlockSpec((2 * LANES, D_MODEL), lambda i, j: (0, 0)),
            pl.BlockSpec((1, tn), lambda i, j: (0, j)),
        ],
        out_specs=(
            pl.BlockSpec((tm, tn), lambda i, j: (i, j)),
            pl.BlockSpec((tm, LANES), lambda i, j: (i, 0)),
            pl.BlockSpec((tm, LANES), lambda i, j: (i, 0)),
        ),
        scratch_shapes=[pltpu.VMEM((tm, D_MODEL), MXU_DTYPE)],
        compiler_params=pltpu.CompilerParams(
            dimension_semantics=("arbitrary", "arbitrary"),
            vmem_limit_bytes=VMEM_LIMIT),
        name="proj",
    )(x2, g_pre, w_main, w_small, col_scale)


def _split3(x):
    hi = x.astype(MXU_DTYPE)
    r1 = x - hi.astype(jnp.float32)
    mid = r1.astype(MXU_DTYPE)
    lo = (r1 - mid.astype(jnp.float32)).astype(MXU_DTYPE)
    return hi, mid, lo


def _gla_kernel(q_ref, k_ref, v_ref, z_ref, sm_ref, wup_ref, ba_ref, gg_ref,
                o_ref, st_ref, qd_ref, kd_ref, kt_ref, b_ref, oacc_ref, *, chunks_per_step):
    c = GLA_CHUNK
    rows = chunks_per_step * c
    heads = range(GLA_HEADS)

    @pl.when(pl.program_id(1) == 0)
    def _():
        st_ref[...] = jnp.zeros_like(st_ref)

    def ks(h):
        return slice(h * GLA_DK, (h + 1) * GLA_DK)

    def vs(h):
        return slice(h * GLA_DV, (h + 1) * GLA_DV)

    def chunk(i):
        return slice(i * c, (i + 1) * c)

    row = lax.broadcasted_iota(jnp.int32, (rows, rows), 0)
    col = lax.broadcasted_iota(jnp.int32, (rows, rows), 1)
    causal = (row // c == col // c) & (col <= row)
    tril = jnp.where(causal, 1.0, 0.0).astype(MXU_DTYPE)

    ga = sm_ref[:, SMALL_GA:SMALL_GA + GLA_RANK].astype(MXU_DTYPE)
    pre = _dot(ga, wup_ref[...]) + ba_ref[...]
    log_a = (jnp.minimum(pre, 0.0) - jnp.log1p(jnp.exp(-jnp.abs(pre)))) / GLA_GATE_NORM
    hi, mid, lo = _split3(log_a)
    b_ref[...] = _dot(tril, hi) + _dot(tril, mid) + _dot(tril, lo)

    decay = []
    for i in range(chunks_per_step):
        b_last = b_ref[i * c + c - 1:i * c + c, :]
        decay.append(jnp.exp(b_last))
        for h in heads:
            b = b_ref[chunk(i), ks(h)]
            q = q_ref[chunk(i), ks(h)].astype(jnp.float32)
            k = k_ref[chunk(i), ks(h)].astype(jnp.float32)
            qd_ref[chunk(i), ks(h)] = (q * jnp.exp(b)).astype(MXU_DTYPE)
            kd_ref[chunk(i), ks(h)] = (k * jnp.exp(-b)).astype(MXU_DTYPE)
            kt_ref[chunk(i), ks(h)] = (k * jnp.exp(b_last[:, ks(h)] - b)).astype(MXU_DTYPE)

    att = [_nt_dot(qd_ref[:, ks(h)], kd_ref[:, ks(h)]) for h in heads]
    att = [jnp.where(causal, a, 0.0).astype(MXU_DTYPE) for a in att]
    for h in heads:
        oacc_ref[:, vs(h)] = _dot(att[h], v_ref[:, vs(h)])

    st = [st_ref[h] for h in heads]
    for i in range(chunks_per_step):
        for h in heads:
            kv = _tn_dot(v_ref[chunk(i), vs(h)], kt_ref[chunk(i), ks(h)])
            oacc_ref[chunk(i), vs(h)] += _nt_dot(qd_ref[chunk(i), ks(h)], st[h].astype(MXU_DTYPE))
            st[h] = st[h] * decay[i][:, ks(h)] + kv
    for h in heads:
        st_ref[h] = st[h]
        oh = oacc_ref[:, vs(h)]
        y = oh * lax.rsqrt(jnp.mean(oh * oh, axis=-1, keepdims=True) + EPS)
        y = y * gg_ref[...]
        zz = z_ref[:, vs(h)].astype(jnp.float32)
        o_ref[:, vs(h)] = (y * _silu(zz)).astype(o_ref.dtype)


def _gla(u, small, w_up, b_a, g_gla, batch, seq, chunks_per_step):
    rows = chunks_per_step * GLA_CHUNK
    steps = seq // rows
    hk = GLA_HEADS * GLA_DK
    hv = GLA_HEADS * GLA_DV

    def rmap(cb):
        return lambda b, s: (b * steps + s, cb)

    return pl.pallas_call(
        functools.partial(_gla_kernel, chunks_per_step=chunks_per_step),
        out_shape=jax.ShapeDtypeStruct((batch * seq, hv), ACT_DTYPE),
        grid=(batch, steps),
        in_specs=[
            pl.BlockSpec((rows, hk), rmap(COL_GQ // hk)),
            pl.BlockSpec((rows, hk), rmap(COL_GK // hk)),
            pl.BlockSpec((rows, hv), rmap(COL_GV // hv)),
            pl.BlockSpec((rows, hv), rmap(COL_GZ // hv)),
            pl.BlockSpec((rows, LANES), rmap(0)),
            pl.BlockSpec((GLA_RANK, hk), lambda b, s: (0, 0)),
            pl.BlockSpec((1, hk), lambda b, s: (0, 0)),
            pl.BlockSpec((1, GLA_DV), lambda b, s: (0, 0)),
        ],
        out_specs=pl.BlockSpec((rows, hv), rmap(0)),
        scratch_shapes=[pltpu.VMEM((GLA_HEADS, GLA_DV, GLA_DK), jnp.float32),
                        pltpu.VMEM((rows, hk), MXU_DTYPE),
                        pltpu.VMEM((rows, hk), MXU_DTYPE),
                        pltpu.VMEM((rows, hk), MXU_DTYPE),
                        pltpu.VMEM((rows, hk), jnp.float32),
                        pltpu.VMEM((rows, hv), jnp.float32)],
        compiler_params=pltpu.CompilerParams(
            dimension_semantics=("arbitrary", "arbitrary"),
            vmem_limit_bytes=VMEM_LIMIT),
        name="gla",
    )(u, u, u, u, small, w_up, b_a, g_gla)


KEY_CHUNK = 4 * Q_BLOCK
BLOCKS_PER_CHUNK = KEY_CHUNK // Q_BLOCK
PLANE_GROUP = 32 * SUBLANES


def _t5_bucket(dist):
    max_exact = REL_BUCKETS // 2
    d = jnp.maximum(dist, 1).astype(jnp.float32)
    large = max_exact + jnp.floor(jnp.log(d / max_exact) / math.log(REL_MAX_DIST / max_exact)
                                  * (REL_BUCKETS - max_exact)).astype(jnp.int32)
    large = jnp.minimum(large, REL_BUCKETS - 1)
    return jnp.where(dist < max_exact, dist, large)


def _dsa_kernel(rb_ref, dq_ref, dk_ref, dv_ref, iq_ref, ikd_ref, sm_ref, dz_ref,
                out_ref,
                kt_ref, sc_ref, planes_ref, mb_ref, vt_ref, bt_ref, iqm_ref, qaug_ref, acc_ref,
                lga_ref, lgb_ref, mxa_ref, mxb_ref, ml_ref,
                *, seq):
    qb = pl.program_id(1)
    blk = Q_BLOCK
    ch = KEY_CHUNK
    k_sel = min(TOPK_MAX, seq // 4)
    n_chunks = qb // BLOCKS_PER_CHUNK + 1
    width = DSA_GROUPS * blk

    row_i = lax.broadcasted_iota(jnp.int32, (blk, blk), 0)
    col_i = lax.broadcasted_iota(jnp.int32, (blk, blk), 1)
    crow_i = lax.broadcasted_iota(jnp.int32, (ch, blk), 0)
    q_pos = qb * blk + lax.broadcasted_iota(jnp.int32, (ch, blk), 1)

    @pl.when((pl.program_id(0) == 0) & (qb == 0))
    def _():
        planes_ref[...] = jnp.zeros_like(planes_ref)
        kt_ref[...] = jnp.zeros_like(kt_ref)
        for delta in range(3):
            dist = jnp.maximum(delta * blk + col_i - row_i, 0)
            bucket = _t5_bucket(dist)
            for h in range(DSA_HEADS):
                tile = jnp.zeros((blk, blk), jnp.float32)
                for bk in range(REL_BUCKETS):
                    tile = jnp.where(bucket == bk, rb_ref[bk, h] * LOG2_E, tile)
                cc, g = divmod(h, DSA_GROUPS)
                bt_ref[delta, cc, :, g * blk:(g + 1) * blk] = tile

    @pl.when(qb == 0)
    def _():
        def body(i, carry):
            r = pl.multiple_of(i * ch, ch)
            vt_ref[i] = dv_ref[pl.ds(r, ch), :].astype(jnp.float32).T.astype(vt_ref.dtype)
            return carry
        lax.fori_loop(0, seq // ch, body, 0)

    w_t = sm_ref[...].T[SMALL_IW:SMALL_IW + IDX_HEADS, :]
    w_t = w_t * (IDX_HEADS ** -0.5) * (IDX_DIM ** -0.5)

    lane = lax.broadcasted_iota(jnp.int32, (blk, LANES), 1)
    for h in range(IDX_HEADS):
        pair = iq_ref[:, (h // 2) * LANES:(h // 2 + 1) * LANES]
        keep = (lane < IDX_DIM) if h % 2 == 0 else (lane >= IDX_DIM)
        iqm_ref[h // 2, (h % 2) * blk:(h % 2 + 1) * blk, :] = jnp.where(keep, pair, jnp.zeros_like(pair))

    def score_chunk(c, causal):
        r = pl.multiple_of(c * ch, ch)
        ikc = ikd_ref[pl.ds(r, ch), :]
        acc = None
        for hp in range(IDX_HEADS // 2):
            s2 = _nt_dot(ikc, iqm_ref[hp])
            for h in (2 * hp, 2 * hp + 1):
                t = w_t[h:h + 1, :] * jnp.maximum(s2[:, (h % 2) * blk:(h % 2 + 1) * blk], 0.0)
                acc = t if acc is None else acc + t
        if causal:
            acc = jnp.where(r + crow_i <= q_pos, acc, -jnp.inf)
        sc_ref[pl.ds(r, ch), :] = acc
        bits = lax.bitcast_convert_type(acc, jnp.int32)
        kt_ref[pl.ds(r, ch), :] = bits ^ ((bits >> 31) & 0x7FFFFFFF)

    def build_planes(c):
        for g in range(ch // PLANE_GROUP):
            base = c * ch + g * PLANE_GROUP
            a = [kt_ref[pl.ds(pl.multiple_of(base + SUBLANES * v, SUBLANES), SUBLANES), :]
                 for v in range(32)]
            j, m = 16, 0x0000FFFF
            while j:
                k0 = 0
                while k0 < 32:
                    t = (a[k0] ^ lax.shift_right_logical(a[k0 + j], jnp.int32(j))) & m
                    a[k0] = a[k0] ^ t
                    a[k0 + j] = a[k0 + j] ^ (t << j)
                    k0 = (k0 + j + 1) & ~j
                j >>= 1
                m = (m ^ (m << j)) & 0xFFFFFFFF if j else m
                m = m - (1 << 32) if m >= (1 << 31) else m
            a[0] = ~a[0]
            row = pl.multiple_of(c * (ch // 32) + g * SUBLANES, SUBLANES)
            for jj in range(32):
                planes_ref[jj, pl.ds(row, SUBLANES), :] = a[jj]

    @pl.when(n_chunks > 1)
    def _():
        score_chunk(0, False)

    def score_body(c, carry):
        build_planes(c - 1)
        score_chunk(c, False)
        return carry
    lax.fori_loop(1, n_chunks - 1, score_body, 0)
    build_planes(jnp.maximum(n_chunks - 2, 0))
    score_chunk(n_chunks - 1, True)
    build_planes(n_chunks - 1)

    n_rows = seq // 32

    def rowsum(x):
        part = jnp.sum(x.reshape(n_rows // SUBLANES, SUBLANES, blk), axis=0)
        return jnp.sum(part, axis=0, keepdims=True)

    prow = lax.broadcasted_iota(jnp.int32, (n_rows, blk), 0)
    alive0 = jnp.where(prow < n_chunks * (ch // 32), -1, 0).astype(jnp.int32)

    def bit_body(j, carry):
        alive, cnt_gt, ukey = carry
        w = planes_ref[j]
        ones = alive & w
        c1 = rowsum(lax.population_count(ones))
        take = cnt_gt + c1 >= k_sel
        alive = jnp.where(take, ones, alive & ~w)
        cnt_gt = jnp.where(take, cnt_gt, cnt_gt + c1)
        ukey = jnp.where(take, ukey | (jnp.int32(1) << (31 - j)), ukey)
        return alive, cnt_gt, ukey

    zero_row = jnp.zeros((1, blk), jnp.int32)
    alive, cnt_gt, ukey = lax.fori_loop(0, 32, bit_body, (alive0, zero_row, zero_row))
    ans = ukey ^ INT_MIN
    thr_bits = jnp.where(ans < 0, ans ^ 0x7FFFFFFF, ans)
    thr0 = lax.bitcast_convert_type(thr_bits, jnp.float32)

    def tile_sum(x):
        return jnp.sum(x.reshape(ch // SUBLANES, SUBLANES, blk), axis=0)

    def fold_rows(acc, combine):
        return functools.reduce(combine, [acc[i:i + 1] for i in range(SUBLANES)])

    def mask_pass(thr, p_max):
        def body(c, carry):
            a_gt, a_ge = carry
            r = pl.multiple_of(c * ch, ch)
            sc = sc_ref[pl.ds(r, ch), :]
            k_pos = r + crow_i
            gt = sc > thr
            ge = sc >= thr
            sel = (gt | (ge & (k_pos <= p_max))) & (k_pos <= q_pos)
            mb_ref[pl.ds(r, ch), :] = jnp.where(sel, 0.0, MASK_NEG).astype(mb_ref.dtype)
            return (a_gt + tile_sum(jnp.where(gt, 1, 0).astype(jnp.int32)),
                    a_ge + tile_sum(jnp.where(ge, 1, 0).astype(jnp.int32)))
        zero = jnp.zeros((SUBLANES, blk), jnp.int32)
        a_gt, a_ge = lax.fori_loop(0, n_chunks, body, (zero, zero))
        return fold_rows(a_gt, jnp.add), fold_rows(a_ge, jnp.add)

    def nearest(pred_fn, fill, combine):
        def body(c, acc):
            r = pl.multiple_of(c * ch, ch)
            sc = sc_ref[pl.ds(r, ch), :]
            part = jnp.where(pred_fn(sc), sc, fill).reshape(ch // SUBLANES, SUBLANES, blk)
            return combine(acc, functools.reduce(combine, [part[i] for i in range(ch // SUBLANES)]))
        acc = lax.fori_loop(0, n_chunks, body, jnp.full((SUBLANES, blk), fill, jnp.float32))
        return fold_rows(acc, combine)

    def off_target(state):
        _, c_gt, c_ge = state
        return jnp.max(jnp.where((c_gt >= k_sel) | (c_ge < k_sel), 1, 0)) > 0

    def walk(state):
        thr, c_gt, c_ge = state
        above = nearest(lambda sc: sc > thr, jnp.inf, jnp.minimum)
        below = nearest(lambda sc: sc < thr, -jnp.inf, jnp.maximum)
        thr = jnp.where(c_gt >= k_sel, above, jnp.where(c_ge < k_sel, below, thr))
        return (thr,) + mask_pass(thr, no_bound)

    no_bound = jnp.full((1, blk), 2 * seq, jnp.int32)
    thr, cnt_gt, cnt_ge = lax.while_loop(off_target, walk, (thr0,) + mask_pass(thr0, no_bound))

    @pl.when(jnp.max(cnt_ge) > k_sel)
    def _():
        need = k_sel - cnt_gt
        n_bits = (2 * seq - 1).bit_length()

        def tie_count(cand):
            def body(c, acc):
                r = pl.multiple_of(c * ch, ch)
                hit = (sc_ref[pl.ds(r, ch), :] == thr) & (r + crow_i < cand)
                return acc + tile_sum(jnp.where(hit, 1, 0).astype(jnp.int32))
            return fold_rows(lax.fori_loop(0, n_chunks, body, jnp.zeros((SUBLANES, blk), jnp.int32)), jnp.add)

        def pos_body(i, p):
            cand = p | (jnp.int32(1) << (n_bits - 1 - i))
            return jnp.where(tie_count(cand) <= need - 1, cand, p)
        p_max = lax.fori_loop(0, n_bits, pos_body, jnp.zeros((1, blk), jnp.int32))
        mask_pass(thr, p_max)

    eye = jnp.where(row_i == col_i, 1.0, 0.0).astype(MXU_DTYPE)
    for cc in range(DSA_KV_HEADS):
        for g in range(DSA_GROUPS):
            h = cc * DSA_GROUPS + g
            qaug_ref[cc, g * blk:(g + 1) * blk, 0:DSA_HD] = dq_ref[:, h * DSA_HD:(h + 1) * DSA_HD]
            qaug_ref[cc, g * blk:(g + 1) * blk, DSA_HD:2 * DSA_HD] = eye
    acc_ref[...] = jnp.zeros_like(acc_ref)
    for cc in range(DSA_KV_HEADS):
        ml_ref[cc, 0] = jnp.full(ml_ref.shape[2:], -jnp.inf, jnp.float32)
        ml_ref[cc, 1] = jnp.zeros(ml_ref.shape[2:], jnp.float32)

    def stage_logits(c, lg_ref, mx_ref, far):
        r = pl.multiple_of(c * ch, ch)
        mbc = mb_ref[pl.ds(r, ch), :]
        for cc in range(DSA_KV_HEADS):
            kaug = jnp.concatenate([dk_ref[pl.ds(r, ch), cc * DSA_HD:(cc + 1) * DSA_HD], mbc], axis=1)
            lg = _nt_dot(kaug, qaug_ref[cc])
            if far:
                off = bt_ref[2, cc, 0:1, :]
            else:
                lg = lg + jnp.concatenate(
                    [bt_ref[jnp.clip(qb - (c * BLOCKS_PER_CHUNK + j), 0, 2), cc]
                     for j in range(BLOCKS_PER_CHUNK)], axis=0)
                off = jnp.zeros((1, width), jnp.float32)
            lg_ref[cc] = lg
            mx_ref[cc, 0] = jnp.broadcast_to(jnp.max(lg, axis=0, keepdims=True) + off, mx_ref.shape[2:])
            mx_ref[cc, 1] = jnp.broadcast_to(off, mx_ref.shape[2:])

    def stage_softmax(c, lg_ref, mx_ref):
        for cc in range(DSA_KV_HEADS):
            m = ml_ref[cc, 0, 0:1, :]
            l = ml_ref[cc, 1, 0:1, :]
            m_new = jnp.maximum(m, mx_ref[cc, 0, 0:1, :])
            alpha = jnp.exp2(m - m_new)
            p = jnp.exp2(lg_ref[cc] - (m_new - mx_ref[cc, 1, 0:1, :]))
            l_new = alpha * l + jnp.sum(p, axis=0, keepdims=True)
            pv = _dot(vt_ref[c, cc * DSA_HD:(cc + 1) * DSA_HD, :], p.astype(MXU_DTYPE))
            acc_ref[cc] = acc_ref[cc] * alpha + pv
            ml_ref[cc, 0] = jnp.broadcast_to(m_new, ml_ref.shape[2:])
            ml_ref[cc, 1] = jnp.broadcast_to(l_new, ml_ref.shape[2:])

    c_near = jnp.maximum(qb - 1, 0) // BLOCKS_PER_CHUNK

    @pl.when(c_near > 0)
    def _():
        stage_logits(0, lga_ref, mxa_ref, far=True)

    @pl.when(c_near == 0)
    def _():
        stage_logits(0, lga_ref, mxa_ref, far=False)

    def pair_body(pi, carry, far):
        c0 = 2 * pi
        stage_logits(c0 + 1, lgb_ref, mxb_ref, far)
        stage_softmax(c0, lga_ref, mxa_ref)

        stage_logits(jnp.minimum(c0 + 2, n_chunks - 1), lga_ref, mxa_ref, far)
        stage_softmax(c0 + 1, lgb_ref, mxb_ref)
        return carry

    far_pairs = jnp.maximum(c_near - 1, 0) // 2
    lax.fori_loop(0, far_pairs, functools.partial(pair_body, far=True), 0)
    lax.fori_loop(far_pairs, n_chunks // 2, functools.partial(pair_body, far=False), 0)

    @pl.when(n_chunks % 2 == 1)
    def _():
        stage_softmax(n_chunks - 1, lga_ref, mxa_ref)

    for cc in range(DSA_KV_HEADS):
        o_t = acc_ref[cc] / ml_ref[cc, 1, 0:1, :]
        for g in range(DSA_GROUPS):
            h = cc * DSA_GROUPS + g
            o = o_t[:, g * blk:(g + 1) * blk].T
            zz = dz_ref[:, h * DSA_HD:(h + 1) * DSA_HD].astype(jnp.float32)
            out_ref[:, h * DSA_HD:(h + 1) * DSA_HD] = (o * _silu(zz)).astype(out_ref.dtype)


def _dsa(u, ikd, small, rel_bias, batch, seq):
    nb = seq // Q_BLOCK
    hq = DSA_HEADS * DSA_HD
    hkv = DSA_KV_HEADS * DSA_HD
    hi = IDX_HEADS * IDX_DIM

    def qmap(cb):
        return lambda b, q: (b * nb + q, cb)

    def bmap(cb):
        return lambda b, q: (b, cb)

    return pl.pallas_call(
        functools.partial(_dsa_kernel, seq=seq),
        out_shape=jax.ShapeDtypeStruct((batch * seq, hq), ACT_DTYPE),
        grid=(batch, nb),
        in_specs=[
            pl.BlockSpec(memory_space=pltpu.SMEM),
            pl.BlockSpec((Q_BLOCK, hq), qmap(COL_DQ // hq)),
            pl.BlockSpec((seq, hkv), bmap(COL_DK // hkv)),
            pl.BlockSpec((seq, hkv), bmap(COL_DV // hkv)),
            pl.BlockSpec((Q_BLOCK, hi), qmap(COL_IQ // hi)),
            pl.BlockSpec((seq, LANES), bmap(0)),
            pl.BlockSpec((Q_BLOCK, LANES), qmap(0)),
            pl.BlockSpec((Q_BLOCK, hq), qmap(COL_DZ // hq)),
        ],
        out_specs=pl.BlockSpec((Q_BLOCK, hq), qmap(0)),
        scratch_shapes=[
            pltpu.VMEM((seq, Q_BLOCK), jnp.int32),
            pltpu.VMEM((seq, Q_BLOCK), jnp.float32),
            pltpu.VMEM((32, seq // 32, Q_BLOCK), jnp.int32),
            pltpu.VMEM((seq, Q_BLOCK), MXU_DTYPE),
            pltpu.VMEM((seq // KEY_CHUNK, hkv, KEY_CHUNK), MXU_DTYPE),
            pltpu.VMEM((3, DSA_KV_HEADS, Q_BLOCK, DSA_GROUPS * Q_BLOCK), jnp.float32),
            pltpu.VMEM((IDX_HEADS // 2, 2 * Q_BLOCK, LANES), MXU_DTYPE),
            pltpu.VMEM((DSA_KV_HEADS, DSA_GROUPS * Q_BLOCK, 2 * DSA_HD), MXU_DTYPE),
            pltpu.VMEM((DSA_KV_HEADS, DSA_HD, DSA_GROUPS * Q_BLOCK), jnp.float32),
            pltpu.VMEM((DSA_KV_HEADS, KEY_CHUNK, DSA_GROUPS * Q_BLOCK), jnp.float32),
            pltpu.VMEM((DSA_KV_HEADS, KEY_CHUNK, DSA_GROUPS * Q_BLOCK), jnp.float32),
            pltpu.VMEM((DSA_KV_HEADS, 2, SUBLANES, DSA_GROUPS * Q_BLOCK), jnp.float32),
            pltpu.VMEM((DSA_KV_HEADS, 2, SUBLANES, DSA_GROUPS * Q_BLOCK), jnp.float32),
            pltpu.VMEM((DSA_KV_HEADS, 2, SUBLANES, DSA_GROUPS * Q_BLOCK), jnp.float32),
        ],
        compiler_params=pltpu.CompilerParams(
            dimension_semantics=("arbitrary", "arbitrary"),
            vmem_limit_bytes=VMEM_LIMIT),
        name="dsa",
    )(rel_bias, u, u, u, u, ikd, small, u)


def _memkv_kernel(mem_ref, g_ref, w_ref, o_ref):
    xf = mem_ref[...]
    y = xf * lax.rsqrt(jnp.mean(xf * xf, axis=-1, keepdims=True) + EPS)
    hb = (y * g_ref[...]).astype(MXU_DTYPE)
    o_ref[...] = _dot(hb, w_ref[...]).astype(o_ref.dtype)


def _memkv(mem2, g_mem, w_kv, batch):
    n = 2 * X_HEADS * X_HD
    return pl.pallas_call(
        _memkv_kernel,
        out_shape=jax.ShapeDtypeStruct((batch * N_MEM, n), ACT_DTYPE),
        grid=(batch,),
        in_specs=[
            pl.BlockSpec((N_MEM, D_MODEL), lambda b: (b, 0)),
            pl.BlockSpec((1, D_MODEL), lambda b: (0, 0)),
            pl.BlockSpec((D_MODEL, n), lambda b: (0, 0)),
        ],
        out_specs=pl.BlockSpec((N_MEM, n), lambda b: (b, 0)),
        compiler_params=pltpu.CompilerParams(
            dimension_semantics=("arbitrary",),
            vmem_limit_bytes=VMEM_LIMIT),
        name="memkv",
    )(mem2, g_mem, w_kv)


def _merge_kernel(x_ref, yg_ref, yd_ref, xq_ref, xz_ref, sg_ref, sd_ref, sm_ref, mkv_ref,
                  wg_ref, wd_ref, wx_ref, wo_ref, gp_ref, o_ref, ym_ref):
    hw = X_HEADS * X_HD
    heads = range(X_HEADS)

    def cs(h):
        return slice(h * X_HD, (h + 1) * X_HD)

    lgs = [_nt_dot(xq_ref[:, cs(h)], mkv_ref[:, cs(h)]) for h in heads]
    t_g = _dot(yg_ref[...], wg_ref[...])
    ps = []
    for lg in lgs:
        e = jnp.exp(lg - jnp.max(lg, axis=-1, keepdims=True))
        ps.append((e / jnp.sum(e, axis=-1, keepdims=True)).astype(MXU_DTYPE))
    outs = [_dot(ps[h], mkv_ref[:, hw + h * X_HD:hw + (h + 1) * X_HD]) for h in heads]
    t_d = _dot(yd_ref[...], wd_ref[...])
    for h in heads:
        ym_ref[:, cs(h)] = (outs[h] * _silu(xz_ref[:, cs(h)].astype(jnp.float32))).astype(ym_ref.dtype)
    merged = jax.nn.sigmoid(sg_ref[...].astype(jnp.float32)) * t_g
    merged = merged + jax.nn.sigmoid(sd_ref[...].astype(jnp.float32)) * t_d
    merged = merged + jax.nn.sigmoid(sm_ref[...].astype(jnp.float32)) * _dot(ym_ref[...], wx_ref[...])
    t = _dot(merged.astype(MXU_DTYPE), wo_ref[...])
    y = t * lax.rsqrt(jnp.mean(t * t, axis=-1, keepdims=True) + EPS)
    o_ref[...] = x_ref[...] + y * gp_ref[...]


def _merge(x2, y_gla, y_dsa, u, mkv, w_g, w_d, w_x, w_o, g_post, seq, tm):
    m = x2.shape[0]
    steps_per_batch = seq // tm
    d = D_MODEL

    def rmap(cb):
        return lambda i: (i, cb)

    wspec = pl.BlockSpec((d, d), lambda i: (0, 0))
    return pl.pallas_call(
        _merge_kernel,
        out_shape=jax.ShapeDtypeStruct((m, d), jnp.float32),
        grid=(m // tm,),
        in_specs=[
            pl.BlockSpec((tm, d), rmap(0)),
            pl.BlockSpec((tm, d), rmap(0)),
            pl.BlockSpec((tm, d), rmap(0)),
            pl.BlockSpec((tm, d), rmap(COL_XQ // d)),
            pl.BlockSpec((tm, d), rmap(COL_XZ // d)),
            pl.BlockSpec((tm, d), rmap(COL_SG // d)),
            pl.BlockSpec((tm, d), rmap(COL_SD // d)),
            pl.BlockSpec((tm, d), rmap(COL_SM // d)),
            pl.BlockSpec((N_MEM, 2 * X_HEADS * X_HD), lambda i: (i // steps_per_batch, 0)),
            wspec, wspec, wspec, wspec,
            pl.BlockSpec((1, d), lambda i: (0, 0)),
        ],
        out_specs=pl.BlockSpec((tm, d), rmap(0)),
        scratch_shapes=[pltpu.VMEM((tm, d), MXU_DTYPE)],
        compiler_params=pltpu.CompilerParams(
            dimension_semantics=("arbitrary",),
            vmem_limit_bytes=VMEM_LIMIT),
        name="merge",
    )(x2, y_gla, y_dsa, u, u, u, u, u, mkv, w_g, w_d, w_x, w_o, g_post)


_MAIN_GROUPS = (0, 1, 2, 4, 5, 11, 12, 13, 14, 6, 7, 8)
_GROUP_GA, _GROUP_IK, _GROUP_IW = 3, 9, 10
RELAYOUT_COLS = 128


def _relayout_kernel(wt_ref, main_ref, small_ref):
    offs = np.concatenate([[0], np.cumsum(np.array(SPLIT_SIZES))]).tolist()

    def group(g):
        return wt_ref[offs[g]:offs[g + 1], :].astype(MXU_DTYPE)

    dst = 0
    for g in _MAIN_GROUPS:
        main_ref[dst:dst + SPLIT_SIZES[g], :] = group(g)
        dst += SPLIT_SIZES[g]
    pad = jnp.zeros((LANES - GLA_RANK - IDX_HEADS, wt_ref.shape[1]), MXU_DTYPE)
    small_ref[...] = jnp.concatenate(
        [group(_GROUP_IK), group(_GROUP_IK), group(_GROUP_GA), group(_GROUP_IW), pad], axis=0)


def _relayout_w_in(w_in):
    wt = w_in.T
    n, d = wt.shape
    return pl.pallas_call(
        _relayout_kernel,
        out_shape=(jax.ShapeDtypeStruct((U_COLS, d), MXU_DTYPE),
                   jax.ShapeDtypeStruct((2 * LANES, d), MXU_DTYPE)),
        grid=(d // RELAYOUT_COLS,),
        in_specs=[pl.BlockSpec((n, RELAYOUT_COLS), lambda i: (0, i))],
        out_specs=(pl.BlockSpec((U_COLS, RELAYOUT_COLS), lambda i: (0, i)),
                   pl.BlockSpec((2 * LANES, RELAYOUT_COLS), lambda i: (0, i))),
        compiler_params=pltpu.CompilerParams(
            dimension_semantics=("arbitrary",),
            vmem_limit_bytes=VMEM_LIMIT),
        name="relayout",
    )(wt)


def _col_scale():
    s = np.ones((1, U_COLS), np.float32)
    s[:, COL_GQ:COL_GQ + GLA_HEADS * GLA_DK] = GLA_DK ** -0.5
    s[:, COL_DQ:COL_DQ + DSA_HEADS * DSA_HD] = DSA_HD ** -0.5 * LOG2_E
    s[:, COL_XQ:COL_XQ + X_HEADS * X_HD] = X_HD ** -0.5
    return jnp.asarray(s)


def _layer(x2, mem2, g_pre, g_post, g_mem, w_in, w_up, b_a, g_gla, rel_bias, w_kv,
           w_g, w_d, w_x, w_o, batch, seq):
    w_main, w_small = _relayout_w_in(w_in)
    tm = min(PROJ_TM, batch * seq)
    u, ikd, small = _proj(x2, g_pre.reshape(1, -1), w_main, w_small, _col_scale(), tm, PROJ_TN)
    y_gla = _gla(u, small, w_up.astype(MXU_DTYPE), b_a.reshape(1, -1), g_gla.reshape(1, -1),
                 batch, seq, chunks_per_step=4)
    y_dsa = _dsa(u, ikd, small, rel_bias, batch, seq)
    mkv = _memkv(mem2, g_mem.reshape(1, -1), w_kv.astype(MXU_DTYPE), batch)
    return _merge(x2, y_gla, y_dsa, u, mkv, w_g.astype(MXU_DTYPE), w_d.astype(MXU_DTYPE),
                  w_x.astype(MXU_DTYPE), w_o.astype(MXU_DTYPE), g_post.reshape(1, -1), seq, MERGE_TM)


def kernel(x, mem, g_pre, g_post, g_mem, w_in, w_gla_a_up, b_gla_a, g_gla, rel_bias,
           w_mem_kv, w_gla_out, w_dsa_out, w_x_out, w_o):
    batch, seq, d = x.shape
    x2 = x.reshape(batch * seq, d)
    mem2 = mem.reshape(batch * N_MEM, d)
    for i in range(g_pre.shape[0]):
        x2 = _layer(x2, mem2, g_pre[i], g_post[i], g_mem[i], w_in[i], w_gla_a_up[i],
                    b_gla_a[i], g_gla[i], rel_bias, w_mem_kv[i], w_gla_out[i],
                    w_dsa_out[i], w_x_out[i], w_o[i], batch, seq)
    return x2.reshape(batch, seq, d)
```

```python
import functools
import math

import jax
import jax.numpy as jnp
import numpy as np
from jax import lax
from jax.experimental import pallas as pl
from jax.experimental.pallas import tpu as pltpu

D_MODEL = 1024
N_MEM = 256
EPS = 1e-6
GLA_HEADS = 4
GLA_DK = 128
GLA_DV = 256
GLA_RANK = 16
GLA_GATE_NORM = 16.0
GLA_CHUNK = 64
DSA_HEADS = 8
DSA_KV_HEADS = 2
DSA_GROUPS = DSA_HEADS // DSA_KV_HEADS
DSA_HD = 128
IDX_HEADS = 8
IDX_DIM = 64
TOPK_MAX = 256
Q_BLOCK = 128
REL_BUCKETS = 32
REL_MAX_DIST = 128
X_HEADS = 4
X_HD = 256

SPLIT_SIZES = (512, 512, 1024, 16, 1024, 1024, 256, 256, 512, 64, 8, 1024, 1024, 1024, 3072)

LANES = 128
SUBLANES = 8

MXU_DTYPE = jnp.bfloat16
ACT_DTYPE = jnp.bfloat16

U_COLS = 11264
COL_GQ, COL_GK, COL_GV, COL_GZ = 0, 512, 1024, 2048
COL_DQ, COL_DZ, COL_XQ, COL_XZ = 3072, 4096, 5120, 6144
COL_SG, COL_SD, COL_SM = 7168, 8192, 9216
COL_DK, COL_DV, COL_IQ = 10240, 10496, 10752
SMALL_GA, SMALL_IW = 0, 16

MASK_NEG = -1e30
LOG2_E = math.log2(math.e)
INT_MIN = -(2 ** 31)
VMEM_LIMIT = 56 * 1024 * 1024

PROJ_TM = 1024
PROJ_TN = U_COLS // 4
MERGE_TM = 512


def _nt_dot(a, b):
    return lax.dot_general(a, b, (((1,), (1,)), ((), ())),
                           preferred_element_type=jnp.float32)


def _tn_dot(a, b):
    return lax.dot_general(a, b, (((0,), (0,)), ((), ())),
                           preferred_element_type=jnp.float32)


def _dot(a, b):
    return jnp.dot(a, b, preferred_element_type=jnp.float32)


def _silu(z):
    return z * jax.nn.sigmoid(z)


def _proj_kernel(x_ref, g_ref, w_ref, ws_ref, cs_ref, u_ref, ikd_ref, sm_ref, h_ref):
    @pl.when(pl.program_id(1) == 0)
    def _():
        xf = x_ref[...]
        y = xf * lax.rsqrt(jnp.mean(xf * xf, axis=-1, keepdims=True) + EPS)
        hb = (y * g_ref[...]).astype(MXU_DTYPE)
        h_ref[...] = hb
        r = _nt_dot(hb, ws_ref[...])
        ikd_ref[...] = r[:, :LANES].astype(ikd_ref.dtype)
        sm_ref[...] = r[:, LANES:]

    acc = _nt_dot(h_ref[...], w_ref[...])
    u_ref[...] = (acc * cs_ref[...]).astype(u_ref.dtype)


def _proj(x2, g_pre, w_main, w_small, col_scale, tm, tn):
    m = x2.shape[0]
    grid = (m // tm, U_COLS // tn)
    return pl.pallas_call(
        _proj_kernel,
        out_shape=(jax.ShapeDtypeStruct((m, U_COLS), ACT_DTYPE),
                   jax.ShapeDtypeStruct((m, LANES), ACT_DTYPE),
                   jax.ShapeDtypeStruct((m, LANES), jnp.float32)),
        grid=grid,
        in_specs=[
            pl.BlockSpec((tm, D_MODEL), lambda i, j: (i, 0)),
            pl.BlockSpec((1, D_MODEL), lambda i, j: (0, 0)),
            pl.BlockSpec((tn, D_MODEL), lambda i, j: (j, 0)),
            pl.BlockSpec((2 * LANES, D_MODEL), lambda i, j: (0, 0)),
            pl.BlockSpec((1, tn), lambda i, j: (0, j)),
        ],
        out_specs=(
            pl.BlockSpec((tm, tn), lambda i, j: (i, j)),
            pl.BlockSpec((tm, LANES), lambda i, j: (i, 0)),
            pl.BlockSpec((tm, LANES), lambda i, j: (i, 0)),
        ),
        scratch_shapes=[pltpu.VMEM((tm, D_MODEL), MXU_DTYPE)],
        compiler_params=pltpu.CompilerParams(
            dimension_semantics=("arbitrary", "arbitrary"),
            vmem_limit_bytes=VMEM_LIMIT),
        name="proj",
    )(x2, g_pre, w_main, w_small, col_scale)


def _split3(x):
    hi = x.astype(MXU_DTYPE)
    r1 = x - hi.astype(jnp.float32)
    mid = r1.astype(MXU_DTYPE)
    lo = (r1 - mid.astype(jnp.float32)).astype(MXU_DTYPE)
    return hi, mid, lo


def _gla_kernel(q_ref, k_ref, v_ref, z_ref, sm_ref, wup_ref, ba_ref, gg_ref,
                o_ref, st_ref, qd_ref, kd_ref, kt_ref, b_ref, oacc_ref, *, chunks_per_step):
    c = GLA_CHUNK
    rows = chunks_per_step * c
    heads = range(GLA_HEADS)

    @pl.when(pl.program_id(1) == 0)
    def _():
        st_ref[...] = jnp.zeros_like(st_ref)

    def ks(h):
        return slice(h * GLA_DK, (h + 1) * GLA_DK)

    def vs(h):
        return slice(h * GLA_DV, (h + 1) * GLA_DV)

    def chunk(i):
        return slice(i * c, (i + 1) * c)

    row = lax.broadcasted_iota(jnp.int32, (rows, rows), 0)
    col = lax.broadcasted_iota(jnp.int32, (rows, rows), 1)
    causal = (row // c == col // c) & (col <= row)
    tril = jnp.where(causal, 1.0, 0.0).astype(MXU_DTYPE)

    ga = sm_ref[:, SMALL_GA:SMALL_GA + GLA_RANK].astype(MXU_DTYPE)
    pre = _dot(ga, wup_ref[...]) + ba_ref[...]
    log_a = (jnp.minimum(pre, 0.0) - jnp.log1p(jnp.exp(-jnp.abs(pre)))) / GLA_GATE_NORM
    hi, mid, lo = _split3(log_a)
    b_ref[...] = _dot(tril, hi) + _dot(tril, mid) + _dot(tril, lo)

    decay = []
    for i in range(chunks_per_step):
        b_last = b_ref[i * c + c - 1:i * c + c, :]
        decay.append(jnp.exp(b_last))
        for h in heads:
            b = b_ref[chunk(i), ks(h)]
            q = q_ref[chunk(i), ks(h)].astype(jnp.float32)
            k = k_ref[chunk(i), ks(h)].astype(jnp.float32)
            qd_ref[chunk(i), ks(h)] = (q * jnp.exp(b)).astype(MXU_DTYPE)
            kd_ref[chunk(i), ks(h)] = (k * jnp.exp(-b)).astype(MXU_DTYPE)
            kt_ref[chunk(i), ks(h)] = (k * jnp.exp(b_last[:, ks(h)] - b)).astype(MXU_DTYPE)

    att = [_nt_dot(qd_ref[:, ks(h)], kd_ref[:, ks(h)]) for h in heads]
    att = [jnp.where(causal, a, 0.0).astype(MXU_DTYPE) for a in att]
    for h in heads:
        oacc_ref[:, vs(h)] = _dot(att[h], v_ref[:, vs(h)])

    st = [st_ref[h] for h in heads]
    for i in range(chunks_per_step):
        for h in heads:
            kv = _tn_dot(v_ref[chunk(i), vs(h)], kt_ref[chunk(i), ks(h)])
            oacc_ref[chunk(i), vs(h)] += _nt_dot(qd_ref[chunk(i), ks(h)], st[h].astype(MXU_DTYPE))
            st[h] = st[h] * decay[i][:, ks(h)] + kv
    for h in heads:
        st_ref[h] = st[h]
        oh = oacc_ref[:, vs(h)]
        y = oh * lax.rsqrt(jnp.mean(oh * oh, axis=-1, keepdims=True) + EPS)
        y = y * gg_ref[...]
        zz = z_ref[:, vs(h)].astype(jnp.float32)
        o_ref[:, vs(h)] = (y * _silu(zz)).astype(o_ref.dtype)


def _gla(u, small, w_up, b_a, g_gla, batch, seq, chunks_per_step):
    rows = chunks_per_step * GLA_CHUNK
    steps = seq // rows
    hk = GLA_HEADS * GLA_DK
    hv = GLA_HEADS * GLA_DV

    def rmap(cb):
        return lambda b, s: (b * steps + s, cb)

    return pl.pallas_call(
        functools.partial(_gla_kernel, chunks_per_step=chunks_per_step),
        out_shape=jax.ShapeDtypeStruct((batch * seq, hv), ACT_DTYPE),
        grid=(batch, steps),
        in_specs=[
            pl.BlockSpec((rows, hk), rmap(COL_GQ // hk)),
            pl.BlockSpec((rows, hk), rmap(COL_GK // hk)),
            pl.BlockSpec((rows, hv), rmap(COL_GV // hv)),
            pl.BlockSpec((rows, hv), rmap(COL_GZ // hv)),
            pl.BlockSpec((rows, LANES), rmap(0)),
            pl.BlockSpec((GLA_RANK, hk), lambda b, s: (0, 0)),
            pl.BlockSpec((1, hk), lambda b, s: (0, 0)),
            pl.BlockSpec((1, GLA_DV), lambda b, s: (0, 0)),
        ],
        out_specs=pl.BlockSpec((rows, hv), rmap(0)),
        scratch_shapes=[pltpu.VMEM((GLA_HEADS, GLA_DV, GLA_DK), jnp.float32),
                        pltpu.VMEM((rows, hk), MXU_DTYPE),
                        pltpu.VMEM((rows, hk), MXU_DTYPE),
                        pltpu.VMEM((rows, hk), MXU_DTYPE),
                        pltpu.VMEM((rows, hk), jnp.float32),
                        pltpu.VMEM((rows, hv), jnp.float32)],
        compiler_params=pltpu.CompilerParams(
            dimension_semantics=("arbitrary", "arbitrary"),
            vmem_limit_bytes=VMEM_LIMIT),
        name="gla",
    )(u, u, u, u, small, w_up, b_a, g_gla)


KEY_CHUNK = 4 * Q_BLOCK
BLOCKS_PER_CHUNK = KEY_CHUNK // Q_BLOCK
PLANE_GROUP = 32 * SUBLANES


def _t5_bucket(dist):
    max_exact = REL_BUCKETS // 2
    d = jnp.maximum(dist, 1).astype(jnp.float32)
    large = max_exact + jnp.floor(jnp.log(d / max_exact) / math.log(REL_MAX_DIST / max_exact)
                                  * (REL_BUCKETS - max_exact)).astype(jnp.int32)
    large = jnp.minimum(large, REL_BUCKETS - 1)
    return jnp.where(dist < max_exact, dist, large)


def _dsa_kernel(rb_ref, dq_ref, dk_ref, dv_ref, iq_ref, ikd_ref, sm_ref, dz_ref,
                out_ref,
                kt_ref, sc_ref, planes_ref, mb_ref, vt_ref, bt_ref, iqm_ref, qaug_ref, acc_ref,
                lga_ref, lgb_ref, mxa_ref, mxb_ref, ml_ref,
                *, seq):
    qb = pl.program_id(1)
    blk = Q_BLOCK
    ch = KEY_CHUNK
    k_sel = min(TOPK_MAX, seq // 4)
    n_chunks = qb // BLOCKS_PER_CHUNK + 1
    width = DSA_GROUPS * blk

    row_i = lax.broadcasted_iota(jnp.int32, (blk, blk), 0)
    col_i = lax.broadcasted_iota(jnp.int32, (blk, blk), 1)
    crow_i = lax.broadcasted_iota(jnp.int32, (ch, blk), 0)
    q_pos = qb * blk + lax.broadcasted_iota(jnp.int32, (ch, blk), 1)

    @pl.when((pl.program_id(0) == 0) & (qb == 0))
    def _():
        planes_ref[...] = jnp.zeros_like(planes_ref)
        kt_ref[...] = jnp.zeros_like(kt_ref)
        for delta in range(3):
            dist = jnp.maximum(delta * blk + col_i - row_i, 0)
            bucket = _t5_bucket(dist)
            for h in range(DSA_HEADS):
                tile = jnp.zeros((blk, blk), jnp.float32)
                for bk in range(REL_BUCKETS):
                    tile = jnp.where(bucket == bk, rb_ref[bk, h] * LOG2_E, tile)
                cc, g = divmod(h, DSA_GROUPS)
                bt_ref[delta, cc, :, g * blk:(g + 1) * blk] = tile

    @pl.when(qb == 0)
    def _():
        def body(i, carry):
            r = pl.multiple_of(i * ch, ch)
            vt_ref[i] = dv_ref[pl.ds(r, ch), :].astype(jnp.float32).T.astype(vt_ref.dtype)
            return carry
        lax.fori_loop(0, seq // ch, body, 0)

    w_t = sm_ref[...].T[SMALL_IW:SMALL_IW + IDX_HEADS, :]
    w_t = w_t * (IDX_HEADS ** -0.5) * (IDX_DIM ** -0.5)

    lane = lax.broadcasted_iota(jnp.int32, (blk, LANES), 1)
    for h in range(IDX_HEADS):
        pair = iq_ref[:, (h // 2) * LANES:(h // 2 + 1) * LANES]
        keep = (lane < IDX_DIM) if h % 2 == 0 else (lane >= IDX_DIM)
        iqm_ref[h // 2, (h % 2) * blk:(h % 2 + 1) * blk, :] = jnp.where(keep, pair, jnp.zeros_like(pair))

    def score_chunk(c, causal):
        r = pl.multiple_of(c * ch, ch)
        ikc = ikd_ref[pl.ds(r, ch), :]
        acc = None
        for hp in range(IDX_HEADS // 2):
            s2 = _nt_dot(ikc, iqm_ref[hp])
            for h in (2 * hp, 2 * hp + 1):
                t = w_t[h:h + 1, :] * jnp.maximum(s2[:, (h % 2) * blk:(h % 2 + 1) * blk], 0.0)
                acc = t if acc is None else acc + t
        if causal:
            acc = jnp.where(r + crow_i <= q_pos, acc, -jnp.inf)
        sc_ref[pl.ds(r, ch), :] = acc
        bits = lax.bitcast_convert_type(acc, jnp.int32)
        kt_ref[pl.ds(r, ch), :] = bits ^ ((bits >> 31) & 0x7FFFFFFF)

    def build_planes(c):
        for g in range(ch // PLANE_GROUP):
            base = c * ch + g * PLANE_GROUP
            a = [kt_ref[pl.ds(pl.multiple_of(base + SUBLANES * v, SUBLANES), SUBLANES), :]
                 for v in range(32)]
            j, m = 16, 0x0000FFFF
            while j:
                k0 = 0
                while k0 < 32:
                    t = (a[k0] ^ lax.shift_right_logical(a[k0 + j], jnp.int32(j))) & m
                    a[k0] = a[k0] ^ t
                    a[k0 + j] = a[k0 + j] ^ (t << j)
                    k0 = (k0 + j + 1) & ~j
                j >>= 1
                m = (m ^ (m << j)) & 0xFFFFFFFF if j else m
                m = m - (1 << 32) if m >= (1 << 31) else m
            a[0] = ~a[0]
            row = pl.multiple_of(c * (ch // 32) + g * SUBLANES, SUBLANES)
            for jj in range(32):
                planes_ref[jj, pl.ds(row, SUBLANES), :] = a[jj]

    @pl.when(n_chunks > 1)
    def _():
        score_chunk(0, False)

    def score_body(c, carry):
        build_planes(c - 1)
        score_chunk(c, False)
        return carry
    lax.fori_loop(1, n_chunks - 1, score_body, 0)
    build_planes(jnp.maximum(n_chunks - 2, 0))
    score_chunk(n_chunks - 1, True)
    build_planes(n_chunks - 1)

    n_rows = seq // 32

    def rowsum(x):
        part = jnp.sum(x.reshape(n_rows // SUBLANES, SUBLANES, blk), axis=0)
        return jnp.sum(part, axis=0, keepdims=True)

    prow = lax.broadcasted_iota(jnp.int32, (n_rows, blk), 0)
    alive0 = jnp.where(prow < n_chunks * (ch // 32), -1, 0).astype(jnp.int32)

    def bit_body(j, carry):
        alive, cnt_gt, ukey = carry
        w = planes_ref[j]
        ones = alive & w
        c1 = rowsum(lax.population_count(ones))
        take = cnt_gt + c1 >= k_sel
        alive = jnp.where(take, ones, alive & ~w)
        cnt_gt = jnp.where(take, cnt_gt, cnt_gt + c1)
        ukey = jnp.where(take, ukey | (jnp.int32(1) << (31 - j)), ukey)
        return alive, cnt_gt, ukey

    zero_row = jnp.zeros((1, blk), jnp.int32)
    alive, cnt_gt, ukey = lax.fori_loop(0, 32, bit_body, (alive0, zero_row, zero_row))
    ans = ukey ^ INT_MIN
    thr_bits = jnp.where(ans < 0, ans ^ 0x7FFFFFFF, ans)
    thr0 = lax.bitcast_convert_type(thr_bits, jnp.float32)

    def tile_sum(x):
        return jnp.sum(x.reshape(ch // SUBLANES, SUBLANES, blk), axis=0)

    def fold_rows(acc, combine):
        return functools.reduce(combine, [acc[i:i + 1] for i in range(SUBLANES)])

    def mask_pass(thr, p_max):
        def body(c, carry):
            a_gt, a_ge = carry
            r = pl.multiple_of(c * ch, ch)
            sc = sc_ref[pl.ds(r, ch), :]
            k_pos = r + crow_i
            gt = sc > thr
            ge = sc >= thr
            sel = (gt | (ge & (k_pos <= p_max))) & (k_pos <= q_pos)
            mb_ref[pl.ds(r, ch), :] = jnp.where(sel, 0.0, MASK_NEG).astype(mb_ref.dtype)
            return (a_gt + tile_sum(jnp.where(gt, 1, 0).astype(jnp.int32)),
                    a_ge + tile_sum(jnp.where(ge, 1, 0).astype(jnp.int32)))
        zero = jnp.zeros((SUBLANES, blk), jnp.int32)
        a_gt, a_ge = lax.fori_loop(0, n_chunks, body, (zero, zero))
        return fold_rows(a_gt, jnp.add), fold_rows(a_ge, jnp.add)

    def nearest(pred_fn, fill, combine):
        def body(c, acc):
            r = pl.multiple_of(c * ch, ch)
            sc = sc_ref[pl.ds(r, ch), :]
            part = jnp.where(pred_fn(sc), sc, fill).reshape(ch // SUBLANES, SUBLANES, blk)
            return combine(acc, functools.reduce(combine, [part[i] for i in range(ch // SUBLANES)]))
        acc = lax.fori_loop(0, n_chunks, body, jnp.full((SUBLANES, blk), fill, jnp.float32))
        return fold_rows(acc, combine)

    def status(c_gt, c_ge):
        off = (c_gt >= k_sel) | (c_ge < k_sel)
        return jnp.max(jnp.where(off, 2, 0) | jnp.where(c_ge > k_sel, 1, 0))

    def walk(state):
        thr, c_gt, c_ge, _ = state
        above = nearest(lambda sc: sc > thr, jnp.inf, jnp.minimum)
        below = nearest(lambda sc: sc < thr, -jnp.inf, jnp.maximum)
        thr = jnp.where(c_gt >= k_sel, above, jnp.where(c_ge < k_sel, below, thr))
        c_gt, c_ge = mask_pass(thr, no_bound)
        return thr, c_gt, c_ge, status(c_gt, c_ge)

    no_bound = jnp.full((1, blk), 2 * seq, jnp.int32)
    cnt_gt0, cnt_ge0 = mask_pass(thr0, no_bound)
    thr, cnt_gt, cnt_ge, flag = lax.while_loop(
        lambda state: state[3] >= 2, walk, (thr0, cnt_gt0, cnt_ge0, status(cnt_gt0, cnt_ge0)))

    @pl.when(flag == 1)
    def _():
        need = k_sel - cnt_gt
        n_bits = (2 * seq - 1).bit_length()

        def tie_count(cand):
            def body(c, acc):
                r = pl.multiple_of(c * ch, ch)
                hit = (sc_ref[pl.ds(r, ch), :] == thr) & (r + crow_i < cand)
                return acc + tile_sum(jnp.where(hit, 1, 0).astype(jnp.int32))
            return fold_rows(lax.fori_loop(0, n_chunks, body, jnp.zeros((SUBLANES, blk), jnp.int32)), jnp.add)

        def pos_body(i, p):
            cand = p | (jnp.int32(1) << (n_bits - 1 - i))
            return jnp.where(tie_count(cand) <= need - 1, cand, p)
        p_max = lax.fori_loop(0, n_bits, pos_body, jnp.zeros((1, blk), jnp.int32))
        mask_pass(thr, p_max)

    eye = jnp.where(row_i == col_i, 1.0, 0.0).astype(MXU_DTYPE)
    for cc in range(DSA_KV_HEADS):
        for g in range(DSA_GROUPS):
            h = cc * DSA_GROUPS + g
            qaug_ref[cc, g * blk:(g + 1) * blk, 0:DSA_HD] = dq_ref[:, h * DSA_HD:(h + 1) * DSA_HD]
            qaug_ref[cc, g * blk:(g + 1) * blk, DSA_HD:2 * DSA_HD] = eye
    acc_ref[...] = jnp.zeros_like(acc_ref)
    for cc in range(DSA_KV_HEADS):
        ml_ref[cc, 0] = jnp.full(ml_ref.shape[2:], -jnp.inf, jnp.float32)
        ml_ref[cc, 1] = jnp.zeros(ml_ref.shape[2:], jnp.float32)

    def stage_logits(c, lg_ref, mx_ref, far):
        r = pl.multiple_of(c * ch, ch)
        mbc = mb_ref[pl.ds(r, ch), :]
        for cc in range(DSA_KV_HEADS):
            kaug = jnp.concatenate([dk_ref[pl.ds(r, ch), cc * DSA_HD:(cc + 1) * DSA_HD], mbc], axis=1)
            lg = _nt_dot(kaug, qaug_ref[cc])
            if far:
                off = bt_ref[2, cc, 0:1, :]
            else:
                lg = lg + jnp.concatenate(
                    [bt_ref[jnp.clip(qb - (c * BLOCKS_PER_CHUNK + j), 0, 2), cc]
                     for j in range(BLOCKS_PER_CHUNK)], axis=0)
                off = jnp.zeros((1, width), jnp.float32)
            lg_ref[cc] = lg
            mx_ref[cc, 0] = jnp.broadcast_to(jnp.max(lg, axis=0, keepdims=True) + off, mx_ref.shape[2:])
            mx_ref[cc, 1] = jnp.broadcast_to(off, mx_ref.shape[2:])

    def stage_softmax(c, lg_ref, mx_ref):
        for cc in range(DSA_KV_HEADS):
            m = ml_ref[cc, 0, 0:1, :]
            l = ml_ref[cc, 1, 0:1, :]
            m_new = jnp.maximum(m, mx_ref[cc, 0, 0:1, :])
            alpha = jnp.exp2(m - m_new)
            p = jnp.exp2(lg_ref[cc] - (m_new - mx_ref[cc, 1, 0:1, :]))
            l_new = alpha * l + jnp.sum(p, axis=0, keepdims=True)
            pv = _dot(vt_ref[c, cc * DSA_HD:(cc + 1) * DSA_HD, :], p.astype(MXU_DTYPE))
            acc_ref[cc] = acc_ref[cc] * alpha + pv
            ml_ref[cc, 0] = jnp.broadcast_to(m_new, ml_ref.shape[2:])
            ml_ref[cc, 1] = jnp.broadcast_to(l_new, ml_ref.shape[2:])

    c_near = jnp.maximum(qb - 1, 0) // BLOCKS_PER_CHUNK

    @pl.when(c_near > 0)
    def _():
        stage_logits(0, lga_ref, mxa_ref, far=True)

    @pl.when(c_near == 0)
    def _():
        stage_logits(0, lga_ref, mxa_ref, far=False)

    def pair_body(pi, carry, far):
        c0 = 2 * pi
        stage_logits(c0 + 1, lgb_ref, mxb_ref, far)
        stage_softmax(c0, lga_ref, mxa_ref)

        stage_logits(jnp.minimum(c0 + 2, n_chunks - 1), lga_ref, mxa_ref, far)
        stage_softmax(c0 + 1, lgb_ref, mxb_ref)
        return carry

    far_pairs = jnp.maximum(c_near - 1, 0) // 2
    lax.fori_loop(0, far_pairs, functools.partial(pair_body, far=True), 0)
    lax.fori_loop(far_pairs, n_chunks // 2, functools.partial(pair_body, far=False), 0)

    @pl.when(n_chunks % 2 == 1)
    def _():
        stage_softmax(n_chunks - 1, lga_ref, mxa_ref)

    for cc in range(DSA_KV_HEADS):
        o_t = acc_ref[cc] / ml_ref[cc, 1, 0:1, :]
        for g in range(DSA_GROUPS):
            h = cc * DSA_GROUPS + g
            o = o_t[:, g * blk:(g + 1) * blk].T
            zz = dz_ref[:, h * DSA_HD:(h + 1) * DSA_HD].astype(jnp.float32)
            out_ref[:, h * DSA_HD:(h + 1) * DSA_HD] = (o * _silu(zz)).astype(out_ref.dtype)


def _dsa(u, ikd, small, rel_bias, batch, seq):
    nb = seq // Q_BLOCK
    hq = DSA_HEADS * DSA_HD
    hkv = DSA_KV_HEADS * DSA_HD
    hi = IDX_HEADS * IDX_DIM

    def qmap(cb):
        return lambda b, q: (b * nb + q, cb)

    def bmap(cb):
        return lambda b, q: (b, cb)

    return pl.pallas_call(
        functools.partial(_dsa_kernel, seq=seq),
        out_shape=jax.ShapeDtypeStruct((batch * seq, hq), ACT_DTYPE),
        grid=(batch, nb),
        in_specs=[
            pl.BlockSpec(memory_space=pltpu.SMEM),
            pl.BlockSpec((Q_BLOCK, hq), qmap(COL_DQ // hq)),
            pl.BlockSpec((seq, hkv), bmap(COL_DK // hkv)),
            pl.BlockSpec((seq, hkv), bmap(COL_DV // hkv)),
            pl.BlockSpec((Q_BLOCK, hi), qmap(COL_IQ // hi)),
            pl.BlockSpec((seq, LANES), bmap(0)),
            pl.BlockSpec((Q_BLOCK, LANES), qmap(0)),
            pl.BlockSpec((Q_BLOCK, hq), qmap(COL_DZ // hq)),
        ],
        out_specs=pl.BlockSpec((Q_BLOCK, hq), qmap(0)),
        scratch_shapes=[
            pltpu.VMEM((seq, Q_BLOCK), jnp.int32),
            pltpu.VMEM((seq, Q_BLOCK), jnp.float32),
            pltpu.VMEM((32, seq // 32, Q_BLOCK), jnp.int32),
            pltpu.VMEM((seq, Q_BLOCK), MXU_DTYPE),
            pltpu.VMEM((seq // KEY_CHUNK, hkv, KEY_CHUNK), MXU_DTYPE),
            pltpu.VMEM((3, DSA_KV_HEADS, Q_BLOCK, DSA_GROUPS * Q_BLOCK), jnp.float32),
            pltpu.VMEM((IDX_HEADS // 2, 2 * Q_BLOCK, LANES), MXU_DTYPE),
            pltpu.VMEM((DSA_KV_HEADS, DSA_GROUPS * Q_BLOCK, 2 * DSA_HD), MXU_DTYPE),
            pltpu.VMEM((DSA_KV_HEADS, DSA_HD, DSA_GROUPS * Q_BLOCK), jnp.float32),
            pltpu.VMEM((DSA_KV_HEADS, KEY_CHUNK, DSA_GROUPS * Q_BLOCK), jnp.float32),
            pltpu.VMEM((DSA_KV_HEADS, KEY_CHUNK, DSA_GROUPS * Q_BLOCK), jnp.float32),
            pltpu.VMEM((DSA_KV_HEADS, 2, SUBLANES, DSA_GROUPS * Q_BLOCK), jnp.float32),
            pltpu.VMEM((DSA_KV_HEADS, 2, SUBLANES, DSA_GROUPS * Q_BLOCK), jnp.float32),
            pltpu.VMEM((DSA_KV_HEADS, 2, SUBLANES, DSA_GROUPS * Q_BLOCK), jnp.float32),
        ],
        compiler_params=pltpu.CompilerParams(
            dimension_semantics=("arbitrary", "arbitrary"),
            vmem_limit_bytes=VMEM_LIMIT),
        name="dsa",
    )(rel_bias, u, u, u, u, ikd, small, u)


def _memkv_kernel(mem_ref, g_ref, w_ref, o_ref):
    xf = mem_ref[...]
    y = xf * lax.rsqrt(jnp.mean(xf * xf, axis=-1, keepdims=True) + EPS)
    hb = (y * g_ref[...]).astype(MXU_DTYPE)
    o_ref[...] = _dot(hb, w_ref[...]).astype(o_ref.dtype)


def _memkv(mem2, g_mem, w_kv, batch):
    n = 2 * X_HEADS * X_HD
    return pl.pallas_call(
        _memkv_kernel,
        out_shape=jax.ShapeDtypeStruct((batch * N_MEM, n), ACT_DTYPE),
        grid=(batch,),
        in_specs=[
            pl.BlockSpec((N_MEM, D_MODEL), lambda b: (b, 0)),
            pl.BlockSpec((1, D_MODEL), lambda b: (0, 0)),
            pl.BlockSpec((D_MODEL, n), lambda b: (0, 0)),
        ],
        out_specs=pl.BlockSpec((N_MEM, n), lambda b: (b, 0)),
        compiler_params=pltpu.CompilerParams(
            dimension_semantics=("arbitrary",),
            vmem_limit_bytes=VMEM_LIMIT),
        name="memkv",
    )(mem2, g_mem, w_kv)


def _merge_kernel(x_ref, yg_ref, yd_ref, xq_ref, xz_ref, sg_ref, sd_ref, sm_ref, mkv_ref,
                  wg_ref, wd_ref, wx_ref, wo_ref, gp_ref, o_ref, ym_ref):
    hw = X_HEADS * X_HD
    heads = range(X_HEADS)

    def cs(h):
        return slice(h * X_HD, (h + 1) * X_HD)

    lgs = [_nt_dot(xq_ref[:, cs(h)], mkv_ref[:, cs(h)]) for h in heads]
    t_g = _dot(yg_ref[...], wg_ref[...])
    ps = []
    for lg in lgs:
        e = jnp.exp(lg - jnp.max(lg, axis=-1, keepdims=True))
        ps.append((e / jnp.sum(e, axis=-1, keepdims=True)).astype(MXU_DTYPE))
    outs = [_dot(ps[h], mkv_ref[:, hw + h * X_HD:hw + (h + 1) * X_HD]) for h in heads]
    t_d = _dot(yd_ref[...], wd_ref[...])
    for h in heads:
        ym_ref[:, cs(h)] = (outs[h] * _silu(xz_ref[:, cs(h)].astype(jnp.float32))).astype(ym_ref.dtype)
    merged = jax.nn.sigmoid(sg_ref[...].astype(jnp.float32)) * t_g
    merged = merged + jax.nn.sigmoid(sd_ref[...].astype(jnp.float32)) * t_d
    merged = merged + jax.nn.sigmoid(sm_ref[...].astype(jnp.float32)) * _dot(ym_ref[...], wx_ref[...])
    t = _dot(merged.astype(MXU_DTYPE), wo_ref[...])
    y = t * lax.rsqrt(jnp.mean(t * t, axis=-1, keepdims=True) + EPS)
    o_ref[...] = x_ref[...] + y * gp_ref[...]


def _merge(x2, y_gla, y_dsa, u, mkv, w_g, w_d, w_x, w_o, g_post, seq, tm):
    m = x2.shape[0]
    steps_per_batch = seq // tm
    d = D_MODEL

    def rmap(cb):
        return lambda i: (i, cb)

    wspec = pl.BlockSpec((d, d), lambda i: (0, 0))
    return pl.pallas_call(
        _merge_kernel,
        out_shape=jax.ShapeDtypeStruct((m, d), jnp.float32),
        grid=(m // tm,),
        in_specs=[
            pl.BlockSpec((tm, d), rmap(0)),
            pl.BlockSpec((tm, d), rmap(0)),
            pl.BlockSpec((tm, d), rmap(0)),
            pl.BlockSpec((tm, d), rmap(COL_XQ // d)),
            pl.BlockSpec((tm, d), rmap(COL_XZ // d)),
            pl.BlockSpec((tm, d), rmap(COL_SG // d)),
            pl.BlockSpec((tm, d), rmap(COL_SD // d)),
            pl.BlockSpec((tm, d), rmap(COL_SM // d)),
            pl.BlockSpec((N_MEM, 2 * X_HEADS * X_HD), lambda i: (i // steps_per_batch, 0)),
            wspec, wspec, wspec, wspec,
            pl.BlockSpec((1, d), lambda i: (0, 0)),
        ],
        out_specs=pl.BlockSpec((tm, d), rmap(0)),
        scratch_shapes=[pltpu.VMEM((tm, d), MXU_DTYPE)],
        compiler_params=pltpu.CompilerParams(
            dimension_semantics=("arbitrary",),
            vmem_limit_bytes=VMEM_LIMIT),
        name="merge",
    )(x2, y_gla, y_dsa, u, u, u, u, u, mkv, w_g, w_d, w_x, w_o, g_post)


_MAIN_GROUPS = (0, 1, 2, 4, 5, 11, 12, 13, 14, 6, 7, 8)
_GROUP_GA, _GROUP_IK, _GROUP_IW = 3, 9, 10
RELAYOUT_COLS = 128


def _relayout_kernel(wt_ref, main_ref, small_ref):
    offs = np.concatenate([[0], np.cumsum(np.array(SPLIT_SIZES))]).tolist()

    def group(g):
        return wt_ref[offs[g]:offs[g + 1], :].astype(MXU_DTYPE)

    dst = 0
    for g in _MAIN_GROUPS:
        main_ref[dst:dst + SPLIT_SIZES[g], :] = group(g)
        dst += SPLIT_SIZES[g]
    pad = jnp.zeros((LANES - GLA_RANK - IDX_HEADS, wt_ref.shape[1]), MXU_DTYPE)
    small_ref[...] = jnp.concatenate(
        [group(_GROUP_IK), group(_GROUP_IK), group(_GROUP_GA), group(_GROUP_IW), pad], axis=0)


def _relayout_w_in(w_in):
    wt = w_in.T
    n, d = wt.shape
    return pl.pallas_call(
        _relayout_kernel,
        out_shape=(jax.ShapeDtypeStruct((U_COLS, d), MXU_DTYPE),
                   jax.ShapeDtypeStruct((2 * LANES, d), MXU_DTYPE)),
        grid=(d // RELAYOUT_COLS,),
        in_specs=[pl.BlockSpec((n, RELAYOUT_COLS), lambda i: (0, i))],
        out_specs=(pl.BlockSpec((U_COLS, RELAYOUT_COLS), lambda i: (0, i)),
                   pl.BlockSpec((2 * LANES, RELAYOUT_COLS), lambda i: (0, i))),
        compiler_params=pltpu.CompilerParams(
            dimension_semantics=("arbitrary",),
            vmem_limit_bytes=VMEM_LIMIT),
        name="relayout",
    )(wt)


def _col_scale():
    s = np.ones((1, U_COLS), np.float32)
    s[:, COL_GQ:COL_GQ + GLA_HEADS * GLA_DK] = GLA_DK ** -0.5
    s[:, COL_DQ:COL_DQ + DSA_HEADS * DSA_HD] = DSA_HD ** -0.5 * LOG2_E
    s[:, COL_XQ:COL_XQ + X_HEADS * X_HD] = X_HD ** -0.5
    return jnp.asarray(s)


def _layer(x2, mem2, g_pre, g_post, g_mem, w_in, w_up, b_a, g_gla, rel_bias, w_kv,
           w_g, w_d, w_x, w_o, batch, seq):
    w_main, w_small = _relayout_w_in(w_in)
    tm = min(PROJ_TM, batch * seq)
    u, ikd, small = _proj(x2, g_pre.reshape(1, -1), w_main, w_small, _col_scale(), tm, PROJ_TN)
    y_gla = _gla(u, small, w_up.astype(MXU_DTYPE), b_a.reshape(1, -1), g_gla.reshape(1, -1),
                 batch, seq, chunks_per_step=4)
    y_dsa = _dsa(u, ikd, small, rel_bias, batch, seq)
    mkv = _memkv(mem2, g_mem.reshape(1, -1), w_kv.astype(MXU_DTYPE), batch)
    return _merge(x2, y_gla, y_dsa, u, mkv, w_g.astype(MXU_DTYPE), w_d.astype(MXU_DTYPE),
                  w_x.astype(MXU_DTYPE), w_o.astype(MXU_DTYPE), g_post.reshape(1, -1), seq, MERGE_TM)


def kernel(x, mem, g_pre, g_post, g_mem, w_in, w_gla_a_up, b_gla_a, g_gla, rel_bias,
           w_mem_kv, w_gla_out, w_dsa_out, w_x_out, w_o):
    batch, seq, d = x.shape
    x2 = x.reshape(batch * seq, d)
    mem2 = mem.reshape(batch * N_MEM, d)
    for i in range(g_pre.shape[0]):
        x2 = _layer(x2, mem2, g_pre[i], g_post[i], g_mem[i], w_in[i], w_gla_a_up[i],
                    b_gla_a[i], g_gla[i], rel_bias, w_mem_kv[i], w_gla_out[i],
                    w_dsa_out[i], w_x_out[i], w_o[i], batch, seq)
    return x2.reshape(batch, seq, d)
```

```python
import functools
import math

import jax
import jax.numpy as jnp
import numpy as np
from jax import lax
from jax.experimental import pallas as pl
from jax.experimental.pallas import tpu as pltpu

D_MODEL = 1024
N_MEM = 256
EPS = 1e-6
GLA_HEADS = 4
GLA_DK = 128
GLA_DV = 256
GLA_RANK = 16
GLA_GATE_NORM = 16.0
GLA_CHUNK = 64
DSA_HEADS = 8
DSA_KV_HEADS = 2
DSA_GROUPS = DSA_HEADS // DSA_KV_HEADS
DSA_HD = 128
IDX_HEADS = 8
IDX_DIM = 64
TOPK_MAX = 256
Q_BLOCK = 128
REL_BUCKETS = 32
REL_MAX_DIST = 128
X_HEADS = 4
X_HD = 256

SPLIT_SIZES = (512, 512, 1024, 16, 1024, 1024, 256, 256, 512, 64, 8, 1024, 1024, 1024, 3072)

LANES = 128
SUBLANES = 8

MXU_DTYPE = jnp.bfloat16
ACT_DTYPE = jnp.bfloat16

U_COLS = 11264
COL_GQ, COL_GK, COL_GV, COL_GZ = 0, 512, 1024, 2048
COL_DQ, COL_DZ, COL_XQ, COL_XZ = 3072, 4096, 5120, 6144
COL_SG, COL_SD, COL_SM = 7168, 8192, 9216
COL_DK, COL_DV, COL_IQ = 10240, 10496, 10752
SMALL_GA, SMALL_IW = 0, 16

MASK_NEG = -1e30
LOG2_E = math.log2(math.e)
INT_MIN = -(2 ** 31)
VMEM_LIMIT = 56 * 1024 * 1024

PROJ_TM = 1024
PROJ_TN = U_COLS // 4
MERGE_TM = 512


def _nt_dot(a, b):
    return lax.dot_general(a, b, (((1,), (1,)), ((), ())),
                           preferred_element_type=jnp.float32)


def _tn_dot(a, b):
    return lax.dot_general(a, b, (((0,), (0,)), ((), ())),
                           preferred_element_type=jnp.float32)


def _dot(a, b):
    return jnp.dot(a, b, preferred_element_type=jnp.float32)


def _silu(z):
    return z * jax.nn.sigmoid(z)


def _proj_kernel(x_ref, g_ref, w_ref, ws_ref, cs_ref, u_ref, ikd_ref, sm_ref, h_ref):
    @pl.when(pl.program_id(1) == 0)
    def _():
        xf = x_ref[...]
        y = xf * lax.rsqrt(jnp.mean(xf * xf, axis=-1, keepdims=True) + EPS)
        hb = (y * g_ref[...]).astype(MXU_DTYPE)
        h_ref[...] = hb
        r = _nt_dot(hb, ws_ref[...])
        ikd_ref[...] = r[:, :LANES].astype(ikd_ref.dtype)
        sm_ref[...] = r[:, LANES:]

    acc = _nt_dot(h_ref[...], w_ref[...])
    u_ref[...] = (acc * cs_ref[...]).astype(u_ref.dtype)


def _proj(x2, g_pre, w_main, w_small, col_scale, tm, tn):
    m = x2.shape[0]
    grid = (m // tm, U_COLS // tn)
    return pl.pallas_call(
        _proj_kernel,
        out_shape=(jax.ShapeDtypeStruct((m, U_COLS), ACT_DTYPE),
                   jax.ShapeDtypeStruct((m, LANES), ACT_DTYPE),
                   jax.ShapeDtypeStruct((m, LANES), jnp.float32)),
        grid=grid,
        in_specs=[
            pl.BlockSpec((tm, D_MODEL), lambda i, j: (i, 0)),
            pl.BlockSpec((1, D_MODEL), lambda i, j: (0, 0)),
            pl.BlockSpec((tn, D_MODEL), lambda i, j: (j, 0)),
            pl.BlockSpec((2 * LANES, D_MODEL), lambda i, j: (0, 0)),
            pl.BlockSpec((1, tn), lambda i, j: (0, j)),
        ],
        out_specs=(
            pl.BlockSpec((tm, tn), lambda i, j: (i, j)),
            pl.BlockSpec((tm, LANES), lambda i, j: (i, 0)),
            pl.BlockSpec((tm, LANES), lambda i, j: (i, 0)),
        ),
        scratch_shapes=[pltpu.VMEM((tm, D_MODEL), MXU_DTYPE)],
        compiler_params=pltpu.CompilerParams(
            dimension_semantics=("arbitrary", "arbitrary"),
            vmem_limit_bytes=VMEM_LIMIT),
        name="proj",
    )(x2, g_pre, w_main, w_small, col_scale)


def _split3(x):
    hi = x.astype(MXU_DTYPE)
    r1 = x - hi.astype(jnp.float32)
    mid = r1.astype(MXU_DTYPE)
    lo = (r1 - mid.astype(jnp.float32)).astype(MXU_DTYPE)
    return hi, mid, lo


def _gla_kernel(q_ref, k_ref, v_ref, z_ref, sm_ref, wup_ref, ba_ref, gg_ref,
                o_ref, st_ref, qd_ref, kd_ref, kt_ref, b_ref, oacc_ref, *, chunks_per_step):
    c = GLA_CHUNK
    rows = chunks_per_step * c
    heads = range(GLA_HEADS)

    @pl.when(pl.program_id(1) == 0)
    def _():
        st_ref[...] = jnp.zeros_like(st_ref)

    def ks(h):
        return slice(h * GLA_DK, (h + 1) * GLA_DK)

    def vs(h):
        return slice(h * GLA_DV, (h + 1) * GLA_DV)

    def chunk(i):
        return slice(i * c, (i + 1) * c)

    row = lax.broadcasted_iota(jnp.int32, (rows, rows), 0)
    col = lax.broadcasted_iota(jnp.int32, (rows, rows), 1)
    causal = (row // c == col // c) & (col <= row)
    tril = jnp.where(causal, 1.0, 0.0).astype(MXU_DTYPE)

    ga = sm_ref[:, SMALL_GA:SMALL_GA + GLA_RANK].astype(MXU_DTYPE)
    pre = _dot(ga, wup_ref[...]) + ba_ref[...]
    log_a = (jnp.minimum(pre, 0.0) - jnp.log1p(jnp.exp(-jnp.abs(pre)))) / GLA_GATE_NORM
    hi, mid, lo = _split3(log_a)
    b_ref[...] = _dot(tril, hi) + _dot(tril, mid) + _dot(tril, lo)

    decay = []
    for i in range(chunks_per_step):
        b_last = b_ref[i * c + c - 1:i * c + c, :]
        decay.append(jnp.exp(b_last))
        for h in heads:
            b = b_ref[chunk(i), ks(h)]
            q = q_ref[chunk(i), ks(h)].astype(jnp.float32)
            k = k_ref[chunk(i), ks(h)].astype(jnp.float32)
            qd_ref[chunk(i), ks(h)] = (q * jnp.exp(b)).astype(MXU_DTYPE)
            kd_ref[chunk(i), ks(h)] = (k * jnp.exp(-b)).astype(MXU_DTYPE)
            kt_ref[chunk(i), ks(h)] = (k * jnp.exp(b_last[:, ks(h)] - b)).astype(MXU_DTYPE)

    att = [_nt_dot(qd_ref[:, ks(h)], kd_ref[:, ks(h)]) for h in heads]
    att = [jnp.where(causal, a, 0.0).astype(MXU_DTYPE) for a in att]
    for h in heads:
        oacc_ref[:, vs(h)] = _dot(att[h], v_ref[:, vs(h)])

    st = [st_ref[h] for h in heads]
    for i in range(chunks_per_step):
        for h in heads:
            kv = _tn_dot(v_ref[chunk(i), vs(h)], kt_ref[chunk(i), ks(h)])
            oacc_ref[chunk(i), vs(h)] += _nt_dot(qd_ref[chunk(i), ks(h)], st[h].astype(MXU_DTYPE))
            st[h] = st[h] * decay[i][:, ks(h)] + kv
    for h in heads:
        st_ref[h] = st[h]
        oh = oacc_ref[:, vs(h)]
        y = oh * lax.rsqrt(jnp.mean(oh * oh, axis=-1, keepdims=True) + EPS)
        y = y * gg_ref[...]
        zz = z_ref[:, vs(h)].astype(jnp.float32)
        o_ref[:, vs(h)] = (y * _silu(zz)).astype(o_ref.dtype)


def _gla(u, small, w_up, b_a, g_gla, batch, seq, chunks_per_step):
    rows = chunks_per_step * GLA_CHUNK
    steps = seq // rows
    hk = GLA_HEADS * GLA_DK
    hv = GLA_HEADS * GLA_DV

    def rmap(cb):
        return lambda b, s: (b * steps + s, cb)

    return pl.pallas_call(
        functools.partial(_gla_kernel, chunks_per_step=chunks_per_step),
        out_shape=jax.ShapeDtypeStruct((batch * seq, hv), ACT_DTYPE),
        grid=(batch, steps),
        in_specs=[
            pl.BlockSpec((rows, hk), rmap(COL_GQ // hk)),
            pl.BlockSpec((rows, hk), rmap(COL_GK // hk)),
            pl.BlockSpec((rows, hv), rmap(COL_GV // hv)),
            pl.BlockSpec((rows, hv), rmap(COL_GZ // hv)),
            pl.BlockSpec((rows, LANES), rmap(0)),
            pl.BlockSpec((GLA_RANK, hk), lambda b, s: (0, 0)),
            pl.BlockSpec((1, hk), lambda b, s: (0, 0)),
            pl.BlockSpec((1, GLA_DV), lambda b, s: (0, 0)),
        ],
        out_specs=pl.BlockSpec((rows, hv), rmap(0)),
        scratch_shapes=[pltpu.VMEM((GLA_HEADS, GLA_DV, GLA_DK), jnp.float32),
                        pltpu.VMEM((rows, hk), MXU_DTYPE),
                        pltpu.VMEM((rows, hk), MXU_DTYPE),
                        pltpu.VMEM((rows, hk), MXU_DTYPE),
                        pltpu.VMEM((rows, hk), jnp.float32),
                        pltpu.VMEM((rows, hv), jnp.float32)],
        compiler_params=pltpu.CompilerParams(
            dimension_semantics=("arbitrary", "arbitrary"),
            vmem_limit_bytes=VMEM_LIMIT),
        name="gla",
    )(u, u, u, u, small, w_up, b_a, g_gla)


KEY_CHUNK = 4 * Q_BLOCK
BLOCKS_PER_CHUNK = KEY_CHUNK // Q_BLOCK
PLANE_GROUP = 32 * SUBLANES


def _t5_bucket(dist):
    max_exact = REL_BUCKETS // 2
    d = jnp.maximum(dist, 1).astype(jnp.float32)
    large = max_exact + jnp.floor(jnp.log(d / max_exact) / math.log(REL_MAX_DIST / max_exact)
                                  * (REL_BUCKETS - max_exact)).astype(jnp.int32)
    large = jnp.minimum(large, REL_BUCKETS - 1)
    return jnp.where(dist < max_exact, dist, large)


def _dsa_kernel(rb_ref, dq_ref, dk_ref, dv_ref, iq_ref, ikd_ref, sm_ref, dz_ref,
                out_ref,
                kt_ref, sc_ref, planes_ref, mb_ref, vt_ref, bt_ref, iqm_ref, qaug_ref, acc_ref,
                lga_ref, lgb_ref, mxa_ref, mxb_ref, ml_ref,
                *, seq):
    qb = pl.program_id(1)
    blk = Q_BLOCK
    ch = KEY_CHUNK
    k_sel = min(TOPK_MAX, seq // 4)
    n_chunks = qb // BLOCKS_PER_CHUNK + 1
    width = DSA_GROUPS * blk

    row_i = lax.broadcasted_iota(jnp.int32, (blk, blk), 0)
    col_i = lax.broadcasted_iota(jnp.int32, (blk, blk), 1)
    crow_i = lax.broadcasted_iota(jnp.int32, (ch, blk), 0)
    q_pos = qb * blk + lax.broadcasted_iota(jnp.int32, (ch, blk), 1)

    @pl.when((pl.program_id(0) == 0) & (qb == 0))
    def _():
        planes_ref[...] = jnp.zeros_like(planes_ref)
        kt_ref[...] = jnp.zeros_like(kt_ref)
        for delta in range(3):
            dist = jnp.maximum(delta * blk + col_i - row_i, 0)
            bucket = _t5_bucket(dist)
            for h in range(DSA_HEADS):
                tile = jnp.zeros((blk, blk), jnp.float32)
                for bk in range(REL_BUCKETS):
                    tile = jnp.where(bucket == bk, rb_ref[bk, h] * LOG2_E, tile)
                cc, g = divmod(h, DSA_GROUPS)
                bt_ref[delta, cc, :, g * blk:(g + 1) * blk] = tile

    @pl.when(qb == 0)
    def _():
        def body(i, carry):
            r = pl.multiple_of(i * ch, ch)
            vt_ref[i] = dv_ref[pl.ds(r, ch), :].astype(jnp.float32).T.astype(vt_ref.dtype)
            return carry
        lax.fori_loop(0, seq // ch, body, 0)

    w_t = sm_ref[...].T[SMALL_IW:SMALL_IW + IDX_HEADS, :]
    w_t = w_t * (IDX_HEADS ** -0.5) * (IDX_DIM ** -0.5)

    lane = lax.broadcasted_iota(jnp.int32, (blk, LANES), 1)
    for h in range(IDX_HEADS):
        pair = iq_ref[:, (h // 2) * LANES:(h // 2 + 1) * LANES]
        keep = (lane < IDX_DIM) if h % 2 == 0 else (lane >= IDX_DIM)
        iqm_ref[h // 2, (h % 2) * blk:(h % 2 + 1) * blk, :] = jnp.where(keep, pair, jnp.zeros_like(pair))

    def score_chunk(c, causal):
        r = pl.multiple_of(c * ch, ch)
        ikc = ikd_ref[pl.ds(r, ch), :]
        acc = None
        for hp in range(IDX_HEADS // 2):
            s2 = _nt_dot(ikc, iqm_ref[hp])
            for h in (2 * hp, 2 * hp + 1):
                t = w_t[h:h + 1, :] * jnp.maximum(s2[:, (h % 2) * blk:(h % 2 + 1) * blk], 0.0)
                acc = t if acc is None else acc + t
        if causal:
            acc = jnp.where(r + crow_i <= q_pos, acc, -jnp.inf)
        sc_ref[pl.ds(r, ch), :] = acc
        bits = lax.bitcast_convert_type(acc, jnp.int32)
        kt_ref[pl.ds(r, ch), :] = bits ^ ((bits >> 31) & 0x7FFFFFFF)

    def build_planes(c):
        for g in range(ch // PLANE_GROUP):
            base = c * ch + g * PLANE_GROUP
            a = [kt_ref[pl.ds(pl.multiple_of(base + SUBLANES * v, SUBLANES), SUBLANES), :]
                 for v in range(32)]
            j, m = 16, 0x0000FFFF
            while j:
                k0 = 0
                while k0 < 32:
                    t = (a[k0] ^ lax.shift_right_logical(a[k0 + j], jnp.int32(j))) & m
                    a[k0] = a[k0] ^ t
                    a[k0 + j] = a[k0 + j] ^ (t << j)
                    k0 = (k0 + j + 1) & ~j
                j >>= 1
                m = (m ^ (m << j)) & 0xFFFFFFFF if j else m
                m = m - (1 << 32) if m >= (1 << 31) else m
            a[0] = ~a[0]
            row = pl.multiple_of(c * (ch // 32) + g * SUBLANES, SUBLANES)
            for jj in range(32):
                planes_ref[jj, pl.ds(row, SUBLANES), :] = a[jj]

    @pl.when(n_chunks > 1)
    def _():
        score_chunk(0, False)

    def score_body(c, carry):
        build_planes(c - 1)
        score_chunk(c, False)
        return carry
    lax.fori_loop(1, n_chunks - 1, score_body, 0)
    build_planes(jnp.maximum(n_chunks - 2, 0))
    score_chunk(n_chunks - 1, True)
    build_planes(n_chunks - 1)

    n_rows = seq // 32

    def rowsum(x):
        part = jnp.sum(x.reshape(n_rows // SUBLANES, SUBLANES, blk), axis=0)
        return jnp.sum(part, axis=0, keepdims=True)

    prow = lax.broadcasted_iota(jnp.int32, (n_rows, blk), 0)
    alive0 = jnp.where(prow < n_chunks * (ch // 32), -1, 0).astype(jnp.int32)

    def bit_body(j, carry):
        alive, cnt_gt, ukey = carry
        w = planes_ref[j]
        ones = alive & w
        c1 = rowsum(lax.population_count(ones))
        take = cnt_gt + c1 >= k_sel
        alive = jnp.where(take, ones, alive & ~w)
        cnt_gt = jnp.where(take, cnt_gt, cnt_gt + c1)
        ukey = jnp.where(take, ukey | (jnp.int32(1) << (31 - j)), ukey)
        return alive, cnt_gt, ukey

    zero_row = jnp.zeros((1, blk), jnp.int32)
    alive, cnt_gt, ukey = lax.fori_loop(0, 32, bit_body, (alive0, zero_row, zero_row))
    ans = ukey ^ INT_MIN
    thr_bits = jnp.where(ans < 0, ans ^ 0x7FFFFFFF, ans)
    thr0 = lax.bitcast_convert_type(thr_bits, jnp.float32)

    def tile_sum(x):
        return jnp.sum(x.reshape(ch // SUBLANES, SUBLANES, blk), axis=0)

    def fold_rows(acc, combine):
        return functools.reduce(combine, [acc[i:i + 1] for i in range(SUBLANES)])

    def mask_pass(thr, p_max):
        def body(c, carry):
            a_gt, a_ge = carry
            r = pl.multiple_of(c * ch, ch)
            sc = sc_ref[pl.ds(r, ch), :]
            k_pos = r + crow_i
            gt = sc > thr
            ge = sc >= thr
            sel = (gt | (ge & (k_pos <= p_max))) & (k_pos <= q_pos)
            mb_ref[pl.ds(r, ch), :] = jnp.where(sel, 0.0, MASK_NEG).astype(mb_ref.dtype)
            return (a_gt + tile_sum(jnp.where(gt, 1, 0).astype(jnp.int32)),
                    a_ge + tile_sum(jnp.where(ge, 1, 0).astype(jnp.int32)))
        zero = jnp.zeros((SUBLANES, blk), jnp.int32)
        a_gt, a_ge = lax.fori_loop(0, n_chunks, body, (zero, zero))
        return fold_rows(a_gt, jnp.add), fold_rows(a_ge, jnp.add)

    def nearest(pred_fn, fill, combine):
        def body(c, acc):
            r = pl.multiple_of(c * ch, ch)
            sc = sc_ref[pl.ds(r, ch), :]
            part = jnp.where(pred_fn(sc), sc, fill).reshape(ch // SUBLANES, SUBLANES, blk)
            return combine(acc, functools.reduce(combine, [part[i] for i in range(ch // SUBLANES)]))
        acc = lax.fori_loop(0, n_chunks, body, jnp.full((SUBLANES, blk), fill, jnp.float32))
        return fold_rows(acc, combine)

    def status(c_gt, c_ge):
        off = (c_gt >= k_sel) | (c_ge < k_sel)
        return jnp.max(jnp.where(off, 2, 0) | jnp.where(c_ge > k_sel, 1, 0))

    def walk(state):
        thr, c_gt, c_ge, _, trips = state
        above = nearest(lambda sc: sc > thr, jnp.inf, jnp.minimum)
        below = nearest(lambda sc: sc < thr, -jnp.inf, jnp.maximum)
        thr = jnp.where(c_gt >= k_sel, above, jnp.where(c_ge < k_sel, below, thr))
        c_gt, c_ge = mask_pass(thr, no_bound)
        return thr, c_gt, c_ge, status(c_gt, c_ge), trips + 1

    no_bound = jnp.full((1, blk), 2 * seq, jnp.int32)
    cnt_gt0, cnt_ge0 = mask_pass(thr0, no_bound)
    thr, cnt_gt, _, flag, _ = lax.while_loop(
        lambda state: (state[3] >= 2) & (state[4] < seq), walk,
        (thr0, cnt_gt0, cnt_ge0, status(cnt_gt0, cnt_ge0), jnp.int32(0)))

    @pl.when(flag == 1)
    def _():
        need = k_sel - cnt_gt
        n_bits = (2 * seq - 1).bit_length()

        def tie_count(cand):
            def body(c, acc):
                r = pl.multiple_of(c * ch, ch)
                hit = (sc_ref[pl.ds(r, ch), :] == thr) & (r + crow_i < cand)
                return acc + tile_sum(jnp.where(hit, 1, 0).astype(jnp.int32))
            return fold_rows(lax.fori_loop(0, n_chunks, body, jnp.zeros((SUBLANES, blk), jnp.int32)), jnp.add)

        def pos_body(i, p):
            cand = p | (jnp.int32(1) << (n_bits - 1 - i))
            return jnp.where(tie_count(cand) <= need - 1, cand, p)
        p_max = lax.fori_loop(0, n_bits, pos_body, jnp.zeros((1, blk), jnp.int32))
        mask_pass(thr, p_max)

    eye = jnp.where(row_i == col_i, 1.0, 0.0).astype(MXU_DTYPE)
    for cc in range(DSA_KV_HEADS):
        for g in range(DSA_GROUPS):
            h = cc * DSA_GROUPS + g
            qaug_ref[cc, g * blk:(g + 1) * blk, 0:DSA_HD] = dq_ref[:, h * DSA_HD:(h + 1) * DSA_HD]
            qaug_ref[cc, g * blk:(g + 1) * blk, DSA_HD:2 * DSA_HD] = eye
    acc_ref[...] = jnp.zeros_like(acc_ref)
    for cc in range(DSA_KV_HEADS):
        ml_ref[cc, 0] = jnp.full(ml_ref.shape[2:], -jnp.inf, jnp.float32)
        ml_ref[cc, 1] = jnp.zeros(ml_ref.shape[2:], jnp.float32)

    def stage_logits(c, lg_ref, mx_ref, far):
        r = pl.multiple_of(c * ch, ch)
        mbc = mb_ref[pl.ds(r, ch), :]
        for cc in range(DSA_KV_HEADS):
            kaug = jnp.concatenate([dk_ref[pl.ds(r, ch), cc * DSA_HD:(cc + 1) * DSA_HD], mbc], axis=1)
            lg = _nt_dot(kaug, qaug_ref[cc])
            if far:
                off = bt_ref[2, cc, 0:1, :]
            else:
                lg = lg + jnp.concatenate(
                    [bt_ref[jnp.clip(qb - (c * BLOCKS_PER_CHUNK + j), 0, 2), cc]
                     for j in range(BLOCKS_PER_CHUNK)], axis=0)
                off = jnp.zeros((1, width), jnp.float32)
            lg_ref[cc] = lg
            mx_ref[cc, 0] = jnp.broadcast_to(jnp.max(lg, axis=0, keepdims=True) + off, mx_ref.shape[2:])
            mx_ref[cc, 1] = jnp.broadcast_to(off, mx_ref.shape[2:])

    def stage_softmax(c, lg_ref, mx_ref):
        for cc in range(DSA_KV_HEADS):
            m = ml_ref[cc, 0, 0:1, :]
            l = ml_ref[cc, 1, 0:1, :]
            m_new = jnp.maximum(m, mx_ref[cc, 0, 0:1, :])
            alpha = jnp.exp2(m - m_new)
            p = jnp.exp2(lg_ref[cc] - (m_new - mx_ref[cc, 1, 0:1, :]))
            l_new = alpha * l + jnp.sum(p, axis=0, keepdims=True)
            pv = _dot(vt_ref[c, cc * DSA_HD:(cc + 1) * DSA_HD, :], p.astype(MXU_DTYPE))
            acc_ref[cc] = acc_ref[cc] * alpha + pv
            ml_ref[cc, 0] = jnp.broadcast_to(m_new, ml_ref.shape[2:])
            ml_ref[cc, 1] = jnp.broadcast_to(l_new, ml_ref.shape[2:])

    c_near = jnp.maximum(qb - 1, 0) // BLOCKS_PER_CHUNK

    @pl.when(c_near > 0)
    def _():
        stage_logits(0, lga_ref, mxa_ref, far=True)

    @pl.when(c_near == 0)
    def _():
        stage_logits(0, lga_ref, mxa_ref, far=False)

    def pair_body(pi, carry, far):
        c0 = 2 * pi
        stage_logits(c0 + 1, lgb_ref, mxb_ref, far)
        stage_softmax(c0, lga_ref, mxa_ref)

        stage_logits(jnp.minimum(c0 + 2, n_chunks - 1), lga_ref, mxa_ref, far)
        stage_softmax(c0 + 1, lgb_ref, mxb_ref)
        return carry

    far_pairs = jnp.maximum(c_near - 1, 0) // 2
    lax.fori_loop(0, far_pairs, functools.partial(pair_body, far=True), 0)
    lax.fori_loop(far_pairs, n_chunks // 2, functools.partial(pair_body, far=False), 0)

    @pl.when(n_chunks % 2 == 1)
    def _():
        stage_softmax(n_chunks - 1, lga_ref, mxa_ref)

    for cc in range(DSA_KV_HEADS):
        o_t = acc_ref[cc] / ml_ref[cc, 1, 0:1, :]
        for g in range(DSA_GROUPS):
            h = cc * DSA_GROUPS + g
            o = o_t[:, g * blk:(g + 1) * blk].T
            zz = dz_ref[:, h * DSA_HD:(h + 1) * DSA_HD].astype(jnp.float32)
            out_ref[:, h * DSA_HD:(h + 1) * DSA_HD] = (o * _silu(zz)).astype(out_ref.dtype)


def _dsa(u, ikd, small, rel_bias, batch, seq):
    nb = seq // Q_BLOCK
    hq = DSA_HEADS * DSA_HD
    hkv = DSA_KV_HEADS * DSA_HD
    hi = IDX_HEADS * IDX_DIM

    def qmap(cb):
        return lambda b, q: (b * nb + q, cb)

    def bmap(cb):
        return lambda b, q: (b, cb)

    return pl.pallas_call(
        functools.partial(_dsa_kernel, seq=seq),
        out_shape=jax.ShapeDtypeStruct((batch * seq, hq), ACT_DTYPE),
        grid=(batch, nb),
        in_specs=[
            pl.BlockSpec(memory_space=pltpu.SMEM),
            pl.BlockSpec((Q_BLOCK, hq), qmap(COL_DQ // hq)),
            pl.BlockSpec((seq, hkv), bmap(COL_DK // hkv)),
            pl.BlockSpec((seq, hkv), bmap(COL_DV // hkv)),
            pl.BlockSpec((Q_BLOCK, hi), qmap(COL_IQ // hi)),
            pl.BlockSpec((seq, LANES), bmap(0)),
            pl.BlockSpec((Q_BLOCK, LANES), qmap(0)),
            pl.BlockSpec((Q_BLOCK, hq), qmap(COL_DZ // hq)),
        ],
        out_specs=pl.BlockSpec((Q_BLOCK, hq), qmap(0)),
        scratch_shapes=[
            pltpu.VMEM((seq, Q_BLOCK), jnp.int32),
            pltpu.VMEM((seq, Q_BLOCK), jnp.float32),
            pltpu.VMEM((32, seq // 32, Q_BLOCK), jnp.int32),
            pltpu.VMEM((seq, Q_BLOCK), MXU_DTYPE),
            pltpu.VMEM((seq // KEY_CHUNK, hkv, KEY_CHUNK), MXU_DTYPE),
            pltpu.VMEM((3, DSA_KV_HEADS, Q_BLOCK, DSA_GROUPS * Q_BLOCK), jnp.float32),
            pltpu.VMEM((IDX_HEADS // 2, 2 * Q_BLOCK, LANES), MXU_DTYPE),
            pltpu.VMEM((DSA_KV_HEADS, DSA_GROUPS * Q_BLOCK, 2 * DSA_HD), MXU_DTYPE),
            pltpu.VMEM((DSA_KV_HEADS, DSA_HD, DSA_GROUPS * Q_BLOCK), jnp.float32),
            pltpu.VMEM((DSA_KV_HEADS, KEY_CHUNK, DSA_GROUPS * Q_BLOCK), jnp.float32),
            pltpu.VMEM((DSA_KV_HEADS, KEY_CHUNK, DSA_GROUPS * Q_BLOCK), jnp.float32),
            pltpu.VMEM((DSA_KV_HEADS, 2, SUBLANES, DSA_GROUPS * Q_BLOCK), jnp.float32),
            pltpu.VMEM((DSA_KV_HEADS, 2, SUBLANES, DSA_GROUPS * Q_BLOCK), jnp.float32),
            pltpu.VMEM((DSA_KV_HEADS, 2, SUBLANES, DSA_GROUPS * Q_BLOCK), jnp.float32),
        ],
        compiler_params=pltpu.CompilerParams(
            dimension_semantics=("arbitrary", "arbitrary"),
            vmem_limit_bytes=VMEM_LIMIT),
        name="dsa",
    )(rel_bias, u, u, u, u, ikd, small, u)


def _memkv_kernel(mem_ref, g_ref, w_ref, o_ref):
    xf = mem_ref[...]
    y = xf * lax.rsqrt(jnp.mean(xf * xf, axis=-1, keepdims=True) + EPS)
    hb = (y * g_ref[...]).astype(MXU_DTYPE)
    o_ref[...] = _dot(hb, w_ref[...]).astype(o_ref.dtype)


def _memkv(mem2, g_mem, w_kv, batch):
    n = 2 * X_HEADS * X_HD
    return pl.pallas_call(
        _memkv_kernel,
        out_shape=jax.ShapeDtypeStruct((batch * N_MEM, n), ACT_DTYPE),
        grid=(batch,),
        in_specs=[
            pl.BlockSpec((N_MEM, D_MODEL), lambda b: (b, 0)),
            pl.BlockSpec((1, D_MODEL), lambda b: (0, 0)),
            pl.BlockSpec((D_MODEL, n), lambda b: (0, 0)),
        ],
        out_specs=pl.BlockSpec((N_MEM, n), lambda b: (b, 0)),
        compiler_params=pltpu.CompilerParams(
            dimension_semantics=("arbitrary",),
            vmem_limit_bytes=VMEM_LIMIT),
        name="memkv",
    )(mem2, g_mem, w_kv)


def _merge_kernel(x_ref, yg_ref, yd_ref, xq_ref, xz_ref, sg_ref, sd_ref, sm_ref, mkv_ref,
                  wg_ref, wd_ref, wx_ref, wo_ref, gp_ref, o_ref, ym_ref):
    hw = X_HEADS * X_HD
    heads = range(X_HEADS)

    def cs(h):
        return slice(h * X_HD, (h + 1) * X_HD)

    lgs = [_nt_dot(xq_ref[:, cs(h)], mkv_ref[:, cs(h)]) for h in heads]
    t_g = _dot(yg_ref[...], wg_ref[...])
    ps = []
    for lg in lgs:
        e = jnp.exp(lg - jnp.max(lg, axis=-1, keepdims=True))
        ps.append((e / jnp.sum(e, axis=-1, keepdims=True)).astype(MXU_DTYPE))
    outs = [_dot(ps[h], mkv_ref[:, hw + h * X_HD:hw + (h + 1) * X_HD]) for h in heads]
    t_d = _dot(yd_ref[...], wd_ref[...])
    for h in heads:
        ym_ref[:, cs(h)] = (outs[h] * _silu(xz_ref[:, cs(h)].astype(jnp.float32))).astype(ym_ref.dtype)
    merged = jax.nn.sigmoid(sg_ref[...].astype(jnp.float32)) * t_g
    merged = merged + jax.nn.sigmoid(sd_ref[...].astype(jnp.float32)) * t_d
    merged = merged + jax.nn.sigmoid(sm_ref[...].astype(jnp.float32)) * _dot(ym_ref[...], wx_ref[...])
    t = _dot(merged.astype(MXU_DTYPE), wo_ref[...])
    y = t * lax.rsqrt(jnp.mean(t * t, axis=-1, keepdims=True) + EPS)
    o_ref[...] = x_ref[...] + y * gp_ref[...]


def _merge(x2, y_gla, y_dsa, u, mkv, w_g, w_d, w_x, w_o, g_post, seq, tm):
    m = x2.shape[0]
    steps_per_batch = seq // tm
    d = D_MODEL

    def rmap(cb):
        return lambda i: (i, cb)

    wspec = pl.BlockSpec((d, d), lambda i: (0, 0))
    return pl.pallas_call(
        _merge_kernel,
        out_shape=jax.ShapeDtypeStruct((m, d), jnp.float32),
        grid=(m // tm,),
        in_specs=[
            pl.BlockSpec((tm, d), rmap(0)),
            pl.BlockSpec((tm, d), rmap(0)),
            pl.BlockSpec((tm, d), rmap(0)),
            pl.BlockSpec((tm, d), rmap(COL_XQ // d)),
            pl.BlockSpec((tm, d), rmap(COL_XZ // d)),
            pl.BlockSpec((tm, d), rmap(COL_SG // d)),
            pl.BlockSpec((tm, d), rmap(COL_SD // d)),
            pl.BlockSpec((tm, d), rmap(COL_SM // d)),
            pl.BlockSpec((N_MEM, 2 * X_HEADS * X_HD), lambda i: (i // steps_per_batch, 0)),
            wspec, wspec, wspec, wspec,
            pl.BlockSpec((1, d), lambda i: (0, 0)),
        ],
        out_specs=pl.BlockSpec((tm, d), rmap(0)),
        scratch_shapes=[pltpu.VMEM((tm, d), MXU_DTYPE)],
        compiler_params=pltpu.CompilerParams(
            dimension_semantics=("arbitrary",),
            vmem_limit_bytes=VMEM_LIMIT),
        name="merge",
    )(x2, y_gla, y_dsa, u, u, u, u, u, mkv, w_g, w_d, w_x, w_o, g_post)


_MAIN_GROUPS = (0, 1, 2, 4, 5, 11, 12, 13, 14, 6, 7, 8)
_GROUP_GA, _GROUP_IK, _GROUP_IW = 3, 9, 10
RELAYOUT_COLS = 128


def _relayout_kernel(wt_ref, main_ref, small_ref):
    offs = np.concatenate([[0], np.cumsum(np.array(SPLIT_SIZES))]).tolist()

    def group(g):
        return wt_ref[offs[g]:offs[g + 1], :].astype(MXU_DTYPE)

    dst = 0
    for g in _MAIN_GROUPS:
        main_ref[dst:dst + SPLIT_SIZES[g], :] = group(g)
        dst += SPLIT_SIZES[g]
    pad = jnp.zeros((LANES - GLA_RANK - IDX_HEADS, wt_ref.shape[1]), MXU_DTYPE)
    small_ref[...] = jnp.concatenate(
        [group(_GROUP_IK), group(_GROUP_IK), group(_GROUP_GA), group(_GROUP_IW), pad], axis=0)


def _relayout_w_in(w_in):
    wt = w_in.T
    n, d = wt.shape
    return pl.pallas_call(
        _relayout_kernel,
        out_shape=(jax.ShapeDtypeStruct((U_COLS, d), MXU_DTYPE),
                   jax.ShapeDtypeStruct((2 * LANES, d), MXU_DTYPE)),
        grid=(d // RELAYOUT_COLS,),
        in_specs=[pl.BlockSpec((n, RELAYOUT_COLS), lambda i: (0, i))],
        out_specs=(pl.BlockSpec((U_COLS, RELAYOUT_COLS), lambda i: (0, i)),
                   pl.BlockSpec((2 * LANES, RELAYOUT_COLS), lambda i: (0, i))),
        compiler_params=pltpu.CompilerParams(
            dimension_semantics=("arbitrary",),
            vmem_limit_bytes=VMEM_LIMIT),
        name="relayout",
    )(wt)


def _col_scale():
    s = np.ones((1, U_COLS), np.float32)
    s[:, COL_GQ:COL_GQ + GLA_HEADS * GLA_DK] = GLA_DK ** -0.5
    s[:, COL_DQ:COL_DQ + DSA_HEADS * DSA_HD] = DSA_HD ** -0.5 * LOG2_E
    s[:, COL_XQ:COL_XQ + X_HEADS * X_HD] = X_HD ** -0.5
    return jnp.asarray(s)


def _layer(x2, mem2, g_pre, g_post, g_mem, w_in, w_up, b_a, g_gla, rel_bias, w_kv,
           w_g, w_d, w_x, w_o, batch, seq):
    w_main, w_small = _relayout_w_in(w_in)
    tm = min(PROJ_TM, batch * seq)
    u, ikd, small = _proj(x2, g_pre.reshape(1, -1), w_main, w_small, _col_scale(), tm, PROJ_TN)
    y_gla = _gla(u, small, w_up.astype(MXU_DTYPE), b_a.reshape(1, -1), g_gla.reshape(1, -1),
                 batch, seq, chunks_per_step=4)
    y_dsa = _dsa(u, ikd, small, rel_bias, batch, seq)
    mkv = _memkv(mem2, g_mem.reshape(1, -1), w_kv.astype(MXU_DTYPE), batch)
    return _merge(x2, y_gla, y_dsa, u, mkv, w_g.astype(MXU_DTYPE), w_d.astype(MXU_DTYPE),
                  w_x.astype(MXU_DTYPE), w_o.astype(MXU_DTYPE), g_post.reshape(1, -1), seq, MERGE_TM)


def kernel(x, mem, g_pre, g_post, g_mem, w_in, w_gla_a_up, b_gla_a, g_gla, rel_bias,
           w_mem_kv, w_gla_out, w_dsa_out, w_x_out, w_o):
    batch, seq, d = x.shape
    x2 = x.reshape(batch * seq, d)
    mem2 = mem.reshape(batch * N_MEM, d)
    for i in range(g_pre.shape[0]):
        x2 = _layer(x2, mem2, g_pre[i], g_post[i], g_mem[i], w_in[i], w_gla_a_up[i],
                    b_gla_a[i], g_gla[i], rel_bias, w_mem_kv[i], w_gla_out[i],
                    w_dsa_out[i], w_x_out[i], w_o[i], batch, seq)
    return x2.reshape(batch, seq, d)
```

```python
import functools
import math

import jax
import jax.numpy as jnp
import numpy as np
from jax import lax
from jax.experimental import pallas as pl
from jax.experimental.pallas import tpu as pltpu

D_MODEL = 1024
N_MEM = 256
EPS = 1e-6
GLA_HEADS = 4
GLA_DK = 128
GLA_DV = 256
GLA_RANK = 16
GLA_GATE_NORM = 16.0
GLA_CHUNK = 64
DSA_HEADS = 8
DSA_KV_HEADS = 2
DSA_GROUPS = DSA_HEADS // DSA_KV_HEADS
DSA_HD = 128
IDX_HEADS = 8
IDX_DIM = 64
TOPK_MAX = 256
Q_BLOCK = 128
REL_BUCKETS = 32
REL_MAX_DIST = 128
X_HEADS = 4
X_HD = 256

SPLIT_SIZES = (512, 512, 1024, 16, 1024, 1024, 256, 256, 512, 64, 8, 1024, 1024, 1024, 3072)

LANES = 128
SUBLANES = 8

MXU_DTYPE = jnp.bfloat16
ACT_DTYPE = jnp.bfloat16

U_COLS = 11264
COL_GQ, COL_GK, COL_GV, COL_GZ = 0, 512, 1024, 2048
COL_DQ, COL_DZ, COL_XQ, COL_XZ = 3072, 4096, 5120, 6144
COL_SG, COL_SD, COL_SM = 7168, 8192, 9216
COL_DK, COL_DV, COL_IQ = 10240, 10496, 10752
SMALL_GA, SMALL_IW = 0, 16

MASK_NEG = -1e30
LOG2_E = math.log2(math.e)
INT_MIN = -(2 ** 31)
VMEM_LIMIT = 56 * 1024 * 1024

PROJ_TM = 1024
PROJ_TN = U_COLS // 4
MERGE_TM = 512


def _nt_dot(a, b):
    return lax.dot_general(a, b, (((1,), (1,)), ((), ())),
                           preferred_element_type=jnp.float32)


def _tn_dot(a, b):
    return lax.dot_general(a, b, (((0,), (0,)), ((), ())),
                           preferred_element_type=jnp.float32)


def _dot(a, b):
    return jnp.dot(a, b, preferred_element_type=jnp.float32)


def _silu(z):
    return z * jax.nn.sigmoid(z)


def _proj_kernel(x_ref, g_ref, w_ref, ws_ref, cs_ref, u_ref, ikd_ref, sm_ref, h_ref):
    @pl.when(pl.program_id(1) == 0)
    def _():
        xf = x_ref[...]
        y = xf * lax.rsqrt(jnp.mean(xf * xf, axis=-1, keepdims=True) + EPS)
        hb = (y * g_ref[...]).astype(MXU_DTYPE)
        h_ref[...] = hb
        r = _nt_dot(hb, ws_ref[...])
        ikd_ref[...] = r[:, :LANES].astype(ikd_ref.dtype)
        sm_ref[...] = r[:, LANES:]

    acc = _nt_dot(h_ref[...], w_ref[...])
    u_ref[...] = (acc * cs_ref[...]).astype(u_ref.dtype)


def _proj(x2, g_pre, w_main, w_small, col_scale, tm, tn):
    m = x2.shape[0]
    grid = (m // tm, U_COLS // tn)
    return pl.pallas_call(
        _proj_kernel,
        out_shape=(jax.ShapeDtypeStruct((m, U_COLS), ACT_DTYPE),
                   jax.ShapeDtypeStruct((m, LANES), ACT_DTYPE),
                   jax.ShapeDtypeStruct((m, LANES), jnp.float32)),
        grid=grid,
        in_specs=[
            pl.BlockSpec((tm, D_MODEL), lambda i, j: (i, 0)),
            pl.BlockSpec((1, D_MODEL), lambda i, j: (0, 0)),
            pl.BlockSpec((tn, D_MODEL), lambda i, j: (j, 0)),
            pl.BlockSpec((2 * LANES, D_MODEL), lambda i, j: (0, 0)),
            pl.BlockSpec((1, tn), lambda i, j: (0, j)),
        ],
        out_specs=(
            pl.BlockSpec((tm, tn), lambda i, j: (i, j)),
            pl.BlockSpec((tm, LANES), lambda i, j: (i, 0)),
            pl.BlockSpec((tm, LANES), lambda i, j: (i, 0)),
        ),
        scratch_shapes=[pltpu.VMEM((tm, D_MODEL), MXU_DTYPE)],
        compiler_params=pltpu.CompilerParams(
            dimension_semantics=("arbitrary", "arbitrary"),
            vmem_limit_bytes=VMEM_LIMIT),
        name="proj",
    )(x2, g_pre, w_main, w_small, col_scale)


def _split3(x):
    hi = x.astype(MXU_DTYPE)
    r1 = x - hi.astype(jnp.float32)
    mid = r1.astype(MXU_DTYPE)
    lo = (r1 - mid.astype(jnp.float32)).astype(MXU_DTYPE)
    return hi, mid, lo


def _gla_kernel(q_ref, k_ref, v_ref, z_ref, sm_ref, wup_ref, ba_ref, gg_ref,
                o_ref, st_ref, qd_ref, kd_ref, kt_ref, b_ref, oacc_ref, *, chunks_per_step):
    c = GLA_CHUNK
    rows = chunks_per_step * c
    heads = range(GLA_HEADS)

    @pl.when(pl.program_id(1) == 0)
    def _():
        st_ref[...] = jnp.zeros_like(st_ref)

    def ks(h):
        return slice(h * GLA_DK, (h + 1) * GLA_DK)

    def vs(h):
        return slice(h * GLA_DV, (h + 1) * GLA_DV)

    def chunk(i):
        return slice(i * c, (i + 1) * c)

    row = lax.broadcasted_iota(jnp.int32, (rows, rows), 0)
    col = lax.broadcasted_iota(jnp.int32, (rows, rows), 1)
    causal = (row // c == col // c) & (col <= row)
    tril = jnp.where(causal, 1.0, 0.0).astype(MXU_DTYPE)

    ga = sm_ref[:, SMALL_GA:SMALL_GA + GLA_RANK].astype(MXU_DTYPE)
    pre = _dot(ga, wup_ref[...]) + ba_ref[...]
    log_a = (jnp.minimum(pre, 0.0) - jnp.log1p(jnp.exp(-jnp.abs(pre)))) / GLA_GATE_NORM
    hi, mid, lo = _split3(log_a)
    b_ref[...] = _dot(tril, hi) + _dot(tril, mid) + _dot(tril, lo)

    decay = []
    for i in range(chunks_per_step):
        b_last = b_ref[i * c + c - 1:i * c + c, :]
        decay.append(jnp.exp(b_last))
        for h in heads:
            b = b_ref[chunk(i), ks(h)]
            q = q_ref[chunk(i), ks(h)].astype(jnp.float32)
            k = k_ref[chunk(i), ks(h)].astype(jnp.float32)
            qd_ref[chunk(i), ks(h)] = (q * jnp.exp(b)).astype(MXU_DTYPE)
            kd_ref[chunk(i), ks(h)] = (k * jnp.exp(-b)).astype(MXU_DTYPE)
            kt_ref[chunk(i), ks(h)] = (k * jnp.exp(b_last[:, ks(h)] - b)).astype(MXU_DTYPE)

    att = [_nt_dot(qd_ref[:, ks(h)], kd_ref[:, ks(h)]) for h in heads]
    att = [jnp.where(causal, a, 0.0).astype(MXU_DTYPE) for a in att]
    for h in heads:
        oacc_ref[:, vs(h)] = _dot(att[h], v_ref[:, vs(h)])

    st = [st_ref[h] for h in heads]
    for i in range(chunks_per_step):
        for h in heads:
            kv = _tn_dot(v_ref[chunk(i), vs(h)], kt_ref[chunk(i), ks(h)])
            oacc_ref[chunk(i), vs(h)] += _nt_dot(qd_ref[chunk(i), ks(h)], st[h].astype(MXU_DTYPE))
            st[h] = st[h] * decay[i][:, ks(h)] + kv
    for h in heads:
        st_ref[h] = st[h]
        oh = oacc_ref[:, vs(h)]
        y = oh * lax.rsqrt(jnp.mean(oh * oh, axis=-1, keepdims=True) + EPS)
        y = y * gg_ref[...]
        zz = z_ref[:, vs(h)].astype(jnp.float32)
        o_ref[:, vs(h)] = (y * _silu(zz)).astype(o_ref.dtype)


def _gla(u, small, w_up, b_a, g_gla, batch, seq, chunks_per_step):
    rows = chunks_per_step * GLA_CHUNK
    steps = seq // rows
    hk = GLA_HEADS * GLA_DK
    hv = GLA_HEADS * GLA_DV

    def rmap(cb):
        return lambda b, s: (b * steps + s, cb)

    return pl.pallas_call(
        functools.partial(_gla_kernel, chunks_per_step=chunks_per_step),
        out_shape=jax.ShapeDtypeStruct((batch * seq, hv), ACT_DTYPE),
        grid=(batch, steps),
        in_specs=[
            pl.BlockSpec((rows, hk), rmap(COL_GQ // hk)),
            pl.BlockSpec((rows, hk), rmap(COL_GK // hk)),
            pl.BlockSpec((rows, hv), rmap(COL_GV // hv)),
            pl.BlockSpec((rows, hv), rmap(COL_GZ // hv)),
            pl.BlockSpec((rows, LANES), rmap(0)),
            pl.BlockSpec((GLA_RANK, hk), lambda b, s: (0, 0)),
            pl.BlockSpec((1, hk), lambda b, s: (0, 0)),
            pl.BlockSpec((1, GLA_DV), lambda b, s: (0, 0)),
        ],
        out_specs=pl.BlockSpec((rows, hv), rmap(0)),
        scratch_shapes=[pltpu.VMEM((GLA_HEADS, GLA_DV, GLA_DK), jnp.float32),
                        pltpu.VMEM((rows, hk), MXU_DTYPE),
                        pltpu.VMEM((rows, hk), MXU_DTYPE),
                        pltpu.VMEM((rows, hk), MXU_DTYPE),
                        pltpu.VMEM((rows, hk), jnp.float32),
                        pltpu.VMEM((rows, hv), jnp.float32)],
        compiler_params=pltpu.CompilerParams(
            dimension_semantics=("arbitrary", "arbitrary"),
            vmem_limit_bytes=VMEM_LIMIT),
        name="gla",
    )(u, u, u, u, small, w_up, b_a, g_gla)


KEY_CHUNK = 4 * Q_BLOCK
BLOCKS_PER_CHUNK = KEY_CHUNK // Q_BLOCK
PLANE_GROUP = 32 * SUBLANES


def _t5_bucket(dist):
    max_exact = REL_BUCKETS // 2
    d = jnp.maximum(dist, 1).astype(jnp.float32)
    large = max_exact + jnp.floor(jnp.log(d / max_exact) / math.log(REL_MAX_DIST / max_exact)
                                  * (REL_BUCKETS - max_exact)).astype(jnp.int32)
    large = jnp.minimum(large, REL_BUCKETS - 1)
    return jnp.where(dist < max_exact, dist, large)


def _dsa_kernel(rb_ref, dq_ref, dk_ref, dv_ref, iq_ref, ikd_ref, sm_ref, dz_ref,
                out_ref,
                sc_ref, planes_ref, mb_ref, vt_ref, bt_ref, iqm_ref, qaug_ref, acc_ref,
                lga_ref, lgb_ref, mxa_ref, mxb_ref, ml_ref,
                *, seq):
    qb = pl.program_id(1)
    blk = Q_BLOCK
    ch = KEY_CHUNK
    k_sel = min(TOPK_MAX, seq // 4)
    n_chunks = qb // BLOCKS_PER_CHUNK + 1
    width = DSA_GROUPS * blk

    row_i = lax.broadcasted_iota(jnp.int32, (blk, blk), 0)
    col_i = lax.broadcasted_iota(jnp.int32, (blk, blk), 1)
    crow_i = lax.broadcasted_iota(jnp.int32, (ch, blk), 0)
    q_pos = qb * blk + lax.broadcasted_iota(jnp.int32, (ch, blk), 1)

    @pl.when((pl.program_id(0) == 0) & (qb == 0))
    def _():
        planes_ref[...] = jnp.zeros_like(planes_ref)
        sc_ref[...] = jnp.zeros_like(sc_ref)
        for delta in range(3):
            dist = jnp.maximum(delta * blk + col_i - row_i, 0)
            bucket = _t5_bucket(dist)
            for h in range(DSA_HEADS):
                tile = jnp.zeros((blk, blk), jnp.float32)
                for bk in range(REL_BUCKETS):
                    tile = jnp.where(bucket == bk, rb_ref[bk, h] * LOG2_E, tile)
                cc, g = divmod(h, DSA_GROUPS)
                bt_ref[delta, cc, :, g * blk:(g + 1) * blk] = tile

    @pl.when(qb == 0)
    def _():
        def body(i, carry):
            r = pl.multiple_of(i * ch, ch)
            vt_ref[i] = dv_ref[pl.ds(r, ch), :].astype(jnp.float32).T.astype(vt_ref.dtype)
            return carry
        lax.fori_loop(0, seq // ch, body, 0)

    w_t = sm_ref[...].T[SMALL_IW:SMALL_IW + IDX_HEADS, :]
    w_t = w_t * (IDX_HEADS ** -0.5) * (IDX_DIM ** -0.5)

    lane = lax.broadcasted_iota(jnp.int32, (blk, LANES), 1)
    for h in range(IDX_HEADS):
        pair = iq_ref[:, (h // 2) * LANES:(h // 2 + 1) * LANES]
        keep = (lane < IDX_DIM) if h % 2 == 0 else (lane >= IDX_DIM)
        iqm_ref[h // 2, (h % 2) * blk:(h % 2 + 1) * blk, :] = jnp.where(keep, pair, jnp.zeros_like(pair))

    def score_chunk(c, causal):
        r = pl.multiple_of(c * ch, ch)
        ikc = ikd_ref[pl.ds(r, ch), :]
        acc = None
        for hp in range(IDX_HEADS // 2):
            s2 = _nt_dot(ikc, iqm_ref[hp])
            for h in (2 * hp, 2 * hp + 1):
                t = w_t[h:h + 1, :] * jnp.maximum(s2[:, (h % 2) * blk:(h % 2 + 1) * blk], 0.0)
                acc = t if acc is None else acc + t
        if causal:
            acc = jnp.where(r + crow_i <= q_pos, acc, -jnp.inf)
        sc_ref[pl.ds(r, ch), :] = acc

    def build_planes(c):
        for g in range(ch // PLANE_GROUP):
            base = c * ch + g * PLANE_GROUP
            a = [lax.bitcast_convert_type(
                     sc_ref[pl.ds(pl.multiple_of(base + SUBLANES * v, SUBLANES), SUBLANES), :], jnp.int32)
                 for v in range(32)]
            j, m = 16, 0x0000FFFF
            while j:
                k0 = 0
                while k0 < 32:
                    t = (a[k0] ^ lax.shift_right_logical(a[k0 + j], jnp.int32(j))) & m
                    a[k0] = a[k0] ^ t
                    a[k0 + j] = a[k0 + j] ^ (t << j)
                    k0 = (k0 + j + 1) & ~j
                j >>= 1
                m = (m ^ (m << j)) & 0xFFFFFFFF if j else m
                m = m - (1 << 32) if m >= (1 << 31) else m
            sign = a[0]
            row = pl.multiple_of(c * (ch // 32) + g * SUBLANES, SUBLANES)
            planes_ref[0, pl.ds(row, SUBLANES), :] = ~sign
            for jj in range(1, 32):
                planes_ref[jj, pl.ds(row, SUBLANES), :] = a[jj] ^ sign

    @pl.when(n_chunks > 1)
    def _():
        score_chunk(0, False)

    def score_body(c, carry):
        build_planes(c - 1)
        score_chunk(c, False)
        return carry
    lax.fori_loop(1, n_chunks - 1, score_body, 0)
    build_planes(jnp.maximum(n_chunks - 2, 0))
    score_chunk(n_chunks - 1, True)
    build_planes(n_chunks - 1)

    n_rows = seq // 32

    def rowsum(x):
        part = jnp.sum(x.reshape(n_rows // SUBLANES, SUBLANES, blk), axis=0)
        return jnp.sum(part, axis=0, keepdims=True)

    prow = lax.broadcasted_iota(jnp.int32, (n_rows, blk), 0)
    alive0 = jnp.where(prow < n_chunks * (ch // 32), -1, 0).astype(jnp.int32)

    def bit_body(j, carry):
        alive, cnt_gt, ukey = carry
        w = planes_ref[j]
        ones = alive & w
        c1 = rowsum(lax.population_count(ones))
        take = cnt_gt + c1 >= k_sel
        alive = jnp.where(take, ones, alive & ~w)
        cnt_gt = jnp.where(take, cnt_gt, cnt_gt + c1)
        ukey = jnp.where(take, ukey | (jnp.int32(1) << (31 - j)), ukey)
        return alive, cnt_gt, ukey

    zero_row = jnp.zeros((1, blk), jnp.int32)
    alive, cnt_gt, ukey = lax.fori_loop(0, 32, bit_body, (alive0, zero_row, zero_row))
    ans = ukey ^ INT_MIN
    thr_bits = jnp.where(ans < 0, ans ^ 0x7FFFFFFF, ans)
    thr0 = lax.bitcast_convert_type(thr_bits, jnp.float32)

    def tile_sum(x):
        return jnp.sum(x.reshape(ch // SUBLANES, SUBLANES, blk), axis=0)

    def fold_rows(acc, combine):
        return functools.reduce(combine, [acc[i:i + 1] for i in range(SUBLANES)])

    def mask_pass(thr, p_max):
        def body(c, carry):
            a_gt, a_ge = carry
            r = pl.multiple_of(c * ch, ch)
            sc = sc_ref[pl.ds(r, ch), :]
            k_pos = r + crow_i
            gt = sc > thr
            ge = sc >= thr
            sel = (gt | (ge & (k_pos <= p_max))) & (k_pos <= q_pos)
            mb_ref[pl.ds(r, ch), :] = jnp.where(sel, 0.0, MASK_NEG).astype(mb_ref.dtype)
            return (a_gt + tile_sum(jnp.where(gt, 1, 0).astype(jnp.int32)),
                    a_ge + tile_sum(jnp.where(ge, 1, 0).astype(jnp.int32)))
        zero = jnp.zeros((SUBLANES, blk), jnp.int32)
        a_gt, a_ge = lax.fori_loop(0, n_chunks, body, (zero, zero))
        return fold_rows(a_gt, jnp.add), fold_rows(a_ge, jnp.add)

    def nearest(pred_fn, fill, combine):
        def body(c, acc):
            r = pl.multiple_of(c * ch, ch)
            sc = sc_ref[pl.ds(r, ch), :]
            part = jnp.where(pred_fn(sc), sc, fill).reshape(ch // SUBLANES, SUBLANES, blk)
            return combine(acc, functools.reduce(combine, [part[i] for i in range(ch // SUBLANES)]))
        acc = lax.fori_loop(0, n_chunks, body, jnp.full((SUBLANES, blk), fill, jnp.float32))
        return fold_rows(acc, combine)

    def status(c_gt, c_ge):
        off = (c_gt >= k_sel) | (c_ge < k_sel)
        return jnp.max(jnp.where(off, 2, 0) | jnp.where(c_ge > k_sel, 1, 0))

    def walk(state):
        thr, c_gt, c_ge, _, trips = state
        above = nearest(lambda sc: sc > thr, jnp.inf, jnp.minimum)
        below = nearest(lambda sc: sc < thr, -jnp.inf, jnp.maximum)
        thr = jnp.where(c_gt >= k_sel, above, jnp.where(c_ge < k_sel, below, thr))
        c_gt, c_ge = mask_pass(thr, no_bound)
        return thr, c_gt, c_ge, status(c_gt, c_ge), trips + 1

    no_bound = jnp.full((1, blk), 2 * seq, jnp.int32)
    cnt_gt0, cnt_ge0 = mask_pass(thr0, no_bound)
    thr, cnt_gt, _, flag, _ = lax.while_loop(
        lambda state: (state[3] >= 2) & (state[4] < seq), walk,
        (thr0, cnt_gt0, cnt_ge0, status(cnt_gt0, cnt_ge0), jnp.int32(0)))

    @pl.when(flag == 1)
    def _():
        need = k_sel - cnt_gt
        n_bits = (2 * seq - 1).bit_length()

        def tie_count(cand):
            def body(c, acc):
                r = pl.multiple_of(c * ch, ch)
                hit = (sc_ref[pl.ds(r, ch), :] == thr) & (r + crow_i < cand)
                return acc + tile_sum(jnp.where(hit, 1, 0).astype(jnp.int32))
            return fold_rows(lax.fori_loop(0, n_chunks, body, jnp.zeros((SUBLANES, blk), jnp.int32)), jnp.add)

        def pos_body(i, p):
            cand = p | (jnp.int32(1) << (n_bits - 1 - i))
            return jnp.where(tie_count(cand) <= need - 1, cand, p)
        p_max = lax.fori_loop(0, n_bits, pos_body, jnp.zeros((1, blk), jnp.int32))
        mask_pass(thr, p_max)

    eye = jnp.where(row_i == col_i, 1.0, 0.0).astype(MXU_DTYPE)
    for cc in range(DSA_KV_HEADS):
        for g in range(DSA_GROUPS):
            h = cc * DSA_GROUPS + g
            qaug_ref[cc, g * blk:(g + 1) * blk, 0:DSA_HD] = dq_ref[:, h * DSA_HD:(h + 1) * DSA_HD]
            qaug_ref[cc, g * blk:(g + 1) * blk, DSA_HD:2 * DSA_HD] = eye
    acc_ref[...] = jnp.zeros_like(acc_ref)
    for cc in range(DSA_KV_HEADS):
        ml_ref[cc, 0] = jnp.full(ml_ref.shape[2:], -jnp.inf, jnp.float32)
        ml_ref[cc, 1] = jnp.zeros(ml_ref.shape[2:], jnp.float32)

    def stage_logits(c, lg_ref, mx_ref, far):
        r = pl.multiple_of(c * ch, ch)
        mbc = mb_ref[pl.ds(r, ch), :]
        for cc in range(DSA_KV_HEADS):
            kaug = jnp.concatenate([dk_ref[pl.ds(r, ch), cc * DSA_HD:(cc + 1) * DSA_HD], mbc], axis=1)
            lg = _nt_dot(kaug, qaug_ref[cc])
            if far:
                off = bt_ref[2, cc, 0:1, :]
            else:
                lg = lg + jnp.concatenate(
                    [bt_ref[jnp.clip(qb - (c * BLOCKS_PER_CHUNK + j), 0, 2), cc]
                     for j in range(BLOCKS_PER_CHUNK)], axis=0)
                off = jnp.zeros((1, width), jnp.float32)
            lg_ref[cc] = lg
            mx_ref[cc, 0] = jnp.broadcast_to(jnp.max(lg, axis=0, keepdims=True) + off, mx_ref.shape[2:])
            mx_ref[cc, 1] = jnp.broadcast_to(off, mx_ref.shape[2:])

    def stage_softmax(c, lg_ref, mx_ref):
        for cc in range(DSA_KV_HEADS):
            m = ml_ref[cc, 0, 0:1, :]
            l = ml_ref[cc, 1, 0:1, :]
            m_new = jnp.maximum(m, mx_ref[cc, 0, 0:1, :])
            alpha = jnp.exp2(m - m_new)
            p = jnp.exp2(lg_ref[cc] - (m_new - mx_ref[cc, 1, 0:1, :]))
            l_new = alpha * l + jnp.sum(p, axis=0, keepdims=True)
            pv = _dot(vt_ref[c, cc * DSA_HD:(cc + 1) * DSA_HD, :], p.astype(MXU_DTYPE))
            acc_ref[cc] = acc_ref[cc] * alpha + pv
            ml_ref[cc, 0] = jnp.broadcast_to(m_new, ml_ref.shape[2:])
            ml_ref[cc, 1] = jnp.broadcast_to(l_new, ml_ref.shape[2:])

    c_near = jnp.maximum(qb - 1, 0) // BLOCKS_PER_CHUNK

    @pl.when(c_near > 0)
    def _():
        stage_logits(0, lga_ref, mxa_ref, far=True)

    @pl.when(c_near == 0)
    def _():
        stage_logits(0, lga_ref, mxa_ref, far=False)

    def pair_body(pi, carry, far):
        c0 = 2 * pi
        stage_logits(c0 + 1, lgb_ref, mxb_ref, far)
        stage_softmax(c0, lga_ref, mxa_ref)

        stage_logits(jnp.minimum(c0 + 2, n_chunks - 1), lga_ref, mxa_ref, far)
        stage_softmax(c0 + 1, lgb_ref, mxb_ref)
        return carry

    far_pairs = jnp.maximum(c_near - 1, 0) // 2
    lax.fori_loop(0, far_pairs, functools.partial(pair_body, far=True), 0)
    lax.fori_loop(far_pairs, n_chunks // 2, functools.partial(pair_body, far=False), 0)

    @pl.when(n_chunks % 2 == 1)
    def _():
        stage_softmax(n_chunks - 1, lga_ref, mxa_ref)

    for cc in range(DSA_KV_HEADS):
        o_t = acc_ref[cc] / ml_ref[cc, 1, 0:1, :]
        for g in range(DSA_GROUPS):
            h = cc * DSA_GROUPS + g
            o = o_t[:, g * blk:(g + 1) * blk].T
            zz = dz_ref[:, h * DSA_HD:(h + 1) * DSA_HD].astype(jnp.float32)
            out_ref[:, h * DSA_HD:(h + 1) * DSA_HD] = (o * _silu(zz)).astype(out_ref.dtype)


def _dsa(u, ikd, small, rel_bias, batch, seq):
    nb = seq // Q_BLOCK
    hq = DSA_HEADS * DSA_HD
    hkv = DSA_KV_HEADS * DSA_HD
    hi = IDX_HEADS * IDX_DIM

    def qmap(cb):
        return lambda b, q: (b * nb + q, cb)

    def bmap(cb):
        return lambda b, q: (b, cb)

    return pl.pallas_call(
        functools.partial(_dsa_kernel, seq=seq),
        out_shape=jax.ShapeDtypeStruct((batch * seq, hq), ACT_DTYPE),
        grid=(batch, nb),
        in_specs=[
            pl.BlockSpec(memory_space=pltpu.SMEM),
            pl.BlockSpec((Q_BLOCK, hq), qmap(COL_DQ // hq)),
            pl.BlockSpec((seq, hkv), bmap(COL_DK // hkv)),
            pl.BlockSpec((seq, hkv), bmap(COL_DV // hkv)),
            pl.BlockSpec((Q_BLOCK, hi), qmap(COL_IQ // hi)),
            pl.BlockSpec((seq, LANES), bmap(0)),
            pl.BlockSpec((Q_BLOCK, LANES), qmap(0)),
            pl.BlockSpec((Q_BLOCK, hq), qmap(COL_DZ // hq)),
        ],
        out_specs=pl.BlockSpec((Q_BLOCK, hq), qmap(0)),
        scratch_shapes=[
            pltpu.VMEM((seq, Q_BLOCK), jnp.float32),
            pltpu.VMEM((32, seq // 32, Q_BLOCK), jnp.int32),
            pltpu.VMEM((seq, Q_BLOCK), MXU_DTYPE),
            pltpu.VMEM((seq // KEY_CHUNK, hkv, KEY_CHUNK), MXU_DTYPE),
            pltpu.VMEM((3, DSA_KV_HEADS, Q_BLOCK, DSA_GROUPS * Q_BLOCK), jnp.float32),
            pltpu.VMEM((IDX_HEADS // 2, 2 * Q_BLOCK, LANES), MXU_DTYPE),
            pltpu.VMEM((DSA_KV_HEADS, DSA_GROUPS * Q_BLOCK, 2 * DSA_HD), MXU_DTYPE),
            pltpu.VMEM((DSA_KV_HEADS, DSA_HD, DSA_GROUPS * Q_BLOCK), jnp.float32),
            pltpu.VMEM((DSA_KV_HEADS, KEY_CHUNK, DSA_GROUPS * Q_BLOCK), jnp.float32),
            pltpu.VMEM((DSA_KV_HEADS, KEY_CHUNK, DSA_GROUPS * Q_BLOCK), jnp.float32),
            pltpu.VMEM((DSA_KV_HEADS, 2, SUBLANES, DSA_GROUPS * Q_BLOCK), jnp.float32),
            pltpu.VMEM((DSA_KV_HEADS, 2, SUBLANES, DSA_GROUPS * Q_BLOCK), jnp.float32),
            pltpu.VMEM((DSA_KV_HEADS, 2, SUBLANES, DSA_GROUPS * Q_BLOCK), jnp.float32),
        ],
        compiler_params=pltpu.CompilerParams(
            dimension_semantics=("arbitrary", "arbitrary"),
            vmem_limit_bytes=VMEM_LIMIT),
        name="dsa",
    )(rel_bias, u, u, u, u, ikd, small, u)


def _memkv_kernel(mem_ref, g_ref, w_ref, o_ref):
    xf = mem_ref[...]
    y = xf * lax.rsqrt(jnp.mean(xf * xf, axis=-1, keepdims=True) + EPS)
    hb = (y * g_ref[...]).astype(MXU_DTYPE)
    o_ref[...] = _dot(hb, w_ref[...]).astype(o_ref.dtype)


def _memkv(mem2, g_mem, w_kv, batch):
    n = 2 * X_HEADS * X_HD
    return pl.pallas_call(
        _memkv_kernel,
        out_shape=jax.ShapeDtypeStruct((batch * N_MEM, n), ACT_DTYPE),
        grid=(batch,),
        in_specs=[
            pl.BlockSpec((N_MEM, D_MODEL), lambda b: (b, 0)),
            pl.BlockSpec((1, D_MODEL), lambda b: (0, 0)),
            pl.BlockSpec((D_MODEL, n), lambda b: (0, 0)),
        ],
        out_specs=pl.BlockSpec((N_MEM, n), lambda b: (b, 0)),
        compiler_params=pltpu.CompilerParams(
            dimension_semantics=("arbitrary",),
            vmem_limit_bytes=VMEM_LIMIT),
        name="memkv",
    )(mem2, g_mem, w_kv)


def _merge_kernel(x_ref, yg_ref, yd_ref, xq_ref, xz_ref, sg_ref, sd_ref, sm_ref, mkv_ref,
                  wg_ref, wd_ref, wx_ref, wo_ref, gp_ref, o_ref, ym_ref):
    hw = X_HEADS * X_HD
    heads = range(X_HEADS)

    def cs(h):
        return slice(h * X_HD, (h + 1) * X_HD)

    lgs = [_nt_dot(xq_ref[:, cs(h)], mkv_ref[:, cs(h)]) for h in heads]
    t_g = _dot(yg_ref[...], wg_ref[...])
    ps = []
    for lg in lgs:
        e = jnp.exp(lg - jnp.max(lg, axis=-1, keepdims=True))
        ps.append((e / jnp.sum(e, axis=-1, keepdims=True)).astype(MXU_DTYPE))
    outs = [_dot(ps[h], mkv_ref[:, hw + h * X_HD:hw + (h + 1) * X_HD]) for h in heads]
    t_d = _dot(yd_ref[...], wd_ref[...])
    for h in heads:
        ym_ref[:, cs(h)] = (outs[h] * _silu(xz_ref[:, cs(h)].astype(jnp.float32))).astype(ym_ref.dtype)
    merged = jax.nn.sigmoid(sg_ref[...].astype(jnp.float32)) * t_g
    merged = merged + jax.nn.sigmoid(sd_ref[...].astype(jnp.float32)) * t_d
    merged = merged + jax.nn.sigmoid(sm_ref[...].astype(jnp.float32)) * _dot(ym_ref[...], wx_ref[...])
    t = _dot(merged.astype(MXU_DTYPE), wo_ref[...])
    y = t * lax.rsqrt(jnp.mean(t * t, axis=-1, keepdims=True) + EPS)
    o_ref[...] = x_ref[...] + y * gp_ref[...]


def _merge(x2, y_gla, y_dsa, u, mkv, w_g, w_d, w_x, w_o, g_post, seq, tm):
    m = x2.shape[0]
    steps_per_batch = seq // tm
    d = D_MODEL

    def rmap(cb):
        return lambda i: (i, cb)

    wspec = pl.BlockSpec((d, d), lambda i: (0, 0))
    return pl.pallas_call(
        _merge_kernel,
        out_shape=jax.ShapeDtypeStruct((m, d), jnp.float32),
        grid=(m // tm,),
        in_specs=[
            pl.BlockSpec((tm, d), rmap(0)),
            pl.BlockSpec((tm, d), rmap(0)),
            pl.BlockSpec((tm, d), rmap(0)),
            pl.BlockSpec((tm, d), rmap(COL_XQ // d)),
            pl.BlockSpec((tm, d), rmap(COL_XZ // d)),
            pl.BlockSpec((tm, d), rmap(COL_SG // d)),
            pl.BlockSpec((tm, d), rmap(COL_SD // d)),
            pl.BlockSpec((tm, d), rmap(COL_SM // d)),
            pl.BlockSpec((N_MEM, 2 * X_HEADS * X_HD), lambda i: (i // steps_per_batch, 0)),
            wspec, wspec, wspec, wspec,
            pl.BlockSpec((1, d), lambda i: (0, 0)),
        ],
        out_specs=pl.BlockSpec((tm, d), rmap(0)),
        scratch_shapes=[pltpu.VMEM((tm, d), MXU_DTYPE)],
        compiler_params=pltpu.CompilerParams(
            dimension_semantics=("arbitrary",),
            vmem_limit_bytes=VMEM_LIMIT),
        name="merge",
    )(x2, y_gla, y_dsa, u, u, u, u, u, mkv, w_g, w_d, w_x, w_o, g_post)


_MAIN_GROUPS = (0, 1, 2, 4, 5, 11, 12, 13, 14, 6, 7, 8)
_GROUP_GA, _GROUP_IK, _GROUP_IW = 3, 9, 10
RELAYOUT_COLS = 128


def _relayout_kernel(wt_ref, main_ref, small_ref):
    offs = np.concatenate([[0], np.cumsum(np.array(SPLIT_SIZES))]).tolist()

    def group(g):
        return wt_ref[offs[g]:offs[g + 1], :].astype(MXU_DTYPE)

    dst = 0
    for g in _MAIN_GROUPS:
        main_ref[dst:dst + SPLIT_SIZES[g], :] = group(g)
        dst += SPLIT_SIZES[g]
    pad = jnp.zeros((LANES - GLA_RANK - IDX_HEADS, wt_ref.shape[1]), MXU_DTYPE)
    small_ref[...] = jnp.concatenate(
        [group(_GROUP_IK), group(_GROUP_IK), group(_GROUP_GA), group(_GROUP_IW), pad], axis=0)


def _relayout_w_in(w_in):
    wt = w_in.T
    n, d = wt.shape
    return pl.pallas_call(
        _relayout_kernel,
        out_shape=(jax.ShapeDtypeStruct((U_COLS, d), MXU_DTYPE),
                   jax.ShapeDtypeStruct((2 * LANES, d), MXU_DTYPE)),
        grid=(d // RELAYOUT_COLS,),
        in_specs=[pl.BlockSpec((n, RELAYOUT_COLS), lambda i: (0, i))],
        out_specs=(pl.BlockSpec((U_COLS, RELAYOUT_COLS), lambda i: (0, i)),
                   pl.BlockSpec((2 * LANES, RELAYOUT_COLS), lambda i: (0, i))),
        compiler_params=pltpu.CompilerParams(
            dimension_semantics=("arbitrary",),
            vmem_limit_bytes=VMEM_LIMIT),
        name="relayout",
    )(wt)


def _col_scale():
    s = np.ones((1, U_COLS), np.float32)
    s[:, COL_GQ:COL_GQ + GLA_HEADS * GLA_DK] = GLA_DK ** -0.5
    s[:, COL_DQ:COL_DQ + DSA_HEADS * DSA_HD] = DSA_HD ** -0.5 * LOG2_E
    s[:, COL_XQ:COL_XQ + X_HEADS * X_HD] = X_HD ** -0.5
    return jnp.asarray(s)


def _layer(x2, mem2, g_pre, g_post, g_mem, w_in, w_up, b_a, g_gla, rel_bias, w_kv,
           w_g, w_d, w_x, w_o, batch, seq):
    w_main, w_small = _relayout_w_in(w_in)
    tm = min(PROJ_TM, batch * seq)
    u, ikd, small = _proj(x2, g_pre.reshape(1, -1), w_main, w_small, _col_scale(), tm, PROJ_TN)
    y_gla = _gla(u, small, w_up.astype(MXU_DTYPE), b_a.reshape(1, -1), g_gla.reshape(1, -1),
                 batch, seq, chunks_per_step=4)
    y_dsa = _dsa(u, ikd, small, rel_bias, batch, seq)
    mkv = _memkv(mem2, g_mem.reshape(1, -1), w_kv.astype(MXU_DTYPE), batch)
    return _merge(x2, y_gla, y_dsa, u, mkv, w_g.astype(MXU_DTYPE), w_d.astype(MXU_DTYPE),
                  w_x.astype(MXU_DTYPE), w_o.astype(MXU_DTYPE), g_post.reshape(1, -1), seq, MERGE_TM)


def kernel(x, mem, g_pre, g_post, g_mem, w_in, w_gla_a_up, b_gla_a, g_gla, rel_bias,
           w_mem_kv, w_gla_out, w_dsa_out, w_x_out, w_o):
    batch, seq, d = x.shape
    x2 = x.reshape(batch * seq, d)
    mem2 = mem.reshape(batch * N_MEM, d)
    for i in range(g_pre.shape[0]):
        x2 = _layer(x2, mem2, g_pre[i], g_post[i], g_mem[i], w_in[i], w_gla_a_up[i],
                    b_gla_a[i], g_gla[i], rel_bias, w_mem_kv[i], w_gla_out[i],
                    w_dsa_out[i], w_x_out[i], w_o[i], batch, seq)
    return x2.reshape(batch, seq, d)
```

```python
import functools
import math

import jax
import jax.numpy as jnp
import numpy as np
from jax import lax
from jax.experimental import pallas as pl
from jax.experimental.pallas import tpu as pltpu

D_MODEL = 1024
N_MEM = 256
EPS = 1e-6
GLA_HEADS = 4
GLA_DK = 128
GLA_DV = 256
GLA_RANK = 16
GLA_GATE_NORM = 16.0
GLA_CHUNK = 64
DSA_HEADS = 8
DSA_KV_HEADS = 2
DSA_GROUPS = DSA_HEADS // DSA_KV_HEADS
DSA_HD = 128
IDX_HEADS = 8
IDX_DIM = 64
TOPK_MAX = 256
Q_BLOCK = 128
REL_BUCKETS = 32
REL_MAX_DIST = 128
X_HEADS = 4
X_HD = 256

SPLIT_SIZES = (512, 512, 1024, 16, 1024, 1024, 256, 256, 512, 64, 8, 1024, 1024, 1024, 3072)

LANES = 128
SUBLANES = 8

MXU_DTYPE = jnp.bfloat16
ACT_DTYPE = jnp.bfloat16

U_COLS = 11264
COL_GQ, COL_GK, COL_GV, COL_GZ = 0, 512, 1024, 2048
COL_DQ, COL_DZ, COL_XQ, COL_XZ = 3072, 4096, 5120, 6144
COL_SG, COL_SD, COL_SM = 7168, 8192, 9216
COL_DK, COL_DV, COL_IQ = 10240, 10496, 10752
SMALL_GA, SMALL_IW = 0, 16

MASK_NEG = -1e30
LOG2_E = math.log2(math.e)
INT_MIN = -(2 ** 31)
VMEM_LIMIT = 56 * 1024 * 1024

PROJ_TM = 1024
PROJ_TN = U_COLS // 4
MERGE_TM = 512


def _nt_dot(a, b):
    return lax.dot_general(a, b, (((1,), (1,)), ((), ())),
                           preferred_element_type=jnp.float32)


def _tn_dot(a, b):
    return lax.dot_general(a, b, (((0,), (0,)), ((), ())),
                           preferred_element_type=jnp.float32)


def _dot(a, b):
    return jnp.dot(a, b, preferred_element_type=jnp.float32)


def _silu(z):
    return z * jax.nn.sigmoid(z)


def _proj_kernel(x_ref, g_ref, w_ref, ws_ref, cs_ref, u_ref, ikd_ref, sm_ref, h_ref):
    @pl.when(pl.program_id(1) == 0)
    def _():
        xf = x_ref[...]
        y = xf * lax.rsqrt(jnp.mean(xf * xf, axis=-1, keepdims=True) + EPS)
        hb = (y * g_ref[...]).astype(MXU_DTYPE)
        h_ref[...] = hb
        r = _nt_dot(hb, ws_ref[...])
        ikd_ref[...] = r[:, :LANES].astype(ikd_ref.dtype)
        sm_ref[...] = r[:, LANES:]

    acc = _nt_dot(h_ref[...], w_ref[...])
    u_ref[...] = (acc * cs_ref[...]).astype(u_ref.dtype)


def _proj(x2, g_pre, w_main, w_small, col_scale, tm, tn):
    m = x2.shape[0]
    grid = (m // tm, U_COLS // tn)
    return pl.pallas_call(
        _proj_kernel,
        out_shape=(jax.ShapeDtypeStruct((m, U_COLS), ACT_DTYPE),
                   jax.ShapeDtypeStruct((m, LANES), ACT_DTYPE),
                   jax.ShapeDtypeStruct((m, LANES), jnp.float32)),
        grid=grid,
        in_specs=[
            pl.BlockSpec((tm, D_MODEL), lambda i, j: (i, 0)),
            pl.BlockSpec((1, D_MODEL), lambda i, j: (0, 0)),
            pl.BlockSpec((tn, D_MODEL), lambda i, j: (j, 0)),
            pl.BlockSpec((2 * LANES, D_MODEL), lambda i, j: (0, 0)),
            pl.BlockSpec((1, tn), lambda i, j: (0, j)),
        ],
        out_specs=(
            pl.BlockSpec((tm, tn), lambda i, j: (i, j)),
            pl.BlockSpec((tm, LANES), lambda i, j: (i, 0)),
            pl.BlockSpec((tm, LANES), lambda i, j: (i, 0)),
        ),
        scratch_shapes=[pltpu.VMEM((tm, D_MODEL), MXU_DTYPE)],
        compiler_params=pltpu.CompilerParams(
            dimension_semantics=("arbitrary", "arbitrary"),
            vmem_limit_bytes=VMEM_LIMIT),
        name="proj",
    )(x2, g_pre, w_main, w_small, col_scale)


def _split3(x):
    hi = x.astype(MXU_DTYPE)
    r1 = x - hi.astype(jnp.float32)
    mid = r1.astype(MXU_DTYPE)
    lo = (r1 - mid.astype(jnp.float32)).astype(MXU_DTYPE)
    return hi, mid, lo


def _gla_kernel(q_ref, k_ref, v_ref, z_ref, sm_ref, wup_ref, ba_ref, gg_ref,
                o_ref, st_ref, qd_ref, kd_ref, kt_ref, b_ref, oacc_ref, *, chunks_per_step):
    c = GLA_CHUNK
    rows = chunks_per_step * c
    heads = range(GLA_HEADS)

    @pl.when(pl.program_id(1) == 0)
    def _():
        st_ref[...] = jnp.zeros_like(st_ref)

    def ks(h):
        return slice(h * GLA_DK, (h + 1) * GLA_DK)

    def vs(h):
        return slice(h * GLA_DV, (h + 1) * GLA_DV)

    def chunk(i):
        return slice(i * c, (i + 1) * c)

    row = lax.broadcasted_iota(jnp.int32, (rows, rows), 0)
    col = lax.broadcasted_iota(jnp.int32, (rows, rows), 1)
    causal = (row // c == col // c) & (col <= row)
    tril = jnp.where(causal, 1.0, 0.0).astype(MXU_DTYPE)

    ga = sm_ref[:, SMALL_GA:SMALL_GA + GLA_RANK].astype(MXU_DTYPE)
    pre = _dot(ga, wup_ref[...]) + ba_ref[...]
    log_a = (jnp.minimum(pre, 0.0) - jnp.log1p(jnp.exp(-jnp.abs(pre)))) / GLA_GATE_NORM
    hi, mid, lo = _split3(log_a)
    b_ref[...] = _dot(tril, hi) + _dot(tril, mid) + _dot(tril, lo)

    decay = []
    for i in range(chunks_per_step):
        b_last = b_ref[i * c + c - 1:i * c + c, :]
        decay.append(jnp.exp(b_last))
        for h in heads:
            b = b_ref[chunk(i), ks(h)]
            q = q_ref[chunk(i), ks(h)].astype(jnp.float32)
            k = k_ref[chunk(i), ks(h)].astype(jnp.float32)
            qd_ref[chunk(i), ks(h)] = (q * jnp.exp(b)).astype(MXU_DTYPE)
            kd_ref[chunk(i), ks(h)] = (k * jnp.exp(-b)).astype(MXU_DTYPE)
            kt_ref[chunk(i), ks(h)] = (k * jnp.exp(b_last[:, ks(h)] - b)).astype(MXU_DTYPE)

    att = [_nt_dot(qd_ref[:, ks(h)], kd_ref[:, ks(h)]) for h in heads]
    att = [jnp.where(causal, a, 0.0).astype(MXU_DTYPE) for a in att]
    for h in heads:
        oacc_ref[:, vs(h)] = _dot(att[h], v_ref[:, vs(h)])

    st = [st_ref[h] for h in heads]
    for i in range(chunks_per_step):
        for h in heads:
            kv = _tn_dot(v_ref[chunk(i), vs(h)], kt_ref[chunk(i), ks(h)])
            oacc_ref[chunk(i), vs(h)] += _nt_dot(qd_ref[chunk(i), ks(h)], st[h].astype(MXU_DTYPE))
            st[h] = st[h] * decay[i][:, ks(h)] + kv
    for h in heads:
        st_ref[h] = st[h]
        oh = oacc_ref[:, vs(h)]
        y = oh * lax.rsqrt(jnp.mean(oh * oh, axis=-1, keepdims=True) + EPS)
        y = y * gg_ref[...]
        zz = z_ref[:, vs(h)].astype(jnp.float32)
        o_ref[:, vs(h)] = (y * _silu(zz)).astype(o_ref.dtype)


def _gla(u, small, w_up, b_a, g_gla, batch, seq, chunks_per_step):
    rows = chunks_per_step * GLA_CHUNK
    steps = seq // rows
    hk = GLA_HEADS * GLA_DK
    hv = GLA_HEADS * GLA_DV

    def rmap(cb):
        return lambda b, s: (b * steps + s, cb)

    return pl.pallas_call(
        functools.partial(_gla_kernel, chunks_per_step=chunks_per_step),
        out_shape=jax.ShapeDtypeStruct((batch * seq, hv), ACT_DTYPE),
        grid=(batch, steps),
        in_specs=[
            pl.BlockSpec((rows, hk), rmap(COL_GQ // hk)),
            pl.BlockSpec((rows, hk), rmap(COL_GK // hk)),
            pl.BlockSpec((rows, hv), rmap(COL_GV // hv)),
            pl.BlockSpec((rows, hv), rmap(COL_GZ // hv)),
            pl.BlockSpec((rows, LANES), rmap(0)),
            pl.BlockSpec((GLA_RANK, hk), lambda b, s: (0, 0)),
            pl.BlockSpec((1, hk), lambda b, s: (0, 0)),
            pl.BlockSpec((1, GLA_DV), lambda b, s: (0, 0)),
        ],
        out_specs=pl.BlockSpec((rows, hv), rmap(0)),
        scratch_shapes=[pltpu.VMEM((GLA_HEADS, GLA_DV, GLA_DK), jnp.float32),
                        pltpu.VMEM((rows, hk), MXU_DTYPE),
                        pltpu.VMEM((rows, hk), MXU_DTYPE),
                        pltpu.VMEM((rows, hk), MXU_DTYPE),
                        pltpu.VMEM((rows, hk), jnp.float32),
                        pltpu.VMEM((rows, hv), jnp.float32)],
        compiler_params=pltpu.CompilerParams(
            dimension_semantics=("arbitrary", "arbitrary"),
            vmem_limit_bytes=VMEM_LIMIT),
        name="gla",
    )(u, u, u, u, small, w_up, b_a, g_gla)


KEY_CHUNK = 4 * Q_BLOCK
BLOCKS_PER_CHUNK = KEY_CHUNK // Q_BLOCK
VT_ROWS = DSA_HD + 16
PLANE_GROUP = 32 * SUBLANES


def _t5_bucket(dist):
    max_exact = REL_BUCKETS // 2
    d = jnp.maximum(dist, 1).astype(jnp.float32)
    large = max_exact + jnp.floor(jnp.log(d / max_exact) / math.log(REL_MAX_DIST / max_exact)
                                  * (REL_BUCKETS - max_exact)).astype(jnp.int32)
    large = jnp.minimum(large, REL_BUCKETS - 1)
    return jnp.where(dist < max_exact, dist, large)


def _dsa_kernel(rb_ref, dq_ref, dk_ref, dv_ref, iq_ref, ikd_ref, sm_ref, dz_ref,
                out_ref,
                kt_ref, sc_ref, planes_ref, mb_ref, vt_ref, bt_ref, iqm_ref, qaug_ref, acc_ref,
                lga_ref, lgb_ref, mxa_ref, mxb_ref, ml_ref,
                *, seq):
    qb = pl.program_id(1)
    blk = Q_BLOCK
    ch = KEY_CHUNK
    k_sel = min(TOPK_MAX, seq // 4)
    n_chunks = qb // BLOCKS_PER_CHUNK + 1
    width = DSA_GROUPS * blk

    row_i = lax.broadcasted_iota(jnp.int32, (blk, blk), 0)
    col_i = lax.broadcasted_iota(jnp.int32, (blk, blk), 1)
    crow_i = lax.broadcasted_iota(jnp.int32, (ch, blk), 0)
    q_pos = qb * blk + lax.broadcasted_iota(jnp.int32, (ch, blk), 1)

    @pl.when((pl.program_id(0) == 0) & (qb == 0))
    def _():
        planes_ref[...] = jnp.zeros_like(planes_ref)
        kt_ref[...] = jnp.zeros_like(kt_ref)
        for delta in range(3):
            dist = jnp.maximum(delta * blk + col_i - row_i, 0)
            bucket = _t5_bucket(dist)
            for h in range(DSA_HEADS):
                tile = jnp.zeros((blk, blk), jnp.float32)
                for bk in range(REL_BUCKETS):
                    tile = jnp.where(bucket == bk, rb_ref[bk, h] * LOG2_E, tile)
                cc, g = divmod(h, DSA_GROUPS)
                bt_ref[delta, cc, :, g * blk:(g + 1) * blk] = tile

    @pl.when(qb == 0)
    def _():
        def body(i, carry):
            r = pl.multiple_of(i * ch, ch)
            v_t = dv_ref[pl.ds(r, ch), :].astype(jnp.float32).T.astype(vt_ref.dtype)
            for cc in range(DSA_KV_HEADS):
                vt_ref[i, cc * VT_ROWS:cc * VT_ROWS + DSA_HD, :] = v_t[cc * DSA_HD:(cc + 1) * DSA_HD]
                vt_ref[i, cc * VT_ROWS + DSA_HD:(cc + 1) * VT_ROWS, :] = jnp.ones(
                    (VT_ROWS - DSA_HD, ch), vt_ref.dtype)
            return carry
        lax.fori_loop(0, seq // ch, body, 0)

    w_t = sm_ref[...].T[SMALL_IW:SMALL_IW + IDX_HEADS, :]
    w_t = w_t * (IDX_HEADS ** -0.5) * (IDX_DIM ** -0.5)

    lane = lax.broadcasted_iota(jnp.int32, (blk, LANES), 1)
    for h in range(IDX_HEADS):
        pair = iq_ref[:, (h // 2) * LANES:(h // 2 + 1) * LANES]
        keep = (lane < IDX_DIM) if h % 2 == 0 else (lane >= IDX_DIM)
        iqm_ref[h // 2, (h % 2) * blk:(h % 2 + 1) * blk, :] = jnp.where(keep, pair, jnp.zeros_like(pair))

    def score_chunk(c, causal):
        r = pl.multiple_of(c * ch, ch)
        ikc = ikd_ref[pl.ds(r, ch), :]
        acc = None
        for hp in range(IDX_HEADS // 2):
            s2 = _nt_dot(ikc, iqm_ref[hp])
            for h in (2 * hp, 2 * hp + 1):
                t = w_t[h:h + 1, :] * jnp.maximum(s2[:, (h % 2) * blk:(h % 2 + 1) * blk], 0.0)
                acc = t if acc is None else acc + t
        if causal:
            acc = jnp.where(r + crow_i <= q_pos, acc, -jnp.inf)
        sc_ref[pl.ds(r, ch), :] = acc
        bits = lax.bitcast_convert_type(acc, jnp.int32)
        kt_ref[pl.ds(r, ch), :] = bits ^ ((bits >> 31) & 0x7FFFFFFF)

    def build_planes(c):
        for g in range(ch // PLANE_GROUP):
            base = c * ch + g * PLANE_GROUP
            a = [kt_ref[pl.ds(pl.multiple_of(base + SUBLANES * v, SUBLANES), SUBLANES), :]
                 for v in range(32)]
            j, m = 16, 0x0000FFFF
            while j:
                k0 = 0
                while k0 < 32:
                    t = (a[k0] ^ lax.shift_right_logical(a[k0 + j], jnp.int32(j))) & m
                    a[k0] = a[k0] ^ t
                    a[k0 + j] = a[k0 + j] ^ (t << j)
                    k0 = (k0 + j + 1) & ~j
                j >>= 1
                m = (m ^ (m << j)) & 0xFFFFFFFF if j else m
                m = m - (1 << 32) if m >= (1 << 31) else m
            a[0] = ~a[0]
            row = pl.multiple_of(c * (ch // 32) + g * SUBLANES, SUBLANES)
            for jj in range(32):
                planes_ref[jj, pl.ds(row, SUBLANES), :] = a[jj]

    @pl.when(n_chunks > 1)
    def _():
        score_chunk(0, False)

    def score_body(c, carry):
        build_planes(c - 1)
        score_chunk(c, False)
        return carry
    lax.fori_loop(1, n_chunks - 1, score_body, 0)
    build_planes(jnp.maximum(n_chunks - 2, 0))
    score_chunk(n_chunks - 1, True)
    build_planes(n_chunks - 1)

    n_rows = seq // 32

    def rowsum(x):
        part = jnp.sum(x.reshape(n_rows // SUBLANES, SUBLANES, blk), axis=0)
        return jnp.sum(part, axis=0, keepdims=True)

    prow = lax.broadcasted_iota(jnp.int32, (n_rows, blk), 0)
    alive0 = jnp.where(prow < n_chunks * (ch // 32), -1, 0).astype(jnp.int32)

    def bit_body(j, carry):
        alive, cnt_gt, ukey = carry
        w = planes_ref[j]
        ones = alive & w
        c1 = rowsum(lax.population_count(ones))
        take = cnt_gt + c1 >= k_sel
        alive = jnp.where(take, ones, alive & ~w)
        cnt_gt = jnp.where(take, cnt_gt, cnt_gt + c1)
        ukey = jnp.where(take, ukey | (jnp.int32(1) << (31 - j)), ukey)
        return alive, cnt_gt, ukey

    zero_row = jnp.zeros((1, blk), jnp.int32)
    alive, cnt_gt, ukey = lax.fori_loop(0, 32, bit_body, (alive0, zero_row, zero_row))
    ans = ukey ^ INT_MIN
    thr_bits = jnp.where(ans < 0, ans ^ 0x7FFFFFFF, ans)
    thr0 = lax.bitcast_convert_type(thr_bits, jnp.float32)

    def tile_sum(x):
        return jnp.sum(x.reshape(ch // SUBLANES, SUBLANES, blk), axis=0)

    def fold_rows(acc, combine):
        return functools.reduce(combine, [acc[i:i + 1] for i in range(SUBLANES)])

    def mask_pass(thr, p_max):
        def body(c, carry):
            a_gt, a_ge = carry
            r = pl.multiple_of(c * ch, ch)
            sc = sc_ref[pl.ds(r, ch), :]
            k_pos = r + crow_i
            gt = sc > thr
            ge = sc >= thr
            sel = (gt | (ge & (k_pos <= p_max))) & (k_pos <= q_pos)
            mb_ref[pl.ds(r, ch), :] = jnp.where(sel, 0.0, MASK_NEG).astype(mb_ref.dtype)
            return (a_gt + tile_sum(jnp.where(gt, 1, 0).astype(jnp.int32)),
                    a_ge + tile_sum(jnp.where(ge, 1, 0).astype(jnp.int32)))
        zero = jnp.zeros((SUBLANES, blk), jnp.int32)
        a_gt, a_ge = lax.fori_loop(0, n_chunks, body, (zero, zero))
        return fold_rows(a_gt, jnp.add), fold_rows(a_ge, jnp.add)

    def nearest(pred_fn, fill, combine):
        def body(c, acc):
            r = pl.multiple_of(c * ch, ch)
            sc = sc_ref[pl.ds(r, ch), :]
            part = jnp.where(pred_fn(sc), sc, fill).reshape(ch // SUBLANES, SUBLANES, blk)
            return combine(acc, functools.reduce(combine, [part[i] for i in range(ch // SUBLANES)]))
        acc = lax.fori_loop(0, n_chunks, body, jnp.full((SUBLANES, blk), fill, jnp.float32))
        return fold_rows(acc, combine)

    def status(c_gt, c_ge):
        off = (c_gt >= k_sel) | (c_ge < k_sel)
        return jnp.max(jnp.where(off, 2, 0) | jnp.where(c_ge > k_sel, 1, 0))

    def walk(state):
        thr, c_gt, c_ge, _, trips = state
        above = nearest(lambda sc: sc > thr, jnp.inf, jnp.minimum)
        below = nearest(lambda sc: sc < thr, -jnp.inf, jnp.maximum)
        thr = jnp.where(c_gt >= k_sel, above, jnp.where(c_ge < k_sel, below, thr))
        c_gt, c_ge = mask_pass(thr, no_bound)
        return thr, c_gt, c_ge, status(c_gt, c_ge), trips + 1

    no_bound = jnp.full((1, blk), 2 * seq, jnp.int32)
    cnt_gt0, cnt_ge0 = mask_pass(thr0, no_bound)
    thr, cnt_gt, _, flag, _ = lax.while_loop(
        lambda state: (state[3] >= 2) & (state[4] < seq), walk,
        (thr0, cnt_gt0, cnt_ge0, status(cnt_gt0, cnt_ge0), jnp.int32(0)))

    @pl.when(flag == 1)
    def _():
        need = k_sel - cnt_gt
        n_bits = (2 * seq - 1).bit_length()

        def tie_count(cand):
            def body(c, acc):
                r = pl.multiple_of(c * ch, ch)
                hit = (sc_ref[pl.ds(r, ch), :] == thr) & (r + crow_i < cand)
                return acc + tile_sum(jnp.where(hit, 1, 0).astype(jnp.int32))
            return fold_rows(lax.fori_loop(0, n_chunks, body, jnp.zeros((SUBLANES, blk), jnp.int32)), jnp.add)

        def pos_body(i, p):
            cand = p | (jnp.int32(1) << (n_bits - 1 - i))
            return jnp.where(tie_count(cand) <= need - 1, cand, p)
        p_max = lax.fori_loop(0, n_bits, pos_body, jnp.zeros((1, blk), jnp.int32))
        mask_pass(thr, p_max)

    eye = jnp.where(row_i == col_i, 1.0, 0.0).astype(MXU_DTYPE)
    for cc in range(DSA_KV_HEADS):
        for g in range(DSA_GROUPS):
            h = cc * DSA_GROUPS + g
            qaug_ref[cc, g * blk:(g + 1) * blk, 0:DSA_HD] = dq_ref[:, h * DSA_HD:(h + 1) * DSA_HD]
            qaug_ref[cc, g * blk:(g + 1) * blk, DSA_HD:2 * DSA_HD] = eye
    acc_ref[...] = jnp.zeros_like(acc_ref)
    for cc in range(DSA_KV_HEADS):
        ml_ref[cc, 0] = jnp.full(ml_ref.shape[2:], -jnp.inf, jnp.float32)
        ml_ref[cc, 1] = jnp.zeros(ml_ref.shape[2:], jnp.float32)

    def stage_logits(c, lg_ref, mx_ref, far):
        r = pl.multiple_of(c * ch, ch)
        mbc = mb_ref[pl.ds(r, ch), :]
        for cc in range(DSA_KV_HEADS):
            kaug = jnp.concatenate([dk_ref[pl.ds(r, ch), cc * DSA_HD:(cc + 1) * DSA_HD], mbc], axis=1)
            lg = _nt_dot(kaug, qaug_ref[cc])
            if far:
                off = bt_ref[2, cc, 0:1, :]
            else:
                lg = lg + jnp.concatenate(
                    [bt_ref[jnp.clip(qb - (c * BLOCKS_PER_CHUNK + j), 0, 2), cc]
                     for j in range(BLOCKS_PER_CHUNK)], axis=0)
                off = jnp.zeros((1, width), jnp.float32)
            lg_ref[cc] = lg
            mx_ref[cc, 0] = jnp.broadcast_to(jnp.max(lg, axis=0, keepdims=True) + off, mx_ref.shape[2:])
            mx_ref[cc, 1] = jnp.broadcast_to(off, mx_ref.shape[2:])

    def stage_softmax(c, lg_ref, mx_ref):
        for cc in range(DSA_KV_HEADS):
            m = ml_ref[cc, 0, 0:1, :]
            l = ml_ref[cc, 1, 0:1, :]
            m_new = jnp.maximum(m, mx_ref[cc, 0, 0:1, :])
            alpha = jnp.exp2(m - m_new)
            p = jnp.exp2(lg_ref[cc] - (m_new - mx_ref[cc, 1, 0:1, :]))
            pv = _dot(vt_ref[c, cc * VT_ROWS:(cc + 1) * VT_ROWS, :], p.astype(MXU_DTYPE))
            l_new = alpha * l + pv[DSA_HD:DSA_HD + 1, :]
            acc_ref[cc] = acc_ref[cc] * alpha + pv[:DSA_HD]
            ml_ref[cc, 0] = jnp.broadcast_to(m_new, ml_ref.shape[2:])
            ml_ref[cc, 1] = jnp.broadcast_to(l_new, ml_ref.shape[2:])

    c_near = jnp.maximum(qb - 1, 0) // BLOCKS_PER_CHUNK

    @pl.when(c_near > 0)
    def _():
        stage_logits(0, lga_ref, mxa_ref, far=True)

    @pl.when(c_near == 0)
    def _():
        stage_logits(0, lga_ref, mxa_ref, far=False)

    def pair_body(pi, carry, far):
        c0 = 2 * pi
        stage_logits(c0 + 1, lgb_ref, mxb_ref, far)
        stage_softmax(c0, lga_ref, mxa_ref)

        stage_logits(jnp.minimum(c0 + 2, n_chunks - 1), lga_ref, mxa_ref, far)
        stage_softmax(c0 + 1, lgb_ref, mxb_ref)
        return carry

    far_pairs = jnp.maximum(c_near - 1, 0) // 2
    lax.fori_loop(0, far_pairs, functools.partial(pair_body, far=True), 0)
    lax.fori_loop(far_pairs, n_chunks // 2, functools.partial(pair_body, far=False), 0)

    @pl.when(n_chunks % 2 == 1)
    def _():
        stage_softmax(n_chunks - 1, lga_ref, mxa_ref)

    for cc in range(DSA_KV_HEADS):
        o_t = acc_ref[cc] / ml_ref[cc, 1, 0:1, :]
        for g in range(DSA_GROUPS):
            h = cc * DSA_GROUPS + g
            o = o_t[:, g * blk:(g + 1) * blk].T
            zz = dz_ref[:, h * DSA_HD:(h + 1) * DSA_HD].astype(jnp.float32)
            out_ref[:, h * DSA_HD:(h + 1) * DSA_HD] = (o * _silu(zz)).astype(out_ref.dtype)


def _dsa(u, ikd, small, rel_bias, batch, seq):
    nb = seq // Q_BLOCK
    hq = DSA_HEADS * DSA_HD
    hkv = DSA_KV_HEADS * DSA_HD
    hi = IDX_HEADS * IDX_DIM

    def qmap(cb):
        return lambda b, q: (b * nb + q, cb)

    def bmap(cb):
        return lambda b, q: (b, cb)

    return pl.pallas_call(
        functools.partial(_dsa_kernel, seq=seq),
        out_shape=jax.ShapeDtypeStruct((batch * seq, hq), ACT_DTYPE),
        grid=(batch, nb),
        in_specs=[
            pl.BlockSpec(memory_space=pltpu.SMEM),
            pl.BlockSpec((Q_BLOCK, hq), qmap(COL_DQ // hq)),
            pl.BlockSpec((seq, hkv), bmap(COL_DK // hkv)),
            pl.BlockSpec((seq, hkv), bmap(COL_DV // hkv)),
            pl.BlockSpec((Q_BLOCK, hi), qmap(COL_IQ // hi)),
            pl.BlockSpec((seq, LANES), bmap(0)),
            pl.BlockSpec((Q_BLOCK, LANES), qmap(0)),
            pl.BlockSpec((Q_BLOCK, hq), qmap(COL_DZ // hq)),
        ],
        out_specs=pl.BlockSpec((Q_BLOCK, hq), qmap(0)),
        scratch_shapes=[
            pltpu.VMEM((seq, Q_BLOCK), jnp.int32),
            pltpu.VMEM((seq, Q_BLOCK), jnp.float32),
            pltpu.VMEM((32, seq // 32, Q_BLOCK), jnp.int32),
            pltpu.VMEM((seq, Q_BLOCK), MXU_DTYPE),
            pltpu.VMEM((seq // KEY_CHUNK, DSA_KV_HEADS * VT_ROWS, KEY_CHUNK), MXU_DTYPE),
            pltpu.VMEM((3, DSA_KV_HEADS, Q_BLOCK, DSA_GROUPS * Q_BLOCK), jnp.float32),
            pltpu.VMEM((IDX_HEADS // 2, 2 * Q_BLOCK, LANES), MXU_DTYPE),
            pltpu.VMEM((DSA_KV_HEADS, DSA_GROUPS * Q_BLOCK, 2 * DSA_HD), MXU_DTYPE),
            pltpu.VMEM((DSA_KV_HEADS, DSA_HD, DSA_GROUPS * Q_BLOCK), jnp.float32),
            pltpu.VMEM((DSA_KV_HEADS, KEY_CHUNK, DSA_GROUPS * Q_BLOCK), jnp.float32),
            pltpu.VMEM((DSA_KV_HEADS, KEY_CHUNK, DSA_GROUPS * Q_BLOCK), jnp.float32),
            pltpu.VMEM((DSA_KV_HEADS, 2, SUBLANES, DSA_GROUPS * Q_BLOCK), jnp.float32),
            pltpu.VMEM((DSA_KV_HEADS, 2, SUBLANES, DSA_GROUPS * Q_BLOCK), jnp.float32),
            pltpu.VMEM((DSA_KV_HEADS, 2, SUBLANES, DSA_GROUPS * Q_BLOCK), jnp.float32),
        ],
        compiler_params=pltpu.CompilerParams(
            dimension_semantics=("arbitrary", "arbitrary"),
            vmem_limit_bytes=VMEM_LIMIT),
        name="dsa",
    )(rel_bias, u, u, u, u, ikd, small, u)


def _memkv_kernel(mem_ref, g_ref, w_ref, o_ref):
    xf = mem_ref[...]
    y = xf * lax.rsqrt(jnp.mean(xf * xf, axis=-1, keepdims=True) + EPS)
    hb = (y * g_ref[...]).astype(MXU_DTYPE)
    o_ref[...] = _dot(hb, w_ref[...]).astype(o_ref.dtype)


def _memkv(mem2, g_mem, w_kv, batch):
    n = 2 * X_HEADS * X_HD
    return pl.pallas_call(
        _memkv_kernel,
        out_shape=jax.ShapeDtypeStruct((batch * N_MEM, n), ACT_DTYPE),
        grid=(batch,),
        in_specs=[
            pl.BlockSpec((N_MEM, D_MODEL), lambda b: (b, 0)),
            pl.BlockSpec((1, D_MODEL), lambda b: (0, 0)),
            pl.BlockSpec((D_MODEL, n), lambda b: (0, 0)),
        ],
        out_specs=pl.BlockSpec((N_MEM, n), lambda b: (b, 0)),
        compiler_params=pltpu.CompilerParams(
            dimension_semantics=("arbitrary",),
            vmem_limit_bytes=VMEM_LIMIT),
        name="memkv",
    )(mem2, g_mem, w_kv)


def _merge_kernel(x_ref, yg_ref, yd_ref, xq_ref, xz_ref, sg_ref, sd_ref, sm_ref, mkv_ref,
                  wg_ref, wd_ref, wx_ref, wo_ref, gp_ref, o_ref, ym_ref):
    hw = X_HEADS * X_HD
    heads = range(X_HEADS)

    def cs(h):
        return slice(h * X_HD, (h + 1) * X_HD)

    lgs = [_nt_dot(xq_ref[:, cs(h)], mkv_ref[:, cs(h)]) for h in heads]
    t_g = _dot(yg_ref[...], wg_ref[...])
    ps = []
    for lg in lgs:
        e = jnp.exp(lg - jnp.max(lg, axis=-1, keepdims=True))
        ps.append((e / jnp.sum(e, axis=-1, keepdims=True)).astype(MXU_DTYPE))
    outs = [_dot(ps[h], mkv_ref[:, hw + h * X_HD:hw + (h + 1) * X_HD]) for h in heads]
    t_d = _dot(yd_ref[...], wd_ref[...])
    for h in heads:
        ym_ref[:, cs(h)] = (outs[h] * _silu(xz_ref[:, cs(h)].astype(jnp.float32))).astype(ym_ref.dtype)
    merged = jax.nn.sigmoid(sg_ref[...].astype(jnp.float32)) * t_g
    merged = merged + jax.nn.sigmoid(sd_ref[...].astype(jnp.float32)) * t_d
    merged = merged + jax.nn.sigmoid(sm_ref[...].astype(jnp.float32)) * _dot(ym_ref[...], wx_ref[...])
    t = _dot(merged.astype(MXU_DTYPE), wo_ref[...])
    y = t * lax.rsqrt(jnp.mean(t * t, axis=-1, keepdims=True) + EPS)
    o_ref[...] = x_ref[...] + y * gp_ref[...]


def _merge(x2, y_gla, y_dsa, u, mkv, w_g, w_d, w_x, w_o, g_post, seq, tm):
    m = x2.shape[0]
    steps_per_batch = seq // tm
    d = D_MODEL

    def rmap(cb):
        return lambda i: (i, cb)

    wspec = pl.BlockSpec((d, d), lambda i: (0, 0))
    return pl.pallas_call(
        _merge_kernel,
        out_shape=jax.ShapeDtypeStruct((m, d), jnp.float32),
        grid=(m // tm,),
        in_specs=[
            pl.BlockSpec((tm, d), rmap(0)),
            pl.BlockSpec((tm, d), rmap(0)),
            pl.BlockSpec((tm, d), rmap(0)),
            pl.BlockSpec((tm, d), rmap(COL_XQ // d)),
            pl.BlockSpec((tm, d), rmap(COL_XZ // d)),
            pl.BlockSpec((tm, d), rmap(COL_SG // d)),
            pl.BlockSpec((tm, d), rmap(COL_SD // d)),
            pl.BlockSpec((tm, d), rmap(COL_SM // d)),
            pl.BlockSpec((N_MEM, 2 * X_HEADS * X_HD), lambda i: (i // steps_per_batch, 0)),
            wspec, wspec, wspec, wspec,
            pl.BlockSpec((1, d), lambda i: (0, 0)),
        ],
        out_specs=pl.BlockSpec((tm, d), rmap(0)),
        scratch_shapes=[pltpu.VMEM((tm, d), MXU_DTYPE)],
        compiler_params=pltpu.CompilerParams(
            dimension_semantics=("arbitrary",),
            vmem_limit_bytes=VMEM_LIMIT),
        name="merge",
    )(x2, y_gla, y_dsa, u, u, u, u, u, mkv, w_g, w_d, w_x, w_o, g_post)


_MAIN_GROUPS = (0, 1, 2, 4, 5, 11, 12, 13, 14, 6, 7, 8)
_GROUP_GA, _GROUP_IK, _GROUP_IW = 3, 9, 10
RELAYOUT_COLS = 128


def _relayout_kernel(wt_ref, main_ref, small_ref):
    offs = np.concatenate([[0], np.cumsum(np.array(SPLIT_SIZES))]).tolist()

    def group(g):
        return wt_ref[offs[g]:offs[g + 1], :].astype(MXU_DTYPE)

    dst = 0
    for g in _MAIN_GROUPS:
        main_ref[dst:dst + SPLIT_SIZES[g], :] = group(g)
        dst += SPLIT_SIZES[g]
    pad = jnp.zeros((LANES - GLA_RANK - IDX_HEADS, wt_ref.shape[1]), MXU_DTYPE)
    small_ref[...] = jnp.concatenate(
        [group(_GROUP_IK), group(_GROUP_IK), group(_GROUP_GA), group(_GROUP_IW), pad], axis=0)


def _relayout_w_in(w_in):
    wt = w_in.T
    n, d = wt.shape
    return pl.pallas_call(
        _relayout_kernel,
        out_shape=(jax.ShapeDtypeStruct((U_COLS, d), MXU_DTYPE),
                   jax.ShapeDtypeStruct((2 * LANES, d), MXU_DTYPE)),
        grid=(d // RELAYOUT_COLS,),
        in_specs=[pl.BlockSpec((n, RELAYOUT_COLS), lambda i: (0, i))],
        out_specs=(pl.BlockSpec((U_COLS, RELAYOUT_COLS), lambda i: (0, i)),
                   pl.BlockSpec((2 * LANES, RELAYOUT_COLS), lambda i: (0, i))),
        compiler_params=pltpu.CompilerParams(
            dimension_semantics=("arbitrary",),
            vmem_limit_bytes=VMEM_LIMIT),
        name="relayout",
    )(wt)


def _col_scale():
    s = np.ones((1, U_COLS), np.float32)
    s[:, COL_GQ:COL_GQ + GLA_HEADS * GLA_DK] = GLA_DK ** -0.5
    s[:, COL_DQ:COL_DQ + DSA_HEADS * DSA_HD] = DSA_HD ** -0.5 * LOG2_E
    s[:, COL_XQ:COL_XQ + X_HEADS * X_HD] = X_HD ** -0.5
    return jnp.asarray(s)


def _layer(x2, mem2, g_pre, g_post, g_mem, w_in, w_up, b_a, g_gla, rel_bias, w_kv,
           w_g, w_d, w_x, w_o, batch, seq):
    w_main, w_small = _relayout_w_in(w_in)
    tm = min(PROJ_TM, batch * seq)
    u, ikd, small = _proj(x2, g_pre.reshape(1, -1), w_main, w_small, _col_scale(), tm, PROJ_TN)
    y_gla = _gla(u, small, w_up.astype(MXU_DTYPE), b_a.reshape(1, -1), g_gla.reshape(1, -1),
                 batch, seq, chunks_per_step=4)
    y_dsa = _dsa(u, ikd, small, rel_bias, batch, seq)
    mkv = _memkv(mem2, g_mem.reshape(1, -1), w_kv.astype(MXU_DTYPE), batch)
    return _merge(x2, y_gla, y_dsa, u, mkv, w_g.astype(MXU_DTYPE), w_d.astype(MXU_DTYPE),
                  w_x.astype(MXU_DTYPE), w_o.astype(MXU_DTYPE), g_post.reshape(1, -1), seq, MERGE_TM)


def kernel(x, mem, g_pre, g_post, g_mem, w_in, w_gla_a_up, b_gla_a, g_gla, rel_bias,
           w_mem_kv, w_gla_out, w_dsa_out, w_x_out, w_o):
    batch, seq, d = x.shape
    x2 = x.reshape(batch * seq, d)
    mem2 = mem.reshape(batch * N_MEM, d)
    for i in range(g_pre.shape[0]):
        x2 = _layer(x2, mem2, g_pre[i], g_post[i], g_mem[i], w_in[i], w_gla_a_up[i],
                    b_gla_a[i], g_gla[i], rel_bias, w_mem_kv[i], w_gla_out[i],
                    w_dsa_out[i], w_x_out[i], w_o[i], batch, seq)
    return x2.reshape(batch, seq, d)
```

```python
import functools
import math

import jax
import jax.numpy as jnp
import numpy as np
from jax import lax
from jax.experimental import pallas as pl
from jax.experimental.pallas import tpu as pltpu

D_MODEL = 1024
N_MEM = 256
EPS = 1e-6
GLA_HEADS = 4
GLA_DK = 128
GLA_DV = 256
GLA_RANK = 16
GLA_GATE_NORM = 16.0
GLA_CHUNK = 64
DSA_HEADS = 8
DSA_KV_HEADS = 2
DSA_GROUPS = DSA_HEADS // DSA_KV_HEADS
DSA_HD = 128
IDX_HEADS = 8
IDX_DIM = 64
TOPK_MAX = 256
Q_BLOCK = 128
REL_BUCKETS = 32
REL_MAX_DIST = 128
X_HEADS = 4
X_HD = 256

SPLIT_SIZES = (512, 512, 1024, 16, 1024, 1024, 256, 256, 512, 64, 8, 1024, 1024, 1024, 3072)

LANES = 128
SUBLANES = 8

MXU_DTYPE = jnp.bfloat16
ACT_DTYPE = jnp.bfloat16

U_COLS = 11264
COL_GQ, COL_GK, COL_GV, COL_GZ = 0, 512, 1024, 2048
COL_DQ, COL_DZ, COL_XQ, COL_XZ = 3072, 4096, 5120, 6144
COL_SG, COL_SD, COL_SM = 7168, 8192, 9216
COL_DK, COL_DV, COL_IQ = 10240, 10496, 10752
SMALL_GA, SMALL_IW = 0, 16

MASK_NEG = -1e30
LOG2_E = math.log2(math.e)
INT_MIN = -(2 ** 31)
VMEM_LIMIT = 56 * 1024 * 1024

PROJ_TM = 1024
PROJ_TN = U_COLS // 4
MERGE_TM = 512


def _nt_dot(a, b):
    return lax.dot_general(a, b, (((1,), (1,)), ((), ())),
                           preferred_element_type=jnp.float32)


def _tn_dot(a, b):
    return lax.dot_general(a, b, (((0,), (0,)), ((), ())),
                           preferred_element_type=jnp.float32)


def _dot(a, b):
    return jnp.dot(a, b, preferred_element_type=jnp.float32)


def _silu(z):
    return z * jax.nn.sigmoid(z)


def _proj_kernel(x_ref, g_ref, w_ref, ws_ref, cs_ref, u_ref, ikd_ref, sm_ref, h_ref):
    @pl.when(pl.program_id(1) == 0)
    def _():
        xf = x_ref[...]
        y = xf * lax.rsqrt(jnp.mean(xf * xf, axis=-1, keepdims=True) + EPS)
        hb = (y * g_ref[...]).astype(MXU_DTYPE)
        h_ref[...] = hb
        r = _nt_dot(hb, ws_ref[...])
        ikd_ref[...] = r[:, :LANES].astype(ikd_ref.dtype)
        sm_ref[...] = r[:, LANES:]

    acc = _nt_dot(h_ref[...], w_ref[...])
    u_ref[...] = (acc * cs_ref[...]).astype(u_ref.dtype)


def _proj(x2, g_pre, w_main, w_small, col_scale, tm, tn):
    m = x2.shape[0]
    grid = (m // tm, U_COLS // tn)
    return pl.pallas_call(
        _proj_kernel,
        out_shape=(jax.ShapeDtypeStruct((m, U_COLS), ACT_DTYPE),
                   jax.ShapeDtypeStruct((m, LANES), ACT_DTYPE),
                   jax.ShapeDtypeStruct((m, LANES), jnp.float32)),
        grid=grid,
        in_specs=[
            pl.BlockSpec((tm, D_MODEL), lambda i, j: (i, 0)),
            pl.BlockSpec((1, D_MODEL), lambda i, j: (0, 0)),
            pl.BlockSpec((tn, D_MODEL), lambda i, j: (j, 0)),
            pl.BlockSpec((2 * LANES, D_MODEL), lambda i, j: (0, 0)),
            pl.BlockSpec((1, tn), lambda i, j: (0, j)),
        ],
        out_specs=(
            pl.BlockSpec((tm, tn), lambda i, j: (i, j)),
            pl.BlockSpec((tm, LANES), lambda i, j: (i, 0)),
            pl.BlockSpec((tm, LANES), lambda i, j: (i, 0)),
        ),
        scratch_shapes=[pltpu.VMEM((tm, D_MODEL), MXU_DTYPE)],
        compiler_params=pltpu.CompilerParams(
            dimension_semantics=("arbitrary", "arbitrary"),
            vmem_limit_bytes=VMEM_LIMIT),
        name="proj",
    )(x2, g_pre, w_main, w_small, col_scale)


def _split3(x):
    hi = x.astype(MXU_DTYPE)
    r1 = x - hi.astype(jnp.float32)
    mid = r1.astype(MXU_DTYPE)
    lo = (r1 - mid.astype(jnp.float32)).astype(MXU_DTYPE)
    return hi, mid, lo


def _gla_kernel(q_ref, k_ref, v_ref, z_ref, sm_ref, wup_ref, ba_ref, gg_ref,
                o_ref, st_ref, qd_ref, kd_ref, kt_ref, b_ref, oacc_ref, *, chunks_per_step):
    c = GLA_CHUNK
    rows = chunks_per_step * c
    heads = range(GLA_HEADS)

    @pl.when(pl.program_id(1) == 0)
    def _():
        st_ref[...] = jnp.zeros_like(st_ref)

    def ks(h):
        return slice(h * GLA_DK, (h + 1) * GLA_DK)

    def vs(h):
        return slice(h * GLA_DV, (h + 1) * GLA_DV)

    def chunk(i):
        return slice(i * c, (i + 1) * c)

    row = lax.broadcasted_iota(jnp.int32, (rows, rows), 0)
    col = lax.broadcasted_iota(jnp.int32, (rows, rows), 1)
    causal = (row // c == col // c) & (col <= row)
    tril = jnp.where(causal, 1.0, 0.0).astype(MXU_DTYPE)

    ga = sm_ref[:, SMALL_GA:SMALL_GA + GLA_RANK].astype(MXU_DTYPE)
    pre = _dot(ga, wup_ref[...]) + ba_ref[...]
    log_a = (jnp.minimum(pre, 0.0) - jnp.log1p(jnp.exp(-jnp.abs(pre)))) / GLA_GATE_NORM
    hi, mid, lo = _split3(log_a)
    b_ref[...] = _dot(tril, hi) + _dot(tril, mid) + _dot(tril, lo)

    decay = []
    for i in range(chunks_per_step):
        b_last = b_ref[i * c + c - 1:i * c + c, :]
        decay.append(jnp.exp(b_last))
        for h in heads:
            b = b_ref[chunk(i), ks(h)]
            q = q_ref[chunk(i), ks(h)].astype(jnp.float32)
            k = k_ref[chunk(i), ks(h)].astype(jnp.float32)
            qd_ref[chunk(i), ks(h)] = (q * jnp.exp(b)).astype(MXU_DTYPE)
            kd_ref[chunk(i), ks(h)] = (k * jnp.exp(-b)).astype(MXU_DTYPE)
            kt_ref[chunk(i), ks(h)] = (k * jnp.exp(b_last[:, ks(h)] - b)).astype(MXU_DTYPE)

    att = [_nt_dot(qd_ref[:, ks(h)], kd_ref[:, ks(h)]) for h in heads]
    att = [jnp.where(causal, a, 0.0).astype(MXU_DTYPE) for a in att]
    for h in heads:
        oacc_ref[:, vs(h)] = _dot(att[h], v_ref[:, vs(h)])

    st = [st_ref[h] for h in heads]
    for i in range(chunks_per_step):
        for h in heads:
            kv = _tn_dot(v_ref[chunk(i), vs(h)], kt_ref[chunk(i), ks(h)])
            oacc_ref[chunk(i), vs(h)] += _nt_dot(qd_ref[chunk(i), ks(h)], st[h].astype(MXU_DTYPE))
            st[h] = st[h] * decay[i][:, ks(h)] + kv
    for h in heads:
        st_ref[h] = st[h]
        oh = oacc_ref[:, vs(h)]
        y = oh * lax.rsqrt(jnp.mean(oh * oh, axis=-1, keepdims=True) + EPS)
        y = y * gg_ref[...]
        zz = z_ref[:, vs(h)].astype(jnp.float32)
        o_ref[:, vs(h)] = (y * _silu(zz)).astype(o_ref.dtype)


def _gla(u, small, w_up, b_a, g_gla, batch, seq, chunks_per_step):
    rows = chunks_per_step * GLA_CHUNK
    steps = seq // rows
    hk = GLA_HEADS * GLA_DK
    hv = GLA_HEADS * GLA_DV

    def rmap(cb):
        return lambda b, s: (b * steps + s, cb)

    return pl.pallas_call(
        functools.partial(_gla_kernel, chunks_per_step=chunks_per_step),
        out_shape=jax.ShapeDtypeStruct((batch * seq, hv), ACT_DTYPE),
        grid=(batch, steps),
        in_specs=[
            pl.BlockSpec((rows, hk), rmap(COL_GQ // hk)),
            pl.BlockSpec((rows, hk), rmap(COL_GK // hk)),
            pl.BlockSpec((rows, hv), rmap(COL_GV // hv)),
            pl.BlockSpec((rows, hv), rmap(COL_GZ // hv)),
            pl.BlockSpec((rows, LANES), rmap(0)),
            pl.BlockSpec((GLA_RANK, hk), lambda b, s: (0, 0)),
            pl.BlockSpec((1, hk), lambda b, s: (0, 0)),
            pl.BlockSpec((1, GLA_DV), lambda b, s: (0, 0)),
        ],
        out_specs=pl.BlockSpec((rows, hv), rmap(0)),
        scratch_shapes=[pltpu.VMEM((GLA_HEADS, GLA_DV, GLA_DK), jnp.float32),
                        pltpu.VMEM((rows, hk), MXU_DTYPE),
                        pltpu.VMEM((rows, hk), MXU_DTYPE),
                        pltpu.VMEM((rows, hk), MXU_DTYPE),
                        pltpu.VMEM((rows, hk), jnp.float32),
                        pltpu.VMEM((rows, hv), jnp.float32)],
        compiler_params=pltpu.CompilerParams(
            dimension_semantics=("arbitrary", "arbitrary"),
            vmem_limit_bytes=VMEM_LIMIT),
        name="gla",
    )(u, u, u, u, small, w_up, b_a, g_gla)


KEY_CHUNK = 4 * Q_BLOCK
BLOCKS_PER_CHUNK = KEY_CHUNK // Q_BLOCK
VT_ROWS = DSA_HD + 16
PLANE_GROUP = 32 * SUBLANES


def _t5_bucket(dist):
    max_exact = REL_BUCKETS // 2
    d = jnp.maximum(dist, 1).astype(jnp.float32)
    large = max_exact + jnp.floor(jnp.log(d / max_exact) / math.log(REL_MAX_DIST / max_exact)
                                  * (REL_BUCKETS - max_exact)).astype(jnp.int32)
    large = jnp.minimum(large, REL_BUCKETS - 1)
    return jnp.where(dist < max_exact, dist, large)


def _dsa_kernel(rb_ref, dq_ref, dk_ref, dv_ref, iq_ref, ikd_ref, sm_ref, dz_ref,
                out_ref,
                kt_ref, sc_ref, planes_ref, mb_ref, vt_ref, bt_ref, iqm_ref, qaug_ref, acc_ref,
                lga_ref, lgb_ref, mxa_ref, mxb_ref, ml_ref,
                *, seq):
    qb = pl.program_id(1)
    blk = Q_BLOCK
    ch = KEY_CHUNK
    k_sel = min(TOPK_MAX, seq // 4)
    n_chunks = qb // BLOCKS_PER_CHUNK + 1
    width = DSA_GROUPS * blk

    row_i = lax.broadcasted_iota(jnp.int32, (blk, blk), 0)
    col_i = lax.broadcasted_iota(jnp.int32, (blk, blk), 1)
    crow_i = lax.broadcasted_iota(jnp.int32, (ch, blk), 0)
    q_pos = qb * blk + lax.broadcasted_iota(jnp.int32, (ch, blk), 1)

    @pl.when((pl.program_id(0) == 0) & (qb == 0))
    def _():
        planes_ref[...] = jnp.zeros_like(planes_ref)
        kt_ref[...] = jnp.zeros_like(kt_ref)
        for delta in range(3):
            dist = jnp.maximum(delta * blk + col_i - row_i, 0)
            bucket = _t5_bucket(dist)
            for h in range(DSA_HEADS):
                tile = jnp.zeros((blk, blk), jnp.float32)
                for bk in range(REL_BUCKETS):
                    tile = jnp.where(bucket == bk, rb_ref[bk, h] * LOG2_E, tile)
                cc, g = divmod(h, DSA_GROUPS)
                bt_ref[delta, cc, :, g * blk:(g + 1) * blk] = tile

    @pl.when(qb == 0)
    def _():
        def body(i, carry):
            r = pl.multiple_of(i * ch, ch)
            v_t = dv_ref[pl.ds(r, ch), :].astype(jnp.float32).T.astype(vt_ref.dtype)
            for cc in range(DSA_KV_HEADS):
                vt_ref[i, cc * VT_ROWS:cc * VT_ROWS + DSA_HD, :] = v_t[cc * DSA_HD:(cc + 1) * DSA_HD]
                vt_ref[i, cc * VT_ROWS + DSA_HD:(cc + 1) * VT_ROWS, :] = jnp.ones(
                    (VT_ROWS - DSA_HD, ch), vt_ref.dtype)
            return carry
        lax.fori_loop(0, seq // ch, body, 0)

    w_t = sm_ref[...].T[SMALL_IW:SMALL_IW + IDX_HEADS, :]
    w_t = w_t * (IDX_HEADS ** -0.5) * (IDX_DIM ** -0.5)

    lane = lax.broadcasted_iota(jnp.int32, (blk, LANES), 1)
    for h in range(IDX_HEADS):
        pair = iq_ref[:, (h // 2) * LANES:(h // 2 + 1) * LANES]
        keep = (lane < IDX_DIM) if h % 2 == 0 else (lane >= IDX_DIM)
        iqm_ref[h // 2, (h % 2) * blk:(h % 2 + 1) * blk, :] = jnp.where(keep, pair, jnp.zeros_like(pair))

    def score_chunk(c, causal):
        r = pl.multiple_of(c * ch, ch)
        ikc = ikd_ref[pl.ds(r, ch), :]
        acc = None
        for hp in range(IDX_HEADS // 2):
            s2 = _nt_dot(ikc, iqm_ref[hp])
            for h in (2 * hp, 2 * hp + 1):
                t = w_t[h:h + 1, :] * jnp.maximum(s2[:, (h % 2) * blk:(h % 2 + 1) * blk], 0.0)
                acc = t if acc is None else acc + t
        if causal:
            acc = jnp.where(r + crow_i <= q_pos, acc, -jnp.inf)
        sc_ref[pl.ds(r, ch), :] = acc
        bits = lax.bitcast_convert_type(acc, jnp.int32)
        kt_ref[pl.ds(r, ch), :] = bits ^ ((bits >> 31) & 0x7FFFFFFF)

    def build_planes(c):
        for g in range(ch // PLANE_GROUP):
            base = c * ch + g * PLANE_GROUP
            a = [kt_ref[pl.ds(pl.multiple_of(base + SUBLANES * v, SUBLANES), SUBLANES), :]
                 for v in range(32)]
            j, m = 16, 0x0000FFFF
            while j:
                k0 = 0
                while k0 < 32:
                    t = (a[k0] ^ lax.shift_right_logical(a[k0 + j], jnp.int32(j))) & m
                    a[k0] = a[k0] ^ t
                    a[k0 + j] = a[k0 + j] ^ (t << j)
                    k0 = (k0 + j + 1) & ~j
                j >>= 1
                m = (m ^ (m << j)) & 0xFFFFFFFF if j else m
                m = m - (1 << 32) if m >= (1 << 31) else m
            a[0] = ~a[0]
            row = pl.multiple_of(c * (ch // 32) + g * SUBLANES, SUBLANES)
            for jj in range(32):
                planes_ref[jj, pl.ds(row, SUBLANES), :] = a[jj]

    @pl.when(n_chunks > 1)
    def _():
        score_chunk(0, False)

    def score_body(c, carry):
        build_planes(c - 1)
        score_chunk(c, False)
        return carry
    lax.fori_loop(1, n_chunks - 1, score_body, 0)
    build_planes(jnp.maximum(n_chunks - 2, 0))
    score_chunk(n_chunks - 1, True)
    build_planes(n_chunks - 1)

    n_rows = seq // 32

    def rowsum(x):
        part = jnp.sum(x.reshape(n_rows // SUBLANES, SUBLANES, blk), axis=0)
        return jnp.sum(part, axis=0, keepdims=True)

    prow = lax.broadcasted_iota(jnp.int32, (n_rows, blk), 0)
    alive0 = jnp.where(prow < n_chunks * (ch // 32), -1, 0).astype(jnp.int32)

    def bit_body(j, carry):
        alive, cnt_gt, ukey = carry
        w = planes_ref[j]
        ones = alive & w
        c1 = rowsum(lax.population_count(ones))
        take = cnt_gt + c1 >= k_sel
        alive = jnp.where(take, ones, alive & ~w)
        cnt_gt = jnp.where(take, cnt_gt, cnt_gt + c1)
        ukey = jnp.where(take, ukey | (jnp.int32(1) << (31 - j)), ukey)
        return alive, cnt_gt, ukey

    zero_row = jnp.zeros((1, blk), jnp.int32)
    alive, cnt_gt, ukey = lax.fori_loop(0, 32, bit_body, (alive0, zero_row, zero_row))
    ans = ukey ^ INT_MIN
    thr_bits = jnp.where(ans < 0, ans ^ 0x7FFFFFFF, ans)
    thr0 = lax.bitcast_convert_type(thr_bits, jnp.float32)

    def tile_sum(x):
        return jnp.sum(x.reshape(ch // SUBLANES, SUBLANES, blk), axis=0)

    def fold_rows(acc, combine):
        return functools.reduce(combine, [acc[i:i + 1] for i in range(SUBLANES)])

    def mask_pass(thr, p_max):
        def body(c, carry):
            a_gt, a_ge = carry
            r = pl.multiple_of(c * ch, ch)
            sc = sc_ref[pl.ds(r, ch), :]
            k_pos = r + crow_i
            gt = sc > thr
            ge = sc >= thr
            sel = (gt | (ge & (k_pos <= p_max))) & (k_pos <= q_pos)
            mb_ref[pl.ds(r, ch), :] = jnp.where(sel, 0.0, MASK_NEG).astype(mb_ref.dtype)
            return (a_gt + tile_sum(jnp.where(gt, 1, 0).astype(jnp.int32)),
                    a_ge + tile_sum(jnp.where(ge, 1, 0).astype(jnp.int32)))
        zero = jnp.zeros((SUBLANES, blk), jnp.int32)
        a_gt, a_ge = lax.fori_loop(0, n_chunks, body, (zero, zero))
        return fold_rows(a_gt, jnp.add), fold_rows(a_ge, jnp.add)

    def nearest(pred_fn, fill, combine):
        def body(c, acc):
            r = pl.multiple_of(c * ch, ch)
            sc = sc_ref[pl.ds(r, ch), :]
            part = jnp.where(pred_fn(sc), sc, fill).reshape(ch // SUBLANES, SUBLANES, blk)
            return combine(acc, functools.reduce(combine, [part[i] for i in range(ch // SUBLANES)]))
        acc = lax.fori_loop(0, n_chunks, body, jnp.full((SUBLANES, blk), fill, jnp.float32))
        return fold_rows(acc, combine)

    def status(c_gt, c_ge):
        off = (c_gt >= k_sel) | (c_ge < k_sel)
        return jnp.max(jnp.where(off, 2, 0) | jnp.where(c_ge > k_sel, 1, 0))

    def walk(state):
        thr, c_gt, c_ge, _, trips = state
        above = nearest(lambda sc: sc > thr, jnp.inf, jnp.minimum)
        below = nearest(lambda sc: sc < thr, -jnp.inf, jnp.maximum)
        thr = jnp.where(c_gt >= k_sel, above, jnp.where(c_ge < k_sel, below, thr))
        c_gt, c_ge = mask_pass(thr, no_bound)
        return thr, c_gt, c_ge, status(c_gt, c_ge), trips + 1

    no_bound = jnp.full((1, blk), 2 * seq, jnp.int32)
    cnt_gt0, cnt_ge0 = mask_pass(thr0, no_bound)
    thr, cnt_gt, _, flag, _ = lax.while_loop(
        lambda state: (state[3] >= 2) & (state[4] < seq), walk,
        (thr0, cnt_gt0, cnt_ge0, status(cnt_gt0, cnt_ge0), jnp.int32(0)))

    @pl.when(flag == 1)
    def _():
        need = k_sel - cnt_gt
        n_bits = (2 * seq - 1).bit_length()

        def tie_count(cand):
            def body(c, acc):
                r = pl.multiple_of(c * ch, ch)
                hit = (sc_ref[pl.ds(r, ch), :] == thr) & (r + crow_i < cand)
                return acc + tile_sum(jnp.where(hit, 1, 0).astype(jnp.int32))
            return fold_rows(lax.fori_loop(0, n_chunks, body, jnp.zeros((SUBLANES, blk), jnp.int32)), jnp.add)

        def pos_body(i, p):
            cand = p | (jnp.int32(1) << (n_bits - 1 - i))
            return jnp.where(tie_count(cand) <= need - 1, cand, p)
        p_max = lax.fori_loop(0, n_bits, pos_body, jnp.zeros((1, blk), jnp.int32))
        mask_pass(thr, p_max)

    eye = jnp.where(row_i == col_i, 1.0, 0.0).astype(MXU_DTYPE)
    for cc in range(DSA_KV_HEADS):
        for g in range(DSA_GROUPS):
            h = cc * DSA_GROUPS + g
            qaug_ref[cc, g * blk:(g + 1) * blk, 0:DSA_HD] = dq_ref[:, h * DSA_HD:(h + 1) * DSA_HD]
            qaug_ref[cc, g * blk:(g + 1) * blk, DSA_HD:2 * DSA_HD] = eye
    acc_ref[...] = jnp.zeros_like(acc_ref)
    for cc in range(DSA_KV_HEADS):
        ml_ref[cc, 0] = jnp.full(ml_ref.shape[2:], -jnp.inf, jnp.float32)
        ml_ref[cc, 1] = jnp.zeros(ml_ref.shape[2:], jnp.float32)

    def stage_logits(c, lg_ref, mx_ref, far):
        r = pl.multiple_of(c * ch, ch)
        mbc = mb_ref[pl.ds(r, ch), :]
        for cc in range(DSA_KV_HEADS):
            kaug = jnp.concatenate([dk_ref[pl.ds(r, ch), cc * DSA_HD:(cc + 1) * DSA_HD], mbc], axis=1)
            lg = _nt_dot(kaug, qaug_ref[cc])
            if far:
                off = bt_ref[2, cc, 0:1, :]
            else:
                lg = lg + jnp.concatenate(
                    [bt_ref[jnp.clip(qb - (c * BLOCKS_PER_CHUNK + j), 0, 2), cc]
                     for j in range(BLOCKS_PER_CHUNK)], axis=0)
                off = jnp.zeros((1, width), jnp.float32)
            lg_ref[cc] = lg
            mx_ref[cc, 0] = jnp.broadcast_to(jnp.max(lg, axis=0, keepdims=True) + off, mx_ref.shape[2:])
            mx_ref[cc, 1] = jnp.broadcast_to(off, mx_ref.shape[2:])

    def stage_softmax(c, lg_ref, mx_ref):
        for cc in range(DSA_KV_HEADS):
            m = ml_ref[cc, 0, 0:1, :]
            l = ml_ref[cc, 1, 0:1, :]
            m_new = jnp.maximum(m, mx_ref[cc, 0, 0:1, :])
            alpha = jnp.exp2(m - m_new)
            p = jnp.exp2(lg_ref[cc] - (m_new - mx_ref[cc, 1, 0:1, :]))
            pv = _dot(vt_ref[c, cc * VT_ROWS:(cc + 1) * VT_ROWS, :], p.astype(MXU_DTYPE))
            l_new = alpha * l + pv[DSA_HD:DSA_HD + 1, :]
            acc_ref[cc] = acc_ref[cc] * alpha + pv[:DSA_HD]
            ml_ref[cc, 0] = jnp.broadcast_to(m_new, ml_ref.shape[2:])
            ml_ref[cc, 1] = jnp.broadcast_to(l_new, ml_ref.shape[2:])

    c_near = jnp.maximum(qb - 1, 0) // BLOCKS_PER_CHUNK

    @pl.when(c_near > 0)
    def _():
        stage_logits(0, lga_ref, mxa_ref, far=True)

    @pl.when(c_near == 0)
    def _():
        stage_logits(0, lga_ref, mxa_ref, far=False)

    def pair_body(pi, carry, far):
        c0 = 2 * pi
        stage_logits(c0 + 1, lgb_ref, mxb_ref, far)
        stage_softmax(c0, lga_ref, mxa_ref)

        stage_logits(jnp.minimum(c0 + 2, n_chunks - 1), lga_ref, mxa_ref, far)
        stage_softmax(c0 + 1, lgb_ref, mxb_ref)
        return carry

    far_pairs = jnp.maximum(c_near - 1, 0) // 2
    lax.fori_loop(0, far_pairs, functools.partial(pair_body, far=True), 0)
    lax.fori_loop(far_pairs, n_chunks // 2, functools.partial(pair_body, far=False), 0)

    @pl.when(n_chunks % 2 == 1)
    def _():
        stage_softmax(n_chunks - 1, lga_ref, mxa_ref)

    for cc in range(DSA_KV_HEADS):
        o_t = acc_ref[cc] / ml_ref[cc, 1, 0:1, :]
        for g in range(DSA_GROUPS):
            h = cc * DSA_GROUPS + g
            o = o_t[:, g * blk:(g + 1) * blk].T
            zz = dz_ref[:, h * DSA_HD:(h + 1) * DSA_HD].astype(jnp.float32)
            out_ref[:, h * DSA_HD:(h + 1) * DSA_HD] = (o * _silu(zz)).astype(out_ref.dtype)


def _dsa(u, ikd, small, rel_bias, batch, seq):
    nb = seq // Q_BLOCK
    hq = DSA_HEADS * DSA_HD
    hkv = DSA_KV_HEADS * DSA_HD
    hi = IDX_HEADS * IDX_DIM

    def qmap(cb):
        return lambda b, q: (b * nb + q, cb)

    def bmap(cb):
        return lambda b, q: (b, cb)

    return pl.pallas_call(
        functools.partial(_dsa_kernel, seq=seq),
        out_shape=jax.ShapeDtypeStruct((batch * seq, hq), ACT_DTYPE),
        grid=(batch, nb),
        in_specs=[
            pl.BlockSpec(memory_space=pltpu.SMEM),
            pl.BlockSpec((Q_BLOCK, hq), qmap(COL_DQ // hq)),
            pl.BlockSpec((seq, hkv), bmap(COL_DK // hkv)),
            pl.BlockSpec((seq, hkv), bmap(COL_DV // hkv)),
            pl.BlockSpec((Q_BLOCK, hi), qmap(COL_IQ // hi)),
            pl.BlockSpec((seq, LANES), bmap(0)),
            pl.BlockSpec((Q_BLOCK, LANES), qmap(0)),
            pl.BlockSpec((Q_BLOCK, hq), qmap(COL_DZ // hq)),
        ],
        out_specs=pl.BlockSpec((Q_BLOCK, hq), qmap(0)),
        scratch_shapes=[
            pltpu.VMEM((seq, Q_BLOCK), jnp.int32),
            pltpu.VMEM((seq, Q_BLOCK), jnp.float32),
            pltpu.VMEM((32, seq // 32, Q_BLOCK), jnp.int32),
            pltpu.VMEM((seq, Q_BLOCK), MXU_DTYPE),
            pltpu.VMEM((seq // KEY_CHUNK, DSA_KV_HEADS * VT_ROWS, KEY_CHUNK), MXU_DTYPE),
            pltpu.VMEM((3, DSA_KV_HEADS, Q_BLOCK, DSA_GROUPS * Q_BLOCK), jnp.float32),
            pltpu.VMEM((IDX_HEADS // 2, 2 * Q_BLOCK, LANES), MXU_DTYPE),
            pltpu.VMEM((DSA_KV_HEADS, DSA_GROUPS * Q_BLOCK, 2 * DSA_HD), MXU_DTYPE),
            pltpu.VMEM((DSA_KV_HEADS, DSA_HD, DSA_GROUPS * Q_BLOCK), jnp.float32),
            pltpu.VMEM((DSA_KV_HEADS, KEY_CHUNK, DSA_GROUPS * Q_BLOCK), jnp.float32),
            pltpu.VMEM((DSA_KV_HEADS, KEY_CHUNK, DSA_GROUPS * Q_BLOCK), jnp.float32),
            pltpu.VMEM((DSA_KV_HEADS, 2, SUBLANES, DSA_GROUPS * Q_BLOCK), jnp.float32),
            pltpu.VMEM((DSA_KV_HEADS, 2, SUBLANES, DSA_GROUPS * Q_BLOCK), jnp.float32),
            pltpu.VMEM((DSA_KV_HEADS, 2, SUBLANES, DSA_GROUPS * Q_BLOCK), jnp.float32),
        ],
        compiler_params=pltpu.CompilerParams(
            dimension_semantics=("arbitrary", "arbitrary"),
            vmem_limit_bytes=VMEM_LIMIT),
        name="dsa",
    )(rel_bias, u, u, u, u, ikd, small, u)


def _merge_kernel(x_ref, yg_ref, yd_ref, xq_ref, xz_ref, sg_ref, sd_ref, sm_ref, mem_ref, gm_ref, wkv_ref,
                  wg_ref, wd_ref, wx_ref, wo_ref, gp_ref, o_ref, ym_ref, mkv_ref, *, steps_per_batch):
    hw = X_HEADS * X_HD
    heads = range(X_HEADS)

    @pl.when(pl.program_id(0) % steps_per_batch == 0)
    def _():
        mf = mem_ref[...]
        mn = mf * lax.rsqrt(jnp.mean(mf * mf, axis=-1, keepdims=True) + EPS)
        mkv_ref[...] = _dot((mn * gm_ref[...]).astype(MXU_DTYPE), wkv_ref[...]).astype(mkv_ref.dtype)

    def cs(h):
        return slice(h * X_HD, (h + 1) * X_HD)

    lgs = [_nt_dot(xq_ref[:, cs(h)], mkv_ref[:, cs(h)]) for h in heads]
    t_g = _dot(yg_ref[...], wg_ref[...])
    ps = []
    for lg in lgs:
        e = jnp.exp(lg - jnp.max(lg, axis=-1, keepdims=True))
        ps.append((e / jnp.sum(e, axis=-1, keepdims=True)).astype(MXU_DTYPE))
    outs = [_dot(ps[h], mkv_ref[:, hw + h * X_HD:hw + (h + 1) * X_HD]) for h in heads]
    t_d = _dot(yd_ref[...], wd_ref[...])
    for h in heads:
        ym_ref[:, cs(h)] = (outs[h] * _silu(xz_ref[:, cs(h)].astype(jnp.float32))).astype(ym_ref.dtype)
    merged = jax.nn.sigmoid(sg_ref[...].astype(jnp.float32)) * t_g
    merged = merged + jax.nn.sigmoid(sd_ref[...].astype(jnp.float32)) * t_d
    merged = merged + jax.nn.sigmoid(sm_ref[...].astype(jnp.float32)) * _dot(ym_ref[...], wx_ref[...])
    t = _dot(merged.astype(MXU_DTYPE), wo_ref[...])
    y = t * lax.rsqrt(jnp.mean(t * t, axis=-1, keepdims=True) + EPS)
    o_ref[...] = x_ref[...] + y * gp_ref[...]


def _merge(x2, y_gla, y_dsa, u, mem2, g_mem, w_kv, w_g, w_d, w_x, w_o, g_post, seq, tm):
    m = x2.shape[0]
    steps_per_batch = seq // tm
    d = D_MODEL

    def rmap(cb):
        return lambda i: (i, cb)

    wspec = pl.BlockSpec((d, d), lambda i: (0, 0))
    return pl.pallas_call(
        functools.partial(_merge_kernel, steps_per_batch=steps_per_batch),
        out_shape=jax.ShapeDtypeStruct((m, d), jnp.float32),
        grid=(m // tm,),
        in_specs=[
            pl.BlockSpec((tm, d), rmap(0)),
            pl.BlockSpec((tm, d), rmap(0)),
            pl.BlockSpec((tm, d), rmap(0)),
            pl.BlockSpec((tm, d), rmap(COL_XQ // d)),
            pl.BlockSpec((tm, d), rmap(COL_XZ // d)),
            pl.BlockSpec((tm, d), rmap(COL_SG // d)),
            pl.BlockSpec((tm, d), rmap(COL_SD // d)),
            pl.BlockSpec((tm, d), rmap(COL_SM // d)),
            pl.BlockSpec((N_MEM, d), lambda i: (i // steps_per_batch, 0)),
            pl.BlockSpec((1, d), lambda i: (0, 0)),
            pl.BlockSpec((d, 2 * X_HEADS * X_HD), lambda i: (0, 0)),
            wspec, wspec, wspec, wspec,
            pl.BlockSpec((1, d), lambda i: (0, 0)),
        ],
        out_specs=pl.BlockSpec((tm, d), rmap(0)),
        scratch_shapes=[pltpu.VMEM((tm, d), MXU_DTYPE),
                        pltpu.VMEM((N_MEM, 2 * X_HEADS * X_HD), MXU_DTYPE)],
        compiler_params=pltpu.CompilerParams(
            dimension_semantics=("arbitrary",),
            vmem_limit_bytes=VMEM_LIMIT),
        name="merge",
    )(x2, y_gla, y_dsa, u, u, u, u, u, mem2, g_mem, w_kv, w_g, w_d, w_x, w_o, g_post)


_MAIN_GROUPS = (0, 1, 2, 4, 5, 11, 12, 13, 14, 6, 7, 8)
_GROUP_GA, _GROUP_IK, _GROUP_IW = 3, 9, 10
RELAYOUT_COLS = 128


def _relayout_kernel(wt_ref, main_ref, small_ref):
    offs = np.concatenate([[0], np.cumsum(np.array(SPLIT_SIZES))]).tolist()

    def group(g):
        return wt_ref[offs[g]:offs[g + 1], :].astype(MXU_DTYPE)

    dst = 0
    for g in _MAIN_GROUPS:
        main_ref[dst:dst + SPLIT_SIZES[g], :] = group(g)
        dst += SPLIT_SIZES[g]
    pad = jnp.zeros((LANES - GLA_RANK - IDX_HEADS, wt_ref.shape[1]), MXU_DTYPE)
    small_ref[...] = jnp.concatenate(
        [group(_GROUP_IK), group(_GROUP_IK), group(_GROUP_GA), group(_GROUP_IW), pad], axis=0)


def _relayout_w_in(w_in):
    wt = w_in.T
    n, d = wt.shape
    return pl.pallas_call(
        _relayout_kernel,
        out_shape=(jax.ShapeDtypeStruct((U_COLS, d), MXU_DTYPE),
                   jax.ShapeDtypeStruct((2 * LANES, d), MXU_DTYPE)),
        grid=(d // RELAYOUT_COLS,),
        in_specs=[pl.BlockSpec((n, RELAYOUT_COLS), lambda i: (0, i))],
        out_specs=(pl.BlockSpec((U_COLS, RELAYOUT_COLS), lambda i: (0, i)),
                   pl.BlockSpec((2 * LANES, RELAYOUT_COLS), lambda i: (0, i))),
        compiler_params=pltpu.CompilerParams(
            dimension_semantics=("arbitrary",),
            vmem_limit_bytes=VMEM_LIMIT),
        name="relayout",
    )(wt)


def _col_scale():
    s = np.ones((1, U_COLS), np.float32)
    s[:, COL_GQ:COL_GQ + GLA_HEADS * GLA_DK] = GLA_DK ** -0.5
    s[:, COL_DQ:COL_DQ + DSA_HEADS * DSA_HD] = DSA_HD ** -0.5 * LOG2_E
    s[:, COL_XQ:COL_XQ + X_HEADS * X_HD] = X_HD ** -0.5
    return jnp.asarray(s)


def _layer(x2, mem2, g_pre, g_post, g_mem, w_in, w_up, b_a, g_gla, rel_bias, w_kv,
           w_g, w_d, w_x, w_o, batch, seq):
    w_main, w_small = _relayout_w_in(w_in)
    tm = min(PROJ_TM, batch * seq)
    u, ikd, small = _proj(x2, g_pre.reshape(1, -1), w_main, w_small, _col_scale(), tm, PROJ_TN)
    y_gla = _gla(u, small, w_up.astype(MXU_DTYPE), b_a.reshape(1, -1), g_gla.reshape(1, -1),
                 batch, seq, chunks_per_step=4)
    y_dsa = _dsa(u, ikd, small, rel_bias, batch, seq)
    return _merge(x2, y_gla, y_dsa, u, mem2, g_mem.reshape(1, -1), w_kv.astype(MXU_DTYPE), w_g.astype(MXU_DTYPE), w_d.astype(MXU_DTYPE),
                  w_x.astype(MXU_DTYPE), w_o.astype(MXU_DTYPE), g_post.reshape(1, -1), seq, MERGE_TM)


def kernel(x, mem, g_pre, g_post, g_mem, w_in, w_gla_a_up, b_gla_a, g_gla, rel_bias,
           w_mem_kv, w_gla_out, w_dsa_out, w_x_out, w_o):
    batch, seq, d = x.shape
    x2 = x.reshape(batch * seq, d)
    mem2 = mem.reshape(batch * N_MEM, d)
    for i in range(g_pre.shape[0]):
        x2 = _layer(x2, mem2, g_pre[i], g_post[i], g_mem[i], w_in[i], w_gla_a_up[i],
                    b_gla_a[i], g_gla[i], rel_bias, w_mem_kv[i], w_gla_out[i],
                    w_dsa_out[i], w_x_out[i], w_o[i], batch, seq)
    return x2.reshape(batch, seq, d)
```
